```python
import jax, jax.numpy as jnp
from jax import lax
import numpy as np

D_MODEL = 2048
BATCH = 8
SEQ = 2048
DEPTH = 2

HEAD_DIM = 128
A_HEADS_PER_GROUP = 4
DILATED_GROUPS = ((128, 1), (512, 4), (2048, 16))
A_HEADS = A_HEADS_PER_GROUP * len(DILATED_GROUPS)
A_WIDTH = A_HEADS * HEAD_DIM
A_OUT = A_HEADS_PER_GROUP * HEAD_DIM
ATT_BLOCK = 128
ROPE_THETA = 10000.0
GLA_HEADS = 4
GLA_DK = 128
GLA_DV = 256
GLA_KEY = GLA_HEADS * GLA_DK
GLA_VAL = GLA_HEADS * GLA_DV
GLA_LOWRANK = 16
GLA_TAU = 16.0
GLA_CHUNK = 64
IN_WIDTHS = (A_WIDTH, A_WIDTH, A_WIDTH, GLA_KEY, GLA_KEY, GLA_VAL, GLA_VAL, GLA_LOWRANK, D_MODEL, D_MODEL)
IN_WIDTH = sum(IN_WIDTHS)
N_EXPERTS = 16
N_GROUPS = 4
EXPERTS_PER_GROUP = N_EXPERTS // N_GROUPS
TOPK_GROUPS = 1
TOP_K = 2
D_FF_EXPERT = 1024
EXPERT_BLOCK = 128
NORM_EPS = 1e-6

kernel_name = 'hybrid_dilated_gla_grouped_moe_adaln'


def rms_norm(x, gain):
    x32 = x.astype(jnp.float32)
    y = x32 * lax.rsqrt(jnp.mean(x32 * x32, axis=-1, keepdims=True) + NORM_EPS)
    return (y * gain.astype(jnp.float32)).astype(x.dtype)


def rope_tables(seq):
    pos = jnp.arange(seq, dtype=jnp.float32)
    inv_freq = ROPE_THETA ** (-jnp.arange(0, HEAD_DIM, 2, dtype=jnp.float32) / HEAD_DIM)
    ang = pos[:, None] * inv_freq[None, :]
    return jnp.cos(ang), jnp.sin(ang)


def apply_rope(t, cos, sin):
    t1, t2 = jnp.split(t, 2, axis=-1)
    c = cos[None, :, None, :]
    s = sin[None, :, None, :]
    return jnp.concatenate([t1 * c - t2 * s, t2 * c + t1 * s], axis=-1)


def dilated_window_attention(q, k, v, window, dilation):
    B, S, H, Dh = q.shape
    L = S // dilation
    nb = -(-L // ATT_BLOCK)
    Lp = nb * ATT_BLOCK

    def to_sub(t):
        return t.reshape(B, L, dilation, H, Dh).transpose(0, 2, 3, 1, 4)

    qs = jnp.pad(to_sub(q), ((0, 0), (0, 0), (0, 0), (0, Lp - L), (0, 0)))
    qs = qs.reshape(B, dilation, H, nb, ATT_BLOCK, Dh)
    kp = jnp.pad(to_sub(k), ((0, 0), (0, 0), (0, 0), (ATT_BLOCK, Lp - L), (0, 0)))
    vp = jnp.pad(to_sub(v), ((0, 0), (0, 0), (0, 0), (ATT_BLOCK, Lp - L), (0, 0)))

    def band(t):
        prev = t[:, :, :, :Lp].reshape(B, dilation, H, nb, ATT_BLOCK, Dh)
        cur = t[:, :, :, ATT_BLOCK:].reshape(B, dilation, H, nb, ATT_BLOCK, Dh)
        return jnp.concatenate([prev, cur], axis=-2)

    kb, vb = band(kp), band(vp)
    s = jnp.einsum('brhnid,brhnjd->brhnij', qs, kb) * (Dh ** -0.5)
    i = jnp.arange(ATT_BLOCK)[:, None]
    j = jnp.arange(2 * ATT_BLOCK)[None, :]
    dist = ATT_BLOCK + i - j
    kpos = (jnp.arange(nb)[:, None, None] - 1) * ATT_BLOCK + j[None]
    mask = (dist >= 0) & (dist <= window // dilation) & (kpos >= 0)
    s = jnp.where(mask, s, -jnp.inf)
    m = jnp.max(s, axis=-1, keepdims=True)
    p = jnp.exp(s - m)
    den = jnp.sum(p, axis=-1)
    o = jnp.einsum('brhnij,brhnjd->brhnid', p, vb) / den[..., None]
    lse = m[..., 0] + jnp.log(den)
    o = o.reshape(B, dilation, H, Lp, Dh)[:, :, :, :L].transpose(0, 3, 1, 2, 4).reshape(B, S, H, Dh)
    lse = lse.reshape(B, dilation, H, Lp)[:, :, :, :L].transpose(0, 3, 1, 2).reshape(B, S, H)
    return o, lse


def gated_linear_attention(q, k, v, log_a):
    B, S, H, Dk = q.shape
    Dv = v.shape[-1]
    N = S // GLA_CHUNK

    def to_chunks(t):
        return t.reshape(B, N, GLA_CHUNK, H, t.shape[-1]).transpose(0, 3, 1, 2, 4)

    q, k, v, log_a = to_chunks(q), to_chunks(k), to_chunks(v), to_chunks(log_a)
    q = q * (Dk ** -0.5)
    b = jnp.cumsum(log_a, axis=-2)
    q_dec = q * jnp.exp(b)
    k_inv = k * jnp.exp(-b)
    causal = jnp.tril(jnp.ones((GLA_CHUNK, GLA_CHUNK), dtype=bool))
    attn = jnp.where(causal, jnp.einsum('bhnid,bhnjd->bhnij', q_dec, k_inv), 0.0)
    o_intra = jnp.einsum('bhnij,bhnjv->bhniv', attn, v)
    b_end = b[..., -1, :]
    k_end = k * jnp.exp(b_end[..., None, :] - b)
    d_state = jnp.einsum('bhncd,bhncv->bhndv', k_end, v)

    def step(state, inp):
        decay, ds = inp
        return decay[..., None] * state + ds, state

    _, s_prev = lax.scan(step, jnp.zeros((B, H, Dk, Dv), jnp.float32),
                         (jnp.moveaxis(jnp.exp(b_end), 2, 0), jnp.moveaxis(d_state, 2, 0)))
    s_prev = jnp.moveaxis(s_prev, 0, 2)
    o = o_intra + jnp.einsum('bhnid,bhndv->bhniv', q_dec, s_prev)
    return o.transpose(0, 2, 3, 1, 4).reshape(B, S, H, Dv)


def token_mixer(h, w_in, w_alpha2, b_alpha2, gla_gain, w_out_a, w_out_b, w_out, cos, sin):
    B, S, _ = h.shape
    f32 = jnp.float32
    proj = h @ w_in
    splits = np.cumsum(IN_WIDTHS)[:-1].tolist()
    q_a, k_a, v_a, q_b, k_b, v_b, r_b, a_low, g_a, g_b = jnp.split(proj, splits, axis=-1)

    q_a = apply_rope(q_a.astype(f32).reshape(B, S, A_HEADS, HEAD_DIM), cos, sin)
    k_a = apply_rope(k_a.astype(f32).reshape(B, S, A_HEADS, HEAD_DIM), cos, sin)
    v_a = v_a.astype(f32).reshape(B, S, A_HEADS, HEAD_DIM)
    outs, lses = [], []
    for gi, (window, dilation) in enumerate(DILATED_GROUPS):
        hs = slice(gi * A_HEADS_PER_GROUP, (gi + 1) * A_HEADS_PER_GROUP)
        o, lse = dilated_window_attention(q_a[:, :, hs], k_a[:, :, hs], v_a[:, :, hs], window, dilation)
        outs.append(o)
        lses.append(lse)
    wts = jax.nn.softmax(jnp.stack(lses), axis=0)
    y_a = jnp.einsum('gbsh,gbshd->bshd', wts, jnp.stack(outs)).reshape(B, S, A_OUT).astype(h.dtype)

    log_a = jax.nn.log_sigmoid((a_low @ w_alpha2 + b_alpha2).astype(f32)) / GLA_TAU
    o_b = gated_linear_attention(q_b.astype(f32).reshape(B, S, GLA_HEADS, GLA_DK),
                                 k_b.astype(f32).reshape(B, S, GLA_HEADS, GLA_DK),
                                 v_b.astype(f32).reshape(B, S, GLA_HEADS, GLA_DV),
                                 log_a.reshape(B, S, GLA_HEADS, GLA_DK))
    o_b = o_b * lax.rsqrt(jnp.mean(o_b * o_b, axis=-1, keepdims=True) + NORM_EPS)
    o_b = o_b * gla_gain.astype(f32).reshape(GLA_HEADS, GLA_DV)
    y_b = (o_b.reshape(B, S, GLA_VAL) * jax.nn.silu(r_b.astype(f32))).astype(h.dtype)

    merged = jax.nn.sigmoid(g_a) * (y_a @ w_out_a) + jax.nn.sigmoid(g_b) * (y_b @ w_out_b)
    return merged @ w_out


def route(h, w_router, b_router):
    T = h.shape[0]
    aff = jax.nn.sigmoid(h.astype(jnp.float32) @ w_router.astype(jnp.float32))
    sel = aff + b_router.astype(jnp.float32)
    grp_score = lax.top_k(sel.reshape(T, N_GROUPS, EXPERTS_PER_GROUP), 2)[0].sum(-1)
    _, top_grp = lax.top_k(grp_score, TOPK_GROUPS)
    grp_mask = jnp.sum(jax.nn.one_hot(top_grp, N_GROUPS, dtype=jnp.float32), axis=1) > 0
    expert_mask = jnp.repeat(grp_mask, EXPERTS_PER_GROUP, axis=1)
    _, idx = lax.top_k(jnp.where(expert_mask, sel, -jnp.inf), TOP_K)
    w = jnp.take_along_axis(aff, idx, axis=1)
    return w / jnp.sum(w, axis=-1, keepdims=True), idx


def moe_ffn(h, w_router, b_router, w_gate_e, w_up_e, w_down_e):
    T, D = h.shape
    gate, eidx = route(h, w_router, b_router)
    TK = T * TOP_K
    flat_e = eidx.reshape(-1)
    flat_g = gate.reshape(-1)
    order = jnp.argsort(flat_e)
    se = flat_e[order]
    tok = order // TOP_K
    counts = jnp.bincount(flat_e, length=N_EXPERTS)
    pcounts = (counts + EXPERT_BLOCK - 1) // EXPERT_BLOCK * EXPERT_BLOCK
    pend = jnp.cumsum(pcounts)
    pstart = pend - pcounts
    start = jnp.cumsum(counts) - counts
    dest = pstart[se] + jnp.arange(TK) - start[se]
    n_blk = TK // EXPERT_BLOCK + N_EXPERTS
    buf = jnp.zeros((n_blk * EXPERT_BLOCK, D), h.dtype).at[dest].set(h[tok])
    blk_e = jnp.minimum(jnp.searchsorted(pend, jnp.arange(n_blk) * EXPERT_BLOCK, side='right'),
                        N_EXPERTS - 1)

    def expert_block(args):
        xb, e = args
        return (jax.nn.silu(xb @ w_gate_e[e]) * (xb @ w_up_e[e])) @ w_down_e[e]

    ybuf = lax.map(expert_block, (buf.reshape(n_blk, EXPERT_BLOCK, D), blk_e)).reshape(-1, D)
    y = ybuf[dest] * flat_g[order][:, None].astype(h.dtype)
    return jnp.zeros_like(h).at[tok].add(y)


def setup_inputs(seed: int = 0) -> dict:
    key = jax.random.key(seed)
    ks = jax.random.split(key, 20)
    f32 = jnp.float32
    D = D_MODEL

    def nrm(k, shape, scale):
        return jax.random.normal(k, shape, f32) * scale

    return {
        'x': nrm(ks[0], (BATCH, SEQ, D), 1.0),
        'c': nrm(ks[1], (BATCH, D), 1.0),
        'w_ada': nrm(ks[2], (DEPTH, D, 6 * D), 0.5 * D ** -0.5),
        'b_ada': nrm(ks[3], (DEPTH, 6 * D), 0.02),
        'norm_mix': 1.0 + nrm(ks[4], (DEPTH, D), 0.02),
        'norm_ffn': 1.0 + nrm(ks[5], (DEPTH, D), 0.02),
        'w_in': nrm(ks[6], (DEPTH, D, IN_WIDTH), D ** -0.5),
        'w_alpha2': nrm(ks[7], (DEPTH, GLA_LOWRANK, GLA_KEY), GLA_LOWRANK ** -0.5),
        'b_alpha2': nrm(ks[8], (DEPTH, GLA_KEY), 0.1),
        'gla_gain': 1.0 + nrm(ks[9], (DEPTH, GLA_VAL), 0.02),
        'w_out_a': nrm(ks[10], (DEPTH, A_OUT, D), A_OUT ** -0.5),
        'w_out_b': nrm(ks[11], (DEPTH, GLA_VAL, D), GLA_VAL ** -0.5),
        'w_out': nrm(ks[12], (DEPTH, D, D), D ** -0.5),
        'w_router': nrm(ks[13], (D, N_EXPERTS), D ** -0.5),
        'b_router': nrm(ks[14], (N_EXPERTS,), 0.01),
        'w_gate_e': nrm(ks[15], (DEPTH, N_EXPERTS, D, D_FF_EXPERT), D ** -0.5),
        'w_up_e': nrm(ks[16], (DEPTH, N_EXPERTS, D, D_FF_EXPERT), D ** -0.5),
        'w_down_e': nrm(ks[17], (DEPTH, N_EXPERTS, D_FF_EXPERT, D), D_FF_EXPERT ** -0.5),
        'final_norm': 1.0 + nrm(ks[18], (D,), 0.02),
    }


def reference(x, c, w_ada, b_ada, norm_mix, norm_ffn, w_in, w_alpha2, b_alpha2, gla_gain,
              w_out_a, w_out_b, w_out, w_router, b_router, w_gate_e, w_up_e, w_down_e, final_norm):
    B, S, D = x.shape
    cos, sin = rope_tables(S)
    c_act = jax.nn.silu(c)
    for layer in range(DEPTH):
        ada = c_act @ w_ada[layer] + b_ada[layer]
        shift1, scale1, gate1, shift2, scale2, gate2 = jnp.split(ada[:, None, :], 6, axis=-1)
        h = rms_norm(x, norm_mix[layer]) * (1 + scale1) + shift1
        x = x + gate1 * token_mixer(h, w_in[layer], w_alpha2[layer], b_alpha2[layer], gla_gain[layer],
                                    w_out_a[layer], w_out_b[layer], w_out[layer], cos, sin)
        h = rms_norm(x, norm_ffn[layer]) * (1 + scale2) + shift2
        y = moe_ffn(h.reshape(B * S, D), w_router, b_router,
                    w_gate_e[layer], w_up_e[layer], w_down_e[layer]).reshape(B, S, D)
        x = x + gate2 * y
    return rms_norm(x, final_norm)
```

```python
import functools

import jax
import jax.numpy as jnp
from jax import lax
from jax.experimental import pallas as pl
from jax.experimental.pallas import tpu as pltpu

F32 = jnp.float32
BF16 = jnp.bfloat16

D_MODEL = 2048
SEQ = 2048
HEAD_DIM = 128
A_HEADS_PER_GROUP = 4
DILATED_GROUPS = ((128, 1), (512, 4), (2048, 16))
A_WIDTH = 12 * HEAD_DIM
A_OUT = A_HEADS_PER_GROUP * HEAD_DIM
ATT_BLOCK = 128
ROPE_THETA = 10000.0
GLA_HEADS = 4
GLA_DK = 128
GLA_DV = 256
GLA_KEY = GLA_HEADS * GLA_DK
GLA_VAL = GLA_HEADS * GLA_DV
GLA_LOWRANK = 16
GLA_TAU = 16.0
GLA_CHUNK = 64
N_EXPERTS = 16
N_GROUPS = 4
EXPERTS_PER_GROUP = N_EXPERTS // N_GROUPS
TOP_K = 2
D_FF_EXPERT = 1024
NORM_EPS = 1e-6

LANES = 128
MOE_BLOCK = 256
GROUP_COLS = 3 * A_OUT

COL_GA = 0
COL_GB = D_MODEL
COL_VB = 2 * D_MODEL
COL_RB = COL_VB + GLA_VAL
COL_A = COL_RB + GLA_VAL
COL_QB = COL_A + 3 * GROUP_COLS
COL_KB = COL_QB + GLA_KEY
PROJ_WIDTH = COL_KB + GLA_KEY


def _params(vmem_mib, sem=None):
    kw = dict(vmem_limit_bytes=int(vmem_mib) << 20)
    if sem is not None:
        kw["dimension_semantics"] = sem
    return pltpu.CompilerParams(**kw)


def _sigmoid(x):
    return 1.0 / (1.0 + jnp.exp(-x))


def _ada_kernel(c_ref, w_ref, b_ref, o_ref):
    c = c_ref[...]
    ca = (c * _sigmoid(c)).astype(BF16)
    o_ref[0] = jnp.dot(ca, w_ref[0].astype(BF16), preferred_element_type=F32) + b_ref[0]


def _ada(c, w_ada, b_ada):
    depth, d, n = w_ada.shape
    b = c.shape[0]
    tn = 1024
    return pl.pallas_call(
        _ada_kernel,
        out_shape=jax.ShapeDtypeStruct((depth, b, n), F32),
        grid=(depth, n // tn),
        in_specs=[
            pl.BlockSpec((b, d), lambda l, j: (0, 0)),
            pl.BlockSpec((1, d, tn), lambda l, j: (l, 0, j)),
            pl.BlockSpec((1, 1, tn), lambda l, j: (l, 0, j)),
        ],
        out_specs=pl.BlockSpec((1, b, tn), lambda l, j: (l, 0, j)),
        compiler_params=_params(40),
        name="ada",
    )(c, w_ada, b_ada.reshape(depth, 1, n))


def _norm_mod_kernel(x_ref, ada_ref, gain_ref, wal_ref, h_ref, al_ref):
    x = x_ref[...]
    y = x * lax.rsqrt(jnp.mean(x * x, axis=-1, keepdims=True) + NORM_EPS) * gain_ref[...]
    h = (y * (1.0 + ada_ref[0, 1:2, :]) + ada_ref[0, 0:1, :]).astype(BF16)
    h_ref[...] = h
    al_ref[...] = jnp.dot(h, wal_ref[...], preferred_element_type=F32)


def _norm_mod(x2, ada_l, gain, w_alow):
    t, d = x2.shape
    tm = 512
    per_b = SEQ // tm
    return pl.pallas_call(
        _norm_mod_kernel,
        out_shape=(jax.ShapeDtypeStruct((t, d), BF16), jax.ShapeDtypeStruct((t, LANES), F32)),
        grid=(t // tm,),
        in_specs=[
            pl.BlockSpec((tm, d), lambda i: (i, 0)),
            pl.BlockSpec((1, 6, d), lambda i: (i // per_b, 0, 0)),
            pl.BlockSpec((1, d), lambda i: (0, 0)),
            pl.BlockSpec((d, LANES), lambda i: (0, 0)),
        ],
        out_specs=(pl.BlockSpec((tm, d), lambda i: (i, 0)), pl.BlockSpec((tm, LANES), lambda i: (i, 0))),
        compiler_params=_params(40),
        name="norm_mod",
    )(x2, ada_l, gain, w_alow)


def _matmul_kernel(a_ref, b_ref, o_ref):
    o_ref[...] = jnp.dot(a_ref[...], b_ref[...], preferred_element_type=F32).astype(o_ref.dtype)


def _in_proj(h, w):
    t, d = h.shape
    n = w.shape[1]
    tm, tn = 2048, 512
    return pl.pallas_call(
        _matmul_kernel,
        out_shape=jax.ShapeDtypeStruct((t, n), BF16),
        grid=(t // tm, n // tn),
        in_specs=[pl.BlockSpec((tm, d), lambda i, j: (i, 0)), pl.BlockSpec((d, tn), lambda i, j: (0, j))],
        out_specs=pl.BlockSpec((tm, tn), lambda i, j: (i, j)),
        compiler_params=_params(48),
        name="in_proj",
    )(h, w)


def _attn_kernel(qkv_ref, cos_ref, sin_ref, o_ref, lse_ref, q_s, k_s, v_s, o_s, l_s, *, dil, nb, wd):
    scale = HEAD_DIM ** -0.5
    blk = ATT_BLOCK
    cosf = cos_ref[...]
    sinf = sin_ref[...]
    for h in range(A_HEADS_PER_GROUP):
        hc = slice(h * HEAD_DIM, (h + 1) * HEAD_DIM)
        tq = qkv_ref[:, hc].astype(F32)
        q_s[h] = (tq * cosf + pltpu.roll(tq, HEAD_DIM // 2, 1) * sinf) * scale
        tk = qkv_ref[:, A_OUT + h * HEAD_DIM:A_OUT + (h + 1) * HEAD_DIM].astype(F32)
        k_s[h] = tk * cosf + pltpu.roll(tk, HEAD_DIM // 2, 1) * sinf
        v_s[h] = qkv_ref[:, 2 * A_OUT + h * HEAD_DIM:2 * A_OUT + (h + 1) * HEAD_DIM].astype(F32)

    def rows(start, size):
        return pl.ds(start, size) if dil == 1 else pl.ds(start, size, stride=dil)

    i1 = lax.broadcasted_iota(jnp.int32, (blk, blk), 0)
    j1 = lax.broadcasted_iota(jnp.int32, (blk, blk), 1)
    mask_first = ((i1 - j1) >= 0) & ((i1 - j1) <= wd)
    i2 = lax.broadcasted_iota(jnp.int32, (blk, 2 * blk), 0)
    j2 = lax.broadcasted_iota(jnp.int32, (blk, 2 * blk), 1)
    dist = blk + i2 - j2
    mask_later = (dist >= 0) & (dist <= wd)
    lane = lax.broadcasted_iota(jnp.int32, (blk, LANES), 1)

    def attend(qs, ks, nk, mask):
        lse_tile = jnp.zeros((blk, LANES), F32)
        for h in range(A_HEADS_PER_GROUP):
            q = q_s[h, rows(qs, blk), :].astype(BF16)
            k = k_s[h, rows(ks, nk), :].astype(BF16)
            v = v_s[h, rows(ks, nk), :].astype(BF16)
            s = lax.dot_general(q, k, (((1,), (1,)), ((), ())), preferred_element_type=F32)
            s = jnp.where(mask, s, -jnp.inf)
            m = jnp.max(s, axis=-1, keepdims=True)
            p = jnp.exp(s - m)
            den = jnp.sum(p, axis=-1, keepdims=True)
            o = jnp.dot(p.astype(BF16), v, preferred_element_type=F32) / den
            o_s[h, rows(qs, blk), :] = o
            lse_tile = jnp.where(lane == h, m + jnp.log(den), lse_tile)
        l_s[rows(qs, blk), :] = lse_tile

    def per_phase(ph, carry):
        attend(ph, ph, blk, mask_first)

        def per_blk(n, c):
            qs = ph + dil * blk * n
            ks = ph + dil * blk * (n - 1)
            if dil == 1:
                qs = pl.multiple_of(qs, blk)
                ks = pl.multiple_of(ks, blk)
            attend(qs, ks, 2 * blk, mask_later)
            return c

        lax.fori_loop(1, nb, per_blk, 0)
        return carry

    lax.fori_loop(0, dil, per_phase, 0)
    for h in range(A_HEADS_PER_GROUP):
        o_ref[:, h * HEAD_DIM:(h + 1) * HEAD_DIM] = o_s[h].astype(o_ref.dtype)
    lse_ref[...] = l_s[...]


def _attention(proj, cosf, sinf, group):
    window, dil = DILATED_GROUPS[group]
    t = proj.shape[0]
    nb = (SEQ // dil) // ATT_BLOCK
    col = COL_A // GROUP_COLS + group
    kern = functools.partial(_attn_kernel, dil=dil, nb=nb, wd=window // dil)
    return pl.pallas_call(
        kern,
        out_shape=(jax.ShapeDtypeStruct((t, A_OUT), BF16), jax.ShapeDtypeStruct((t, LANES), F32)),
        grid=(t // SEQ,),
        in_specs=[
            pl.BlockSpec((SEQ, GROUP_COLS), lambda b: (b, col)),
            pl.BlockSpec((SEQ, HEAD_DIM), lambda b: (0, 0)),
            pl.BlockSpec((SEQ, HEAD_DIM), lambda b: (0, 0)),
        ],
        out_specs=(pl.BlockSpec((SEQ, A_OUT), lambda b: (b, 0)), pl.BlockSpec((SEQ, LANES), lambda b: (b, 0))),
        scratch_shapes=[pltpu.VMEM((A_HEADS_PER_GROUP, SEQ, HEAD_DIM), F32) for _ in range(4)]
        + [pltpu.VMEM((SEQ, LANES), F32)],
        compiler_params=_params(48),
        name=f"dilated_attn_g{group}",
    )(proj, cosf, sinf)


def _gla_kernel(q_ref, k_ref, v_ref, r_ref, al_ref, wa_ref, ba_ref, gain_ref, y_ref, st_ref, *, rows):
    @pl.when(pl.program_id(1) == 0)
    def _():
        st_ref[...] = jnp.zeros_like(st_ref)

    ck = GLA_CHUNK
    z = jnp.dot(al_ref[...].astype(BF16), wa_ref[...], preferred_element_type=F32) + ba_ref[...]
    log_a = (jnp.minimum(z, 0.0) - jnp.log(1.0 + jnp.exp(-jnp.abs(z)))) * (1.0 / GLA_TAU)
    ri = lax.broadcasted_iota(jnp.int32, (ck, ck), 0)
    ci = lax.broadcasted_iota(jnp.int32, (ck, ck), 1)
    causal = ri >= ci
    tri = causal.astype(BF16)
    nt = (((1,), (1,)), ((), ()))
    for c in range(rows // ck):
        rs = slice(c * ck, (c + 1) * ck)
        for h in range(GLA_HEADS):
            hk = slice(h * GLA_DK, (h + 1) * GLA_DK)
            hv = slice(h * GLA_DV, (h + 1) * GLA_DV)
            la = log_a[rs, hk]
            la_hi = la.astype(BF16)
            la_lo = (la - la_hi.astype(F32)).astype(BF16)
            b = jnp.dot(tri, la_hi, preferred_element_type=F32) + jnp.dot(tri, la_lo, preferred_element_type=F32)
            q = q_ref[rs, hk].astype(F32) * (GLA_DK ** -0.5)
            k = k_ref[rs, hk].astype(F32)
            v = v_ref[rs, hv]
            q_dec = (q * jnp.exp(b)).astype(BF16)
            k_inv = (k * jnp.exp(-b)).astype(BF16)
            attn = lax.dot_general(q_dec, k_inv, nt, preferred_element_type=F32)
            attn = jnp.where(causal, attn, 0.0).astype(BF16)
            o = jnp.dot(attn, v, preferred_element_type=F32)
            b_end = b[ck - 1:ck, :]
            k_end = (k * jnp.exp(b_end - b)).astype(BF16)
            st = st_ref[h]
            o = o + lax.dot_general(q_dec, st.astype(BF16), nt, preferred_element_type=F32)
            v_t = v.astype(F32).T.astype(BF16)
            st_ref[h] = st * jnp.exp(b_end) + jnp.dot(v_t, k_end, preferred_element_type=F32)
            o = o * lax.rsqrt(jnp.mean(o * o, axis=-1, keepdims=True) + NORM_EPS) * gain_ref[:, hv]
            r = r_ref[rs, hv].astype(F32)
            y_ref[rs, hv] = (o * (r * _sigmoid(r))).astype(y_ref.dtype)


def _gla(proj, a_low, w_alpha2p, b_alpha2, gla_gain):
    t = proj.shape[0]
    rows = 256
    per_b = SEQ // rows
    kern = functools.partial(_gla_kernel, rows=rows)
    row = lambda b, s: b * per_b + s
    return pl.pallas_call(
        kern,
        out_shape=jax.ShapeDtypeStruct((t, GLA_VAL), BF16),
        grid=(t // SEQ, per_b),
        in_specs=[
            pl.BlockSpec((rows, GLA_KEY), lambda b, s: (row(b, s), COL_QB // GLA_KEY)),
            pl.BlockSpec((rows, GLA_KEY), lambda b, s: (row(b, s), COL_KB // GLA_KEY)),
            pl.BlockSpec((rows, GLA_VAL), lambda b, s: (row(b, s), COL_VB // GLA_VAL)),
            pl.BlockSpec((rows, GLA_VAL), lambda b, s: (row(b, s), COL_RB // GLA_VAL)),
            pl.BlockSpec((rows, LANES), lambda b, s: (row(b, s), 0)),
            pl.BlockSpec((LANES, GLA_KEY), lambda b, s: (0, 0)),
            pl.BlockSpec((1, GLA_KEY), lambda b, s: (0, 0)),
            pl.BlockSpec((1, GLA_VAL), lambda b, s: (0, 0)),
        ],
        out_specs=pl.BlockSpec((rows, GLA_VAL), lambda b, s: (row(b, s), 0)),
        scratch_shapes=[pltpu.VMEM((GLA_HEADS, GLA_DV, GLA_DK), F32)],
        compiler_params=_params(32),
        name="gla",
    )(proj, proj, proj, proj, a_low, w_alpha2p, b_alpha2, gla_gain)


def _post_mixer_kernel(x_ref, ga_ref, gb_ref, o0_ref, o1_ref, o2_ref, l0_ref, l1_ref, l2_ref, yb_ref, ada_ref,
                       woa_ref, wob_ref, wout_ref, g2_ref, wr_ref, br_ref,
                       xo_ref, h2_ref, e_ref, w_ref):
    ls = [l0_ref[...], l1_ref[...], l2_ref[...]]
    os_ = [o0_ref, o1_ref, o2_ref]
    mx = jnp.maximum(jnp.maximum(ls[0], ls[1]), ls[2])
    es = [jnp.exp(l - mx) for l in ls]
    zs = es[0] + es[1] + es[2]
    wt = [e / zs for e in es]
    parts = []
    for h in range(A_HEADS_PER_GROUP):
        hc = slice(h * HEAD_DIM, (h + 1) * HEAD_DIM)
        acc = wt[0][:, h:h + 1] * os_[0][:, hc].astype(F32)
        acc = acc + wt[1][:, h:h + 1] * os_[1][:, hc].astype(F32)
        acc = acc + wt[2][:, h:h + 1] * os_[2][:, hc].astype(F32)
        parts.append(acc)
    ya = jnp.concatenate(parts, axis=1).astype(BF16)
    ta = jnp.dot(ya, woa_ref[...], preferred_element_type=F32)
    tb = jnp.dot(yb_ref[...], wob_ref[...], preferred_element_type=F32)
    merged = _sigmoid(ga_ref[...].astype(F32)) * ta + _sigmoid(gb_ref[...].astype(F32)) * tb
    mix = jnp.dot(merged.astype(BF16), wout_ref[...], preferred_element_type=F32)
    xn = x_ref[...] + ada_ref[0, 2:3, :] * mix
    xo_ref[...] = xn
    y = xn * lax.rsqrt(jnp.mean(xn * xn, axis=-1, keepdims=True) + NORM_EPS) * g2_ref[...]
    h2 = y * (1.0 + ada_ref[0, 4:5, :]) + ada_ref[0, 3:4, :]
    h2_ref[...] = h2

    hh = h2.astype(BF16)
    hl = (h2 - hh.astype(F32)).astype(BF16)
    wr = wr_ref[...]
    wh = wr.astype(BF16)
    wl = (wr - wh.astype(F32)).astype(BF16)
    logits = (jnp.dot(hh, wh, preferred_element_type=F32) + jnp.dot(hl, wh, preferred_element_type=F32)
              + jnp.dot(hh, wl, preferred_element_type=F32))
    tm = logits.shape[0]
    lane = lax.broadcasted_iota(jnp.int32, (tm, LANES), 1)
    lane_f = lane.astype(F32)
    valid = lane < N_EXPERTS
    neg = -jnp.inf
    aff = _sigmoid(logits)
    sel = jnp.where(valid, aff + br_ref[...], neg)

    def top1(vals):
        m = jnp.max(vals, axis=-1, keepdims=True)
        idx = jnp.min(jnp.where(vals == m, lane_f, 1e9), axis=-1, keepdims=True)
        return m, idx

    grp = lane // EXPERTS_PER_GROUP
    scores = []
    for g in range(N_GROUPS):
        vals = jnp.where(grp == g, sel, neg)
        m1, i1 = top1(vals)
        m2 = jnp.max(jnp.where(lane_f == i1, neg, vals), axis=-1, keepdims=True)
        scores.append(m1 + m2)
    best = jnp.zeros_like(scores[0])
    best_s = scores[0]
    for g in range(1, N_GROUPS):
        better = scores[g] > best_s
        best = jnp.where(better, float(g), best)
        best_s = jnp.where(better, scores[g], best_s)
    vals = jnp.where(grp.astype(F32) == best, sel, neg)
    _, i1 = top1(vals)
    _, i2 = top1(jnp.where(lane_f == i1, neg, vals))
    a1 = jnp.sum(jnp.where(lane_f == i1, aff, 0.0), axis=-1, keepdims=True)
    a2 = jnp.sum(jnp.where(lane_f == i2, aff, 0.0), axis=-1, keepdims=True)
    tot = a1 + a2
    col = lax.broadcasted_iota(jnp.int32, (tm, TOP_K), 1)
    e_ref[...] = jnp.where(col == 0, i1, i2).astype(jnp.int32)
    w_ref[...] = jnp.where(col == 0, a1 / tot, a2 / tot)


def _post_mixer(x2, proj, o_list, l_list, y_b, ada_l, woa, wob, wout, gain2, w_router_p, b_router_p):
    t, d = x2.shape
    tm = 256
    per_b = SEQ // tm
    const = lambda shape: pl.BlockSpec(shape, lambda i: (0,) * len(shape), pipeline_mode=pl.Buffered(1))
    rowblk = lambda w: pl.BlockSpec((tm, w), lambda i: (i, 0))
    return pl.pallas_call(
        _post_mixer_kernel,
        out_shape=(
            jax.ShapeDtypeStruct((t, d), F32),
            jax.ShapeDtypeStruct((t, d), F32),
            jax.ShapeDtypeStruct((t, TOP_K), jnp.int32),
            jax.ShapeDtypeStruct((t, TOP_K), F32),
        ),
        grid=(t // tm,),
        in_specs=[
            rowblk(d),
            pl.BlockSpec((tm, d), lambda i: (i, COL_GA // D_MODEL)),
            pl.BlockSpec((tm, d), lambda i: (i, COL_GB // D_MODEL)),
            rowblk(A_OUT), rowblk(A_OUT), rowblk(A_OUT),
            rowblk(LANES), rowblk(LANES), rowblk(LANES),
            rowblk(GLA_VAL),
            pl.BlockSpec((1, 6, d), lambda i: (i // per_b, 0, 0)),
            const((A_OUT, d)), const((GLA_VAL, d)), const((d, d)),
            const((1, d)), const((d, LANES)), const((1, LANES)),
        ],
        out_specs=(rowblk(d), rowblk(d), rowblk(TOP_K), rowblk(TOP_K)),
        compiler_params=_params(56),
        name="post_mixer",
    )(x2, proj, proj, *o_list, *l_list, y_b, ada_l, woa, wob, wout, gain2, w_router_p, b_router_p)


def _row_copy(src_hbm, row, dst, slot, r, sem):
    return pltpu.make_async_copy(src_hbm.at[pl.ds(row, 1), :], dst.at[slot, pl.ds(r, 1), :], sem.at[slot])


def _moe_kernel(be_ref, nu_ref, src_ref, h_hbm, wg_ref, wu_ref, wd_ref, y_ref, xg, sem):
    i = pl.program_id(0)
    n_used = nu_ref[0]
    blk = MOE_BLOCK
    slot = i % 2

    def gather(block, s, start):
        def body(r, c):
            cp = _row_copy(h_hbm, src_ref[block * blk + r], xg, s, r, sem)
            if start:
                cp.start()
            else:
                cp.wait()
            return c
        lax.fori_loop(0, blk, body, 0, unroll=8)

    @pl.when(i == 0)
    def _():
        gather(0, 0, True)

    @pl.when(i + 1 < n_used)
    def _():
        gather(i + 1, 1 - slot, True)

    @pl.when(i < n_used)
    def _():
        gather(i, slot, False)
        xb = xg[slot].astype(BF16)
        g = jnp.dot(xb, wg_ref[0], preferred_element_type=F32)
        u = jnp.dot(xb, wu_ref[0], preferred_element_type=F32)
        a = (g * _sigmoid(g) * u).astype(BF16)
        y_ref[...] = jnp.dot(a, wd_ref[0], preferred_element_type=F32)

    @pl.when(i >= n_used)
    def _():
        y_ref[...] = jnp.zeros_like(y_ref)


def _moe_experts(blk_e, n_used, src, h2, wg, wu, wd):
    t, d = h2.shape
    n_blk = blk_e.shape[0]
    f = wg.shape[2]
    grid_spec = pltpu.PrefetchScalarGridSpec(
        num_scalar_prefetch=3,
        grid=(n_blk,),
        in_specs=[
            pl.BlockSpec(memory_space=pl.ANY),
            pl.BlockSpec((1, d, f), lambda i, be, nu, sr: (be[i], 0, 0)),
            pl.BlockSpec((1, d, f), lambda i, be, nu, sr: (be[i], 0, 0)),
            pl.BlockSpec((1, f, d), lambda i, be, nu, sr: (be[i], 0, 0)),
        ],
        out_specs=pl.BlockSpec((MOE_BLOCK, d), lambda i, be, nu, sr: (i, 0)),
        scratch_shapes=[pltpu.VMEM((2, MOE_BLOCK, d), F32), pltpu.SemaphoreType.DMA((2,))],
    )
    return pl.pallas_call(
        _moe_kernel,
        out_shape=jax.ShapeDtypeStruct((n_blk * MOE_BLOCK, d), F32),
        grid_spec=grid_spec,
        compiler_params=_params(48, ("arbitrary",)),
        name="moe_experts",
    )(blk_e, n_used, src, h2, wg, wu, wd)


def _pair_copy(src_hbm, row, dst, slot, k, r, sem):
    return pltpu.make_async_copy(src_hbm.at[pl.ds(row, 1), :], dst.at[slot, k, pl.ds(r, 1), :], sem.at[slot])


def _combine_kernel(pos_ref, y_hbm, x_ref, w_ref, ada_ref, fg_ref, o_ref, yg, sem, *, tm, final):
    i = pl.program_id(0)
    n = pl.num_programs(0)
    slot = i % 2

    def gather(tile, s, start):
        def body(r, c):
            for k in range(TOP_K):
                cp = _pair_copy(y_hbm, pos_ref[(tile * tm + r) * TOP_K + k], yg, s, k, r, sem)
                if start:
                    cp.start()
                else:
                    cp.wait()
            return c
        lax.fori_loop(0, tm, body, 0, unroll=8)

    @pl.when(i == 0)
    def _():
        gather(0, 0, True)

    @pl.when(i + 1 < n)
    def _():
        gather(i + 1, 1 - slot, True)

    gather(i, slot, False)
    w = w_ref[...]
    y = w[:, 0:1] * yg[slot, 0] + w[:, 1:2] * yg[slot, 1]
    out = x_ref[...] + ada_ref[0, 5:6, :] * y
    if final:
        out = out * lax.rsqrt(jnp.mean(out * out, axis=-1, keepdims=True) + NORM_EPS) * fg_ref[...]
    o_ref[...] = out


def _combine(pos, ybuf, x2, gw, ada_l, final_gain, final):
    t, d = x2.shape
    tm = 256
    per_b = SEQ // tm
    kern = functools.partial(_combine_kernel, tm=tm, final=final)
    grid_spec = pltpu.PrefetchScalarGridSpec(
        num_scalar_prefetch=1,
        grid=(t // tm,),
        in_specs=[
            pl.BlockSpec(memory_space=pl.ANY),
            pl.BlockSpec((tm, d), lambda i, p: (i, 0)),
            pl.BlockSpec((tm, TOP_K), lambda i, p: (i, 0)),
            pl.BlockSpec((1, 6, d), lambda i, p: (i // per_b, 0, 0)),
            pl.BlockSpec((1, d), lambda i, p: (0, 0)),
        ],
        out_specs=pl.BlockSpec((tm, d), lambda i, p: (i, 0)),
        scratch_shapes=[pltpu.VMEM((2, TOP_K, tm, d), F32), pltpu.SemaphoreType.DMA((2,))],
    )
    return pl.pallas_call(
        kern,
        out_shape=jax.ShapeDtypeStruct((t, d), F32),
        grid_spec=grid_spec,
        compiler_params=_params(40, ("arbitrary",)),
        name="moe_combine_final" if final else "moe_combine",
    )(pos, ybuf, x2, gw, ada_l, final_gain)


def _dispatch_plan(eidx, n_blk):
    t = eidx.shape[0]
    flat_e = eidx.reshape(-1)
    onehot = (flat_e[:, None] == jnp.arange(N_EXPERTS, dtype=jnp.int32)[None, :]).astype(jnp.int32)
    csum = jnp.cumsum(onehot, axis=0)
    rank = jnp.sum((csum - onehot) * onehot, axis=1)
    counts = csum[-1]
    pcounts = (counts + MOE_BLOCK - 1) // MOE_BLOCK * MOE_BLOCK
    pend = jnp.cumsum(pcounts)
    pstart = pend - pcounts
    dest = (pstart[flat_e] + rank).astype(jnp.int32)
    tok = jnp.arange(t * TOP_K, dtype=jnp.int32) // TOP_K
    src = jnp.zeros((n_blk * MOE_BLOCK,), jnp.int32).at[dest].set(tok)
    blk_start = jnp.arange(n_blk, dtype=jnp.int32) * MOE_BLOCK
    blk_e = jnp.minimum(jnp.searchsorted(pend, blk_start, side="right"), N_EXPERTS - 1).astype(jnp.int32)
    n_used = (pend[-1] // MOE_BLOCK).astype(jnp.int32).reshape(1)
    return blk_e, n_used, src, dest


def _rope_tables():
    pos = jnp.arange(SEQ, dtype=F32)
    inv_freq = ROPE_THETA ** (-jnp.arange(0, HEAD_DIM, 2, dtype=F32) / HEAD_DIM)
    ang = pos[:, None] * inv_freq[None, :]
    cos, sin = jnp.cos(ang), jnp.sin(ang)
    return jnp.concatenate([cos, cos], axis=1), jnp.concatenate([-sin, sin], axis=1)


def _reorder_w_in(w):
    o = 0
    q_a, o = w[:, o:o + A_WIDTH], o + A_WIDTH
    k_a, o = w[:, o:o + A_WIDTH], o + A_WIDTH
    v_a, o = w[:, o:o + A_WIDTH], o + A_WIDTH
    q_b, o = w[:, o:o + GLA_KEY], o + GLA_KEY
    k_b, o = w[:, o:o + GLA_KEY], o + GLA_KEY
    v_b, o = w[:, o:o + GLA_VAL], o + GLA_VAL
    r_b, o = w[:, o:o + GLA_VAL], o + GLA_VAL
    a_l, o = w[:, o:o + GLA_LOWRANK], o + GLA_LOWRANK
    g_a, o = w[:, o:o + D_MODEL], o + D_MODEL
    g_b, o = w[:, o:o + D_MODEL], o + D_MODEL
    cols = [g_a, g_b, v_b, r_b]
    for g in range(len(DILATED_GROUPS)):
        gs = slice(g * A_OUT, (g + 1) * A_OUT)
        cols += [q_a[:, gs], k_a[:, gs], v_a[:, gs]]
    cols += [q_b, k_b]
    w_main = jnp.concatenate(cols, axis=1).astype(BF16)
    w_alow = jnp.pad(a_l, ((0, 0), (0, LANES - GLA_LOWRANK))).astype(BF16)
    return w_main, w_alow


def kernel(x, c, w_ada, b_ada, norm_mix, norm_ffn, w_in, w_alpha2, b_alpha2, gla_gain, w_out_a, w_out_b, w_out,
           w_router, b_router, w_gate_e, w_up_e, w_down_e, final_norm):
    b, s, d = x.shape
    depth = w_ada.shape[0]
    t = b * s
    n_blk = (t * TOP_K) // MOE_BLOCK + N_EXPERTS
    cosf, sinf = _rope_tables()
    ada = _ada(c, w_ada, b_ada).reshape(depth, b, 6, d)
    w_router_p = jnp.pad(w_router, ((0, 0), (0, LANES - N_EXPERTS)))
    b_router_p = jnp.pad(b_router, (0, LANES - N_EXPERTS)).reshape(1, LANES)
    final_gain = final_norm.reshape(1, d)
    x2 = x.reshape(t, d)
    for l in range(depth):
        w_main, w_alow = _reorder_w_in(w_in[l])
        w_alpha2p = jnp.pad(w_alpha2[l], ((0, LANES - GLA_LOWRANK), (0, 0))).astype(BF16)
        h, a_low = _norm_mod(x2, ada[l], norm_mix[l].reshape(1, d), w_alow)
        proj = _in_proj(h, w_main)
        o_list, l_list = [], []
        for g in range(len(DILATED_GROUPS)):
            o_g, l_g = _attention(proj, cosf, sinf, g)
            o_list.append(o_g)
            l_list.append(l_g)
        y_b = _gla(proj, a_low, w_alpha2p, b_alpha2[l].reshape(1, GLA_KEY), gla_gain[l].reshape(1, GLA_VAL))
        x2, h2, eidx, gw = _post_mixer(
            x2, proj, o_list, l_list, y_b, ada[l],
            w_out_a[l].astype(BF16), w_out_b[l].astype(BF16), w_out[l].astype(BF16),
            norm_ffn[l].reshape(1, d), w_router_p, b_router_p)
        blk_e, n_used, src, dest = _dispatch_plan(eidx, n_blk)
        ybuf = _moe_experts(blk_e, n_used, src, h2,
                            w_gate_e[l].astype(BF16), w_up_e[l].astype(BF16), w_down_e[l].astype(BF16))
        x2 = _combine(dest, ybuf, x2, gw, ada[l], final_gain, final=(l == depth - 1))
    return x2.reshape(b, s, d)
```

```python
import functools

import jax
import jax.numpy as jnp
from jax import lax
from jax.experimental import pallas as pl
from jax.experimental.pallas import tpu as pltpu

F32 = jnp.float32
BF16 = jnp.bfloat16

D_MODEL = 2048
SEQ = 2048
HEAD_DIM = 128
A_HEADS_PER_GROUP = 4
DILATED_GROUPS = ((128, 1), (512, 4), (2048, 16))
A_WIDTH = 12 * HEAD_DIM
A_OUT = A_HEADS_PER_GROUP * HEAD_DIM
ATT_BLOCK = 128
ROPE_THETA = 10000.0
GLA_HEADS = 4
GLA_DK = 128
GLA_DV = 256
GLA_KEY = GLA_HEADS * GLA_DK
GLA_VAL = GLA_HEADS * GLA_DV
GLA_LOWRANK = 16
GLA_TAU = 16.0
GLA_CHUNK = 64
N_EXPERTS = 16
N_GROUPS = 4
EXPERTS_PER_GROUP = N_EXPERTS // N_GROUPS
TOP_K = 2
D_FF_EXPERT = 1024
NORM_EPS = 1e-6

LANES = 128
MOE_BLOCK = 256
GROUP_COLS = 3 * A_OUT

COL_GA = 0
COL_GB = D_MODEL
COL_VB = 2 * D_MODEL
COL_RB = COL_VB + GLA_VAL
COL_A = COL_RB + GLA_VAL
COL_QB = COL_A + 3 * GROUP_COLS
COL_KB = COL_QB + GLA_KEY
PROJ_WIDTH = COL_KB + GLA_KEY


def _params(vmem_mib, sem=None):
    kw = dict(vmem_limit_bytes=int(vmem_mib) << 20)
    if sem is not None:
        kw["dimension_semantics"] = sem
    return pltpu.CompilerParams(**kw)


def _sigmoid(x):
    return 1.0 / (1.0 + jnp.exp(-x))


def _ada_kernel(c_ref, w_ref, b_ref, o_ref):
    c = c_ref[...]
    ca = (c * _sigmoid(c)).astype(BF16)
    o_ref[0] = jnp.dot(ca, w_ref[0].astype(BF16), preferred_element_type=F32) + b_ref[0]


def _ada(c, w_ada, b_ada):
    depth, d, n = w_ada.shape
    b = c.shape[0]
    tn = 1024
    return pl.pallas_call(
        _ada_kernel,
        out_shape=jax.ShapeDtypeStruct((depth, b, n), F32),
        grid=(depth, n // tn),
        in_specs=[
            pl.BlockSpec((b, d), lambda l, j: (0, 0)),
            pl.BlockSpec((1, d, tn), lambda l, j: (l, 0, j)),
            pl.BlockSpec((1, 1, tn), lambda l, j: (l, 0, j)),
        ],
        out_specs=pl.BlockSpec((1, b, tn), lambda l, j: (l, 0, j)),
        compiler_params=_params(40),
        name="ada",
    )(c, w_ada, b_ada.reshape(depth, 1, n))


def _norm_mod_kernel(x_ref, ada_ref, gain_ref, wal_ref, h_ref, al_ref):
    x = x_ref[...]
    y = x * lax.rsqrt(jnp.mean(x * x, axis=-1, keepdims=True) + NORM_EPS) * gain_ref[...]
    h = (y * (1.0 + ada_ref[0, 1:2, :]) + ada_ref[0, 0:1, :]).astype(BF16)
    h_ref[...] = h
    al_ref[...] = jnp.dot(h, wal_ref[...], preferred_element_type=F32)


def _norm_mod(x2, ada_l, gain, w_alow):
    t, d = x2.shape
    tm = 512
    per_b = SEQ // tm
    return pl.pallas_call(
        _norm_mod_kernel,
        out_shape=(jax.ShapeDtypeStruct((t, d), BF16), jax.ShapeDtypeStruct((t, LANES), F32)),
        grid=(t // tm,),
        in_specs=[
            pl.BlockSpec((tm, d), lambda i: (i, 0)),
            pl.BlockSpec((1, 6, d), lambda i: (i // per_b, 0, 0)),
            pl.BlockSpec((1, d), lambda i: (0, 0)),
            pl.BlockSpec((d, LANES), lambda i: (0, 0)),
        ],
        out_specs=(pl.BlockSpec((tm, d), lambda i: (i, 0)), pl.BlockSpec((tm, LANES), lambda i: (i, 0))),
        compiler_params=_params(40),
        name="norm_mod",
    )(x2, ada_l, gain, w_alow)


def _matmul_kernel(a_ref, b_ref, o_ref):
    o_ref[...] = jnp.dot(a_ref[...], b_ref[...], preferred_element_type=F32).astype(o_ref.dtype)


def _in_proj(h, w):
    t, d = h.shape
    n = w.shape[1]
    tm, tn = 2048, 512
    return pl.pallas_call(
        _matmul_kernel,
        out_shape=jax.ShapeDtypeStruct((t, n), BF16),
        grid=(t // tm, n // tn),
        in_specs=[pl.BlockSpec((tm, d), lambda i, j: (i, 0)), pl.BlockSpec((d, tn), lambda i, j: (0, j))],
        out_specs=pl.BlockSpec((tm, tn), lambda i, j: (i, j)),
        compiler_params=_params(48),
        name="in_proj",
    )(h, w)


def _attn_kernel(qkv_ref, cos_ref, sin_ref, o_ref, lse_ref, q_s, k_s, v_s, o_s, l_s, *, dil, nb, wd):
    scale = HEAD_DIM ** -0.5
    blk = ATT_BLOCK
    cosf = cos_ref[...]
    sinf = sin_ref[...]
    for h in range(A_HEADS_PER_GROUP):
        hc = slice(h * HEAD_DIM, (h + 1) * HEAD_DIM)
        tq = qkv_ref[:, hc].astype(F32)
        q_s[h] = (tq * cosf + pltpu.roll(tq, HEAD_DIM // 2, 1) * sinf) * scale
        tk = qkv_ref[:, A_OUT + h * HEAD_DIM:A_OUT + (h + 1) * HEAD_DIM].astype(F32)
        k_s[h] = tk * cosf + pltpu.roll(tk, HEAD_DIM // 2, 1) * sinf
        v_s[h] = qkv_ref[:, 2 * A_OUT + h * HEAD_DIM:2 * A_OUT + (h + 1) * HEAD_DIM].astype(F32)

    def rows(start, size):
        return pl.ds(start, size) if dil == 1 else pl.ds(start, size, stride=dil)

    i1 = lax.broadcasted_iota(jnp.int32, (blk, blk), 0)
    j1 = lax.broadcasted_iota(jnp.int32, (blk, blk), 1)
    mask_first = ((i1 - j1) >= 0) & ((i1 - j1) <= wd)
    i2 = lax.broadcasted_iota(jnp.int32, (blk, 2 * blk), 0)
    j2 = lax.broadcasted_iota(jnp.int32, (blk, 2 * blk), 1)
    dist = blk + i2 - j2
    mask_later = (dist >= 0) & (dist <= wd)
    lane = lax.broadcasted_iota(jnp.int32, (blk, LANES), 1)

    def attend(qs, ks, nk, mask):
        lse_tile = jnp.zeros((blk, LANES), F32)
        for h in range(A_HEADS_PER_GROUP):
            q = q_s[h, rows(qs, blk), :].astype(BF16)
            k = k_s[h, rows(ks, nk), :].astype(BF16)
            v = v_s[h, rows(ks, nk), :].astype(BF16)
            s = lax.dot_general(q, k, (((1,), (1,)), ((), ())), preferred_element_type=F32)
            s = jnp.where(mask, s, -jnp.inf)
            m = jnp.max(s, axis=-1, keepdims=True)
            p = jnp.exp(s - m)
            den = jnp.sum(p, axis=-1, keepdims=True)
            o = jnp.dot(p.astype(BF16), v, preferred_element_type=F32) / den
            o_s[h, rows(qs, blk), :] = o
            lse_tile = jnp.where(lane == h, m + jnp.log(den), lse_tile)
        l_s[rows(qs, blk), :] = lse_tile

    def per_phase(ph, carry):
        attend(ph, ph, blk, mask_first)

        def per_blk(n, c):
            qs = ph + dil * blk * n
            ks = ph + dil * blk * (n - 1)
            if dil == 1:
                qs = pl.multiple_of(qs, blk)
                ks = pl.multiple_of(ks, blk)
            attend(qs, ks, 2 * blk, mask_later)
            return c

        if nb > 1:
            lax.fori_loop(1, nb, per_blk, 0, unroll=3)
        return carry

    lax.fori_loop(0, dil, per_phase, 0, unroll=4 if nb == 1 else 1)
    for h in range(A_HEADS_PER_GROUP):
        o_ref[:, h * HEAD_DIM:(h + 1) * HEAD_DIM] = o_s[h].astype(o_ref.dtype)
    lse_ref[...] = l_s[...]


def _attention(proj, cosf, sinf, group):
    window, dil = DILATED_GROUPS[group]
    t = proj.shape[0]
    nb = (SEQ // dil) // ATT_BLOCK
    col = COL_A // GROUP_COLS + group
    kern = functools.partial(_attn_kernel, dil=dil, nb=nb, wd=window // dil)
    return pl.pallas_call(
        kern,
        out_shape=(jax.ShapeDtypeStruct((t, A_OUT), BF16), jax.ShapeDtypeStruct((t, LANES), F32)),
        grid=(t // SEQ,),
        in_specs=[
            pl.BlockSpec((SEQ, GROUP_COLS), lambda b: (b, col)),
            pl.BlockSpec((SEQ, HEAD_DIM), lambda b: (0, 0)),
            pl.BlockSpec((SEQ, HEAD_DIM), lambda b: (0, 0)),
        ],
        out_specs=(pl.BlockSpec((SEQ, A_OUT), lambda b: (b, 0)), pl.BlockSpec((SEQ, LANES), lambda b: (b, 0))),
        scratch_shapes=[pltpu.VMEM((A_HEADS_PER_GROUP, SEQ, HEAD_DIM), F32) for _ in range(4)]
        + [pltpu.VMEM((SEQ, LANES), F32)],
        compiler_params=_params(48),
        name=f"dilated_attn_g{group}",
    )(proj, cosf, sinf)


def _gla_kernel(q_ref, k_ref, v_ref, r_ref, al_ref, wa_ref, ba_ref, gain_ref, y_ref, st_ref, *, rows):
    @pl.when(pl.program_id(1) == 0)
    def _():
        st_ref[...] = jnp.zeros_like(st_ref)

    ck = GLA_CHUNK
    z = jnp.dot(al_ref[...].astype(BF16), wa_ref[...], preferred_element_type=F32) + ba_ref[...]
    log_a = (jnp.minimum(z, 0.0) - jnp.log(1.0 + jnp.exp(-jnp.abs(z)))) * (1.0 / GLA_TAU)
    ri = lax.broadcasted_iota(jnp.int32, (ck, ck), 0)
    ci = lax.broadcasted_iota(jnp.int32, (ck, ck), 1)
    causal = ri >= ci
    tri = causal.astype(BF16)
    nt = (((1,), (1,)), ((), ()))
    for c in range(rows // ck):
        rs = slice(c * ck, (c + 1) * ck)
        for h in range(GLA_HEADS):
            hk = slice(h * GLA_DK, (h + 1) * GLA_DK)
            hv = slice(h * GLA_DV, (h + 1) * GLA_DV)
            la = log_a[rs, hk]
            la_hi = la.astype(BF16)
            la_lo = (la - la_hi.astype(F32)).astype(BF16)
            b = jnp.dot(tri, la_hi, preferred_element_type=F32) + jnp.dot(tri, la_lo, preferred_element_type=F32)
            q = q_ref[rs, hk].astype(F32) * (GLA_DK ** -0.5)
            k = k_ref[rs, hk].astype(F32)
            v = v_ref[rs, hv]
            q_dec = (q * jnp.exp(b)).astype(BF16)
            k_inv = (k * jnp.exp(-b)).astype(BF16)
            attn = lax.dot_general(q_dec, k_inv, nt, preferred_element_type=F32)
            attn = jnp.where(causal, attn, 0.0).astype(BF16)
            o = jnp.dot(attn, v, preferred_element_type=F32)
            b_end = b[ck - 1:ck, :]
            k_end = (k * jnp.exp(b_end - b)).astype(BF16)
            st = st_ref[h]
            o = o + lax.dot_general(q_dec, st.astype(BF16), nt, preferred_element_type=F32)
            v_t = v.astype(F32).T.astype(BF16)
            st_ref[h] = st * jnp.exp(b_end) + jnp.dot(v_t, k_end, preferred_element_type=F32)
            o = o * lax.rsqrt(jnp.mean(o * o, axis=-1, keepdims=True) + NORM_EPS) * gain_ref[:, hv]
            r = r_ref[rs, hv].astype(F32)
            y_ref[rs, hv] = (o * (r * _sigmoid(r))).astype(y_ref.dtype)


def _gla(proj, a_low, w_alpha2p, b_alpha2, gla_gain):
    t = proj.shape[0]
    rows = 256
    per_b = SEQ // rows
    kern = functools.partial(_gla_kernel, rows=rows)
    row = lambda b, s: b * per_b + s
    return pl.pallas_call(
        kern,
        out_shape=jax.ShapeDtypeStruct((t, GLA_VAL), BF16),
        grid=(t // SEQ, per_b),
        in_specs=[
            pl.BlockSpec((rows, GLA_KEY), lambda b, s: (row(b, s), COL_QB // GLA_KEY)),
            pl.BlockSpec((rows, GLA_KEY), lambda b, s: (row(b, s), COL_KB // GLA_KEY)),
            pl.BlockSpec((rows, GLA_VAL), lambda b, s: (row(b, s), COL_VB // GLA_VAL)),
            pl.BlockSpec((rows, GLA_VAL), lambda b, s: (row(b, s), COL_RB // GLA_VAL)),
            pl.BlockSpec((rows, LANES), lambda b, s: (row(b, s), 0)),
            pl.BlockSpec((LANES, GLA_KEY), lambda b, s: (0, 0)),
            pl.BlockSpec((1, GLA_KEY), lambda b, s: (0, 0)),
            pl.BlockSpec((1, GLA_VAL), lambda b, s: (0, 0)),
        ],
        out_specs=pl.BlockSpec((rows, GLA_VAL), lambda b, s: (row(b, s), 0)),
        scratch_shapes=[pltpu.VMEM((GLA_HEADS, GLA_DV, GLA_DK), F32)],
        compiler_params=_params(32),
        name="gla",
    )(proj, proj, proj, proj, a_low, w_alpha2p, b_alpha2, gla_gain)


def _post_mixer_kernel(x_ref, ga_ref, gb_ref, o0_ref, o1_ref, o2_ref, l0_ref, l1_ref, l2_ref, yb_ref, ada_ref,
                       woa_ref, wob_ref, wout_ref, g2_ref, wr_ref, br_ref,
                       xo_ref, h2_ref, e_ref, w_ref):
    ls = [l0_ref[...], l1_ref[...], l2_ref[...]]
    os_ = [o0_ref, o1_ref, o2_ref]
    mx = jnp.maximum(jnp.maximum(ls[0], ls[1]), ls[2])
    es = [jnp.exp(l - mx) for l in ls]
    zs = es[0] + es[1] + es[2]
    wt = [e / zs for e in es]
    parts = []
    for h in range(A_HEADS_PER_GROUP):
        hc = slice(h * HEAD_DIM, (h + 1) * HEAD_DIM)
        acc = wt[0][:, h:h + 1] * os_[0][:, hc].astype(F32)
        acc = acc + wt[1][:, h:h + 1] * os_[1][:, hc].astype(F32)
        acc = acc + wt[2][:, h:h + 1] * os_[2][:, hc].astype(F32)
        parts.append(acc)
    ya = jnp.concatenate(parts, axis=1).astype(BF16)
    ta = jnp.dot(ya, woa_ref[...], preferred_element_type=F32)
    tb = jnp.dot(yb_ref[...], wob_ref[...], preferred_element_type=F32)
    merged = _sigmoid(ga_ref[...]) * ta.astype(BF16) + _sigmoid(gb_ref[...]) * tb.astype(BF16)
    mix = jnp.dot(merged, wout_ref[...], preferred_element_type=F32)
    xn = x_ref[...] + ada_ref[0, 2:3, :] * mix
    xo_ref[...] = xn
    y = xn * lax.rsqrt(jnp.mean(xn * xn, axis=-1, keepdims=True) + NORM_EPS) * g2_ref[...]
    h2 = y * (1.0 + ada_ref[0, 4:5, :]) + ada_ref[0, 3:4, :]
    h2_ref[...] = h2

    hh = h2.astype(BF16)
    hl = (h2 - hh.astype(F32)).astype(BF16)
    wr = wr_ref[...]
    wh = wr.astype(BF16)
    wl = (wr - wh.astype(F32)).astype(BF16)
    logits = (jnp.dot(hh, wh, preferred_element_type=F32) + jnp.dot(hl, wh, preferred_element_type=F32)
              + jnp.dot(hh, wl, preferred_element_type=F32))
    tm = logits.shape[0]
    lane = lax.broadcasted_iota(jnp.int32, (tm, LANES), 1)
    lane_f = lane.astype(F32)
    valid = lane < N_EXPERTS
    neg = -jnp.inf
    aff = _sigmoid(logits)
    sel = jnp.where(valid, aff + br_ref[...], neg)

    def top1(vals):
        m = jnp.max(vals, axis=-1, keepdims=True)
        idx = jnp.min(jnp.where(vals == m, lane_f, 1e9), axis=-1, keepdims=True)
        return m, idx

    grp = lane // EXPERTS_PER_GROUP
    scores = []
    for g in range(N_GROUPS):
        vals = jnp.where(grp == g, sel, neg)
        m1, i1 = top1(vals)
        m2 = jnp.max(jnp.where(lane_f == i1, neg, vals), axis=-1, keepdims=True)
        scores.append(m1 + m2)
    best = jnp.zeros_like(scores[0])
    best_s = scores[0]
    for g in range(1, N_GROUPS):
        better = scores[g] > best_s
        best = jnp.where(better, float(g), best)
        best_s = jnp.where(better, scores[g], best_s)
    vals = jnp.where(grp.astype(F32) == best, sel, neg)
    _, i1 = top1(vals)
    _, i2 = top1(jnp.where(lane_f == i1, neg, vals))
    a1 = jnp.sum(jnp.where(lane_f == i1, aff, 0.0), axis=-1, keepdims=True)
    a2 = jnp.sum(jnp.where(lane_f == i2, aff, 0.0), axis=-1, keepdims=True)
    tot = a1 + a2
    col = lax.broadcasted_iota(jnp.int32, (tm, TOP_K), 1)
    e_ref[...] = jnp.where(col == 0, i1, i2).astype(jnp.int32)
    w_ref[...] = jnp.where(col == 0, a1 / tot, a2 / tot)


def _post_mixer(x2, proj, o_list, l_list, y_b, ada_l, woa, wob, wout, gain2, w_router_p, b_router_p):
    t, d = x2.shape
    tm = 256
    per_b = SEQ // tm
    const = lambda shape: pl.BlockSpec(shape, lambda i: (0,) * len(shape), pipeline_mode=pl.Buffered(1))
    rowblk = lambda w: pl.BlockSpec((tm, w), lambda i: (i, 0))
    return pl.pallas_call(
        _post_mixer_kernel,
        out_shape=(
            jax.ShapeDtypeStruct((t, d), F32),
            jax.ShapeDtypeStruct((t, d), F32),
            jax.ShapeDtypeStruct((t, TOP_K), jnp.int32),
            jax.ShapeDtypeStruct((t, TOP_K), F32),
        ),
        grid=(t // tm,),
        in_specs=[
            rowblk(d),
            pl.BlockSpec((tm, d), lambda i: (i, COL_GA // D_MODEL)),
            pl.BlockSpec((tm, d), lambda i: (i, COL_GB // D_MODEL)),
            rowblk(A_OUT), rowblk(A_OUT), rowblk(A_OUT),
            rowblk(LANES), rowblk(LANES), rowblk(LANES),
            rowblk(GLA_VAL),
            pl.BlockSpec((1, 6, d), lambda i: (i // per_b, 0, 0)),
            const((A_OUT, d)), const((GLA_VAL, d)), const((d, d)),
            const((1, d)), const((d, LANES)), const((1, LANES)),
        ],
        out_specs=(rowblk(d), rowblk(d), rowblk(TOP_K), rowblk(TOP_K)),
        compiler_params=_params(56),
        name="post_mixer",
    )(x2, proj, proj, *o_list, *l_list, y_b, ada_l, woa, wob, wout, gain2, w_router_p, b_router_p)


def _row_copy(src_hbm, row, dst, slot, r, sem):
    return pltpu.make_async_copy(src_hbm.at[pl.ds(row, 1), :], dst.at[slot, pl.ds(r, 1), :], sem.at[slot])


def _moe_kernel(be_ref, nu_ref, dest_ref, h_hbm, wg_ref, wu_ref, wd_ref, y_ref, xg, src_s, sem):
    i = pl.program_id(0)
    n_used = nu_ref[0]
    blk = MOE_BLOCK
    slot = i % 2

    @pl.when(i == 0)
    def _():
        def init(r, c):
            src_s[r] = 0
            return c
        lax.fori_loop(0, src_s.shape[0], init, 0, unroll=16)

        def scat(j, c):
            src_s[dest_ref[j]] = lax.shift_right_logical(j, 1)
            return c
        lax.fori_loop(0, dest_ref.shape[0], scat, 0, unroll=8)

    def gather(block, s, start):
        def body(r, c):
            cp = _row_copy(h_hbm, src_s[block * blk + r], xg, s, r, sem)
            if start:
                cp.start()
            else:
                cp.wait()
            return c
        lax.fori_loop(0, blk, body, 0, unroll=8)

    @pl.when(i == 0)
    def _():
        gather(0, 0, True)

    @pl.when(i + 1 < n_used)
    def _():
        gather(i + 1, 1 - slot, True)

    @pl.when(i < n_used)
    def _():
        gather(i, slot, False)
        xb = xg[slot].astype(BF16)
        g = jnp.dot(xb, wg_ref[0], preferred_element_type=F32)
        u = jnp.dot(xb, wu_ref[0], preferred_element_type=F32)
        a = (g * _sigmoid(g) * u).astype(BF16)
        y_ref[...] = jnp.dot(a, wd_ref[0], preferred_element_type=F32)

    @pl.when(i >= n_used)
    def _():
        y_ref[...] = jnp.zeros_like(y_ref)


def _moe_experts(blk_e, n_used, dest, h2, wg, wu, wd):
    t, d = h2.shape
    n_blk = blk_e.shape[0]
    f = wg.shape[2]
    grid_spec = pltpu.PrefetchScalarGridSpec(
        num_scalar_prefetch=3,
        grid=(n_blk,),
        in_specs=[
            pl.BlockSpec(memory_space=pl.ANY),
            pl.BlockSpec((1, d, f), lambda i, be, nu, sr: (be[i], 0, 0)),
            pl.BlockSpec((1, d, f), lambda i, be, nu, sr: (be[i], 0, 0)),
            pl.BlockSpec((1, f, d), lambda i, be, nu, sr: (be[i], 0, 0)),
        ],
        out_specs=pl.BlockSpec((MOE_BLOCK, d), lambda i, be, nu, sr: (i, 0)),
        scratch_shapes=[pltpu.VMEM((2, MOE_BLOCK, d), F32), pltpu.SMEM((n_blk * MOE_BLOCK,), jnp.int32),
                        pltpu.SemaphoreType.DMA((2,))],
    )
    return pl.pallas_call(
        _moe_kernel,
        out_shape=jax.ShapeDtypeStruct((n_blk * MOE_BLOCK, d), F32),
        grid_spec=grid_spec,
        compiler_params=_params(48, ("arbitrary",)),
        name="moe_experts",
    )(blk_e, n_used, dest, h2, wg, wu, wd)


def _pair_copy(src_hbm, row, dst, slot, k, r, sem):
    return pltpu.make_async_copy(src_hbm.at[pl.ds(row, 1), :], dst.at[slot, k, pl.ds(r, 1), :], sem.at[slot])


def _combine_kernel(pos_ref, y_hbm, x_ref, w_ref, ada_ref, fg_ref, o_ref, yg, sem, *, tm, final):
    i = pl.program_id(0)
    n = pl.num_programs(0)
    slot = i % 2

    def gather(tile, s, start):
        def body(r, c):
            for k in range(TOP_K):
                cp = _pair_copy(y_hbm, pos_ref[(tile * tm + r) * TOP_K + k], yg, s, k, r, sem)
                if start:
                    cp.start()
                else:
                    cp.wait()
            return c
        lax.fori_loop(0, tm, body, 0, unroll=8)

    @pl.when(i == 0)
    def _():
        gather(0, 0, True)

    @pl.when(i + 1 < n)
    def _():
        gather(i + 1, 1 - slot, True)

    gather(i, slot, False)
    w = w_ref[...]
    y = w[:, 0:1] * yg[slot, 0] + w[:, 1:2] * yg[slot, 1]
    out = x_ref[...] + ada_ref[0, 5:6, :] * y
    if final:
        out = out * lax.rsqrt(jnp.mean(out * out, axis=-1, keepdims=True) + NORM_EPS) * fg_ref[...]
    o_ref[...] = out


def _combine(pos, ybuf, x2, gw, ada_l, final_gain, final):
    t, d = x2.shape
    tm = 256
    per_b = SEQ // tm
    kern = functools.partial(_combine_kernel, tm=tm, final=final)
    grid_spec = pltpu.PrefetchScalarGridSpec(
        num_scalar_prefetch=1,
        grid=(t // tm,),
        in_specs=[
            pl.BlockSpec(memory_space=pl.ANY),
            pl.BlockSpec((tm, d), lambda i, p: (i, 0)),
            pl.BlockSpec((tm, TOP_K), lambda i, p: (i, 0)),
            pl.BlockSpec((1, 6, d), lambda i, p: (i // per_b, 0, 0)),
            pl.BlockSpec((1, d), lambda i, p: (0, 0)),
        ],
        out_specs=pl.BlockSpec((tm, d), lambda i, p: (i, 0)),
        scratch_shapes=[pltpu.VMEM((2, TOP_K, tm, d), F32), pltpu.SemaphoreType.DMA((2,))],
    )
    return pl.pallas_call(
        kern,
        out_shape=jax.ShapeDtypeStruct((t, d), F32),
        grid_spec=grid_spec,
        compiler_params=_params(40, ("arbitrary",)),
        name="moe_combine_final" if final else "moe_combine",
    )(pos, ybuf, x2, gw, ada_l, final_gain)


def _plan_kernel(e_ref, dest_ref, cnt_ref, carry, pstart_s, *, tm):
    ph = pl.program_id(0)
    i = pl.program_id(1)
    lane = lax.broadcasted_iota(jnp.int32, (tm, LANES), 1)
    e = e_ref[...]
    oh0 = e[:, 0:1] == lane
    oh1 = e[:, 1:2] == lane
    oh = oh0.astype(F32) + oh1.astype(F32)
    tile_counts = jnp.sum(oh, axis=0, keepdims=True)

    @pl.when((ph == 0) & (i == 0))
    def _():
        carry[...] = jnp.zeros_like(carry)

    @pl.when(ph == 0)
    def _():
        carry[...] += tile_counts

    @pl.when((ph == 1) & (i == 0))
    def _():
        counts = carry[...]
        cnt_ref[...] = counts.astype(jnp.int32)
        nblk = jnp.floor((counts + (MOE_BLOCK - 1)) * (1.0 / MOE_BLOCK))
        r = lax.broadcasted_iota(jnp.int32, (LANES, LANES), 0)
        c = lax.broadcasted_iota(jnp.int32, (LANES, LANES), 1)
        upper = (r < c).astype(BF16)
        nb8 = jnp.broadcast_to(nblk, (8, LANES)).astype(BF16)
        pstart_s[...] = jnp.dot(nb8, upper, preferred_element_type=F32)[0:1, :] * float(MOE_BLOCK)
        carry[...] = jnp.zeros_like(carry)

    @pl.when(ph == 1)
    def _():
        r = lax.broadcasted_iota(jnp.int32, (tm, tm), 0)
        c = lax.broadcasted_iota(jnp.int32, (tm, tm), 1)
        lower = (r > c).astype(BF16)
        before = jnp.dot(lower, oh.astype(BF16), preferred_element_type=F32) + carry[...]
        base = before + pstart_s[...]
        d0 = jnp.sum(jnp.where(oh0, base, 0.0), axis=1, keepdims=True)
        d1 = jnp.sum(jnp.where(oh1, base, 0.0), axis=1, keepdims=True)
        col = lax.broadcasted_iota(jnp.int32, (tm, TOP_K), 1)
        dest_ref[...] = jnp.where(col == 0, d0, d1).astype(jnp.int32)
        carry[...] += tile_counts


def _dispatch_plan(eidx, n_blk):
    t = eidx.shape[0]
    tm = 512
    dest, counts = pl.pallas_call(
        functools.partial(_plan_kernel, tm=tm),
        out_shape=(jax.ShapeDtypeStruct((t, TOP_K), jnp.int32), jax.ShapeDtypeStruct((1, LANES), jnp.int32)),
        grid=(2, t // tm),
        in_specs=[pl.BlockSpec((tm, TOP_K), lambda p, i: (i, 0))],
        out_specs=(pl.BlockSpec((tm, TOP_K), lambda p, i: (i * p, 0)), pl.BlockSpec((1, LANES), lambda p, i: (0, 0))),
        scratch_shapes=[pltpu.VMEM((1, LANES), F32), pltpu.VMEM((1, LANES), F32)],
        compiler_params=_params(32, ("arbitrary", "arbitrary")),
        name="dispatch_plan",
    )(eidx)
    counts = counts[0, :N_EXPERTS]
    pcounts = (counts + MOE_BLOCK - 1) // MOE_BLOCK * MOE_BLOCK
    pend = jnp.cumsum(pcounts)
    blk_start = jnp.arange(n_blk, dtype=jnp.int32) * MOE_BLOCK
    blk_e = jnp.sum((pend[None, :] <= blk_start[:, None]).astype(jnp.int32), axis=1)
    blk_e = jnp.minimum(blk_e, N_EXPERTS - 1).astype(jnp.int32)
    n_used = (pend[-1] // MOE_BLOCK).astype(jnp.int32).reshape(1)
    return blk_e, n_used, dest.reshape(-1)


def _rope_tables():
    pos = jnp.arange(SEQ, dtype=F32)
    inv_freq = ROPE_THETA ** (-jnp.arange(0, HEAD_DIM, 2, dtype=F32) / HEAD_DIM)
    ang = pos[:, None] * inv_freq[None, :]
    cos, sin = jnp.cos(ang), jnp.sin(ang)
    return jnp.concatenate([cos, cos], axis=1), jnp.concatenate([-sin, sin], axis=1)


def _reorder_w_in(w):
    o = 0
    q_a, o = w[:, o:o + A_WIDTH], o + A_WIDTH
    k_a, o = w[:, o:o + A_WIDTH], o + A_WIDTH
    v_a, o = w[:, o:o + A_WIDTH], o + A_WIDTH
    q_b, o = w[:, o:o + GLA_KEY], o + GLA_KEY
    k_b, o = w[:, o:o + GLA_KEY], o + GLA_KEY
    v_b, o = w[:, o:o + GLA_VAL], o + GLA_VAL
    r_b, o = w[:, o:o + GLA_VAL], o + GLA_VAL
    a_l, o = w[:, o:o + GLA_LOWRANK], o + GLA_LOWRANK
    g_a, o = w[:, o:o + D_MODEL], o + D_MODEL
    g_b, o = w[:, o:o + D_MODEL], o + D_MODEL
    cols = [g_a, g_b, v_b, r_b]
    for g in range(len(DILATED_GROUPS)):
        gs = slice(g * A_OUT, (g + 1) * A_OUT)
        cols += [q_a[:, gs], k_a[:, gs], v_a[:, gs]]
    cols += [q_b, k_b]
    w_main = jnp.concatenate(cols, axis=1).astype(BF16)
    w_alow = jnp.pad(a_l, ((0, 0), (0, LANES - GLA_LOWRANK))).astype(BF16)
    return w_main, w_alow


def kernel(x, c, w_ada, b_ada, norm_mix, norm_ffn, w_in, w_alpha2, b_alpha2, gla_gain, w_out_a, w_out_b, w_out,
           w_router, b_router, w_gate_e, w_up_e, w_down_e, final_norm):
    b, s, d = x.shape
    depth = w_ada.shape[0]
    t = b * s
    n_blk = (t * TOP_K) // MOE_BLOCK + N_EXPERTS
    cosf, sinf = _rope_tables()
    ada = _ada(c, w_ada, b_ada).reshape(depth, b, 6, d)
    w_router_p = jnp.pad(w_router, ((0, 0), (0, LANES - N_EXPERTS)))
    b_router_p = jnp.pad(b_router, (0, LANES - N_EXPERTS)).reshape(1, LANES)
    final_gain = final_norm.reshape(1, d)
    x2 = x.reshape(t, d)
    for l in range(depth):
        w_main, w_alow = _reorder_w_in(w_in[l])
        w_alpha2p = jnp.pad(w_alpha2[l], ((0, LANES - GLA_LOWRANK), (0, 0))).astype(BF16)
        h, a_low = _norm_mod(x2, ada[l], norm_mix[l].reshape(1, d), w_alow)
        proj = _in_proj(h, w_main)
        o_list, l_list = [], []
        for g in range(len(DILATED_GROUPS)):
            o_g, l_g = _attention(proj, cosf, sinf, g)
            o_list.append(o_g)
            l_list.append(l_g)
        y_b = _gla(proj, a_low, w_alpha2p, b_alpha2[l].reshape(1, GLA_KEY), gla_gain[l].reshape(1, GLA_VAL))
        x2, h2, eidx, gw = _post_mixer(
            x2, proj, o_list, l_list, y_b, ada[l],
            w_out_a[l].astype(BF16), w_out_b[l].astype(BF16), w_out[l].astype(BF16),
            norm_ffn[l].reshape(1, d), w_router_p, b_router_p)
        blk_e, n_used, dest = _dispatch_plan(eidx, n_blk)
        ybuf = _moe_experts(blk_e, n_used, dest, h2,
                            w_gate_e[l].astype(BF16), w_up_e[l].astype(BF16), w_down_e[l].astype(BF16))
        x2 = _combine(dest, ybuf, x2, gw, ada[l], final_gain, final=(l == depth - 1))
    return x2.reshape(b, s, d)
```

```python
import functools

import jax
import jax.numpy as jnp
from jax import lax
from jax.experimental import pallas as pl
from jax.experimental.pallas import tpu as pltpu

F32 = jnp.float32
BF16 = jnp.bfloat16

D_MODEL = 2048
SEQ = 2048
HEAD_DIM = 128
A_HEADS_PER_GROUP = 4
DILATED_GROUPS = ((128, 1), (512, 4), (2048, 16))
A_WIDTH = 12 * HEAD_DIM
A_OUT = A_HEADS_PER_GROUP * HEAD_DIM
ATT_BLOCK = 128
ROPE_THETA = 10000.0
GLA_HEADS = 4
GLA_DK = 128
GLA_DV = 256
GLA_KEY = GLA_HEADS * GLA_DK
GLA_VAL = GLA_HEADS * GLA_DV
GLA_LOWRANK = 16
GLA_TAU = 16.0
GLA_CHUNK = 64
N_EXPERTS = 16
N_GROUPS = 4
EXPERTS_PER_GROUP = N_EXPERTS // N_GROUPS
TOP_K = 2
D_FF_EXPERT = 1024
NORM_EPS = 1e-6

LANES = 128
MOE_BLOCK = 256
PROJ_TILE = 512

COL_QA = 0
COL_KA = A_WIDTH
COL_VA = 2 * A_WIDTH
COL_QB = 3 * A_WIDTH
COL_KB = COL_QB + GLA_KEY
COL_VB = COL_KB + GLA_KEY
COL_RB = COL_VB + GLA_VAL
COL_ALOW = COL_RB + GLA_VAL
COL_GATES = COL_ALOW + GLA_LOWRANK


def _params(vmem_mib, sem=None):
    kw = dict(vmem_limit_bytes=int(vmem_mib) << 20)
    if sem is not None:
        kw["dimension_semantics"] = sem
    return pltpu.CompilerParams(**kw)


def _sigmoid(x):
    return 0.5 * jnp.tanh(0.5 * x) + 0.5


def _ada_kernel(c_ref, w_ref, b_ref, o_ref):
    c = c_ref[...]
    ca = (c * _sigmoid(c)).astype(BF16)
    o_ref[0] = jnp.dot(ca, w_ref[0].astype(BF16), preferred_element_type=F32) + b_ref[0]


def _ada(c, w_ada, b_ada):
    depth, d, n = w_ada.shape
    b = c.shape[0]
    tn = 1024
    return pl.pallas_call(
        _ada_kernel,
        out_shape=jax.ShapeDtypeStruct((depth, b, n), F32),
        grid=(depth, n // tn),
        in_specs=[
            pl.BlockSpec((b, d), lambda l, j: (0, 0)),
            pl.BlockSpec((1, d, tn), lambda l, j: (l, 0, j)),
            pl.BlockSpec((1, 1, tn), lambda l, j: (l, 0, j)),
        ],
        out_specs=pl.BlockSpec((1, b, tn), lambda l, j: (l, 0, j)),
        compiler_params=_params(40),
        name="ada",
    )(c, w_ada, b_ada.reshape(depth, 1, n))


def _norm_mod_kernel(x_ref, ada_ref, gain_ref, wal_ref, h_ref, al_ref):
    x = x_ref[...]
    y = x * lax.rsqrt(jnp.mean(x * x, axis=-1, keepdims=True) + NORM_EPS) * gain_ref[...]
    h = (y * (1.0 + ada_ref[0, 1:2, :]) + ada_ref[0, 0:1, :]).astype(BF16)
    h_ref[...] = h
    al_ref[...] = jnp.dot(h, wal_ref[...], preferred_element_type=F32)


def _norm_mod(x2, ada_l, gain, w_alow):
    t, d = x2.shape
    tm = 512
    per_b = SEQ // tm
    return pl.pallas_call(
        _norm_mod_kernel,
        out_shape=(jax.ShapeDtypeStruct((t, d), BF16), jax.ShapeDtypeStruct((t, LANES), F32)),
        grid=(t // tm,),
        in_specs=[
            pl.BlockSpec((tm, d), lambda i: (i, 0)),
            pl.BlockSpec((1, 6, d), lambda i: (i // per_b, 0, 0)),
            pl.BlockSpec((1, d), lambda i: (0, 0)),
            pl.BlockSpec((d, LANES), lambda i: (0, 0)),
        ],
        out_specs=(pl.BlockSpec((tm, d), lambda i: (i, 0)), pl.BlockSpec((tm, LANES), lambda i: (i, 0))),
        compiler_params=_params(40),
        name="norm_mod",
    )(x2, ada_l, gain, w_alow)


def _matmul_kernel(a_ref, b_ref, o_ref):
    o_ref[...] = jnp.dot(a_ref[...], b_ref[...].astype(BF16), preferred_element_type=F32).astype(o_ref.dtype)


def _in_proj(h, w, n_cols, name):
    t, d = h.shape
    tm, tn = 2048, PROJ_TILE
    return pl.pallas_call(
        _matmul_kernel,
        out_shape=jax.ShapeDtypeStruct((t, n_cols), BF16),
        grid=(t // tm, n_cols // tn),
        in_specs=[pl.BlockSpec((tm, d), lambda i, j: (i, 0)), pl.BlockSpec((d, tn), lambda i, j: (0, j))],
        out_specs=pl.BlockSpec((tm, tn), lambda i, j: (i, j)),
        compiler_params=_params(48),
        name=name,
    )(h, w)


def _attn_kernel(q_ref, k_ref, v_ref, cos_ref, sin_ref, o_ref, lse_ref, q_s, k_s, v_s, o_s, l_s, *, dil, nb, wd):
    scale = HEAD_DIM ** -0.5
    blk = ATT_BLOCK
    nh = A_HEADS_PER_GROUP
    cosf = cos_ref[...]
    sinf = sin_ref[...]
    for h in range(nh):
        hc = slice(h * HEAD_DIM, (h + 1) * HEAD_DIM)
        tq = q_ref[:, hc].astype(F32)
        q_s[h] = (tq * cosf + pltpu.roll(tq, HEAD_DIM // 2, 1) * sinf) * scale
        tk = k_ref[:, hc].astype(F32)
        k_s[h] = tk * cosf + pltpu.roll(tk, HEAD_DIM // 2, 1) * sinf
        v_s[h] = v_ref[:, hc].astype(F32)

    def rows(start, size):
        return pl.ds(start, size) if dil == 1 else pl.ds(start, size, stride=dil)

    def head_mask(nk, offset):
        i = lax.broadcasted_iota(jnp.int32, (nh * blk, nk), 0) % blk
        j = lax.broadcasted_iota(jnp.int32, (nh * blk, nk), 1)
        dist = offset + i - j
        return (dist >= 0) & (dist <= wd)

    mask_first = head_mask(blk, 0)
    mask_later = head_mask(2 * blk, blk)
    lane = lax.broadcasted_iota(jnp.int32, (blk, LANES), 1)
    ones = jnp.ones((2 * blk, HEAD_DIM), BF16)

    def attend(qs, ks, nk, mask):
        s = jnp.concatenate(
            [lax.dot_general(q_s[h, rows(qs, blk), :].astype(BF16), k_s[h, rows(ks, nk), :].astype(BF16),
                             (((1,), (1,)), ((), ())), preferred_element_type=F32) for h in range(nh)], axis=0)
        s = jnp.where(mask, s, -jnp.inf)
        m = jnp.max(s, axis=-1, keepdims=True)
        p = jnp.exp(s - m).astype(BF16)
        lse_tile = jnp.zeros((blk, LANES), F32)
        for h in range(nh):
            v_aug = jnp.concatenate([v_s[h, rows(ks, nk), :].astype(BF16), ones[:nk]], axis=1)
            ov = jnp.dot(p[h * blk:(h + 1) * blk], v_aug, preferred_element_type=F32)
            den = ov[:, HEAD_DIM:]
            o_s[h, rows(qs, blk), :] = ov[:, :HEAD_DIM] / den
            lse_tile = jnp.where(lane == h, m[h * blk:(h + 1) * blk] + jnp.log(den), lse_tile)
        l_s[rows(qs, blk), :] = lse_tile

    def per_phase(ph, carry):
        attend(ph, ph, blk, mask_first)

        def per_blk(n, c):
            qs = ph + dil * blk * n
            ks = ph + dil * blk * (n - 1)
            if dil == 1:
                qs = pl.multiple_of(qs, blk)
                ks = pl.multiple_of(ks, blk)
            attend(qs, ks, 2 * blk, mask_later)
            return c

        if nb > 1:
            lax.fori_loop(1, nb, per_blk, 0, unroll=3)
        return carry

    lax.fori_loop(0, dil, per_phase, 0, unroll=4 if nb == 1 else 1)
    for h in range(nh):
        o_ref[:, h * HEAD_DIM:(h + 1) * HEAD_DIM] = o_s[h].astype(o_ref.dtype)
    lse_ref[...] = l_s[...]


def _attention(proj, cosf, sinf, group):
    window, dil = DILATED_GROUPS[group]
    t = proj.shape[0]
    nb = (SEQ // dil) // ATT_BLOCK
    col = lambda base: base // A_OUT + group
    kern = functools.partial(_attn_kernel, dil=dil, nb=nb, wd=window // dil)
    return pl.pallas_call(
        kern,
        out_shape=(jax.ShapeDtypeStruct((t, A_OUT), BF16), jax.ShapeDtypeStruct((t, LANES), F32)),
        grid=(t // SEQ,),
        in_specs=[
            pl.BlockSpec((SEQ, A_OUT), lambda b: (b, col(COL_QA))),
            pl.BlockSpec((SEQ, A_OUT), lambda b: (b, col(COL_KA))),
            pl.BlockSpec((SEQ, A_OUT), lambda b: (b, col(COL_VA))),
            pl.BlockSpec((SEQ, HEAD_DIM), lambda b: (0, 0)),
            pl.BlockSpec((SEQ, HEAD_DIM), lambda b: (0, 0)),
        ],
        out_specs=(pl.BlockSpec((SEQ, A_OUT), lambda b: (b, 0)), pl.BlockSpec((SEQ, LANES), lambda b: (b, 0))),
        scratch_shapes=[pltpu.VMEM((A_HEADS_PER_GROUP, SEQ, HEAD_DIM), F32) for _ in range(4)]
        + [pltpu.VMEM((SEQ, LANES), F32)],
        compiler_params=_params(48),
        name=f"dilated_attn_g{group}",
    )(proj, proj, proj, cosf, sinf)


def _gla_kernel(q_ref, k_ref, v0_ref, v1_ref, r0_ref, r1_ref, al_ref, wa_ref, ba_ref, gain_ref, y_ref, st_ref, *, rows):
    @pl.when(pl.program_id(1) == 0)
    def _():
        st_ref[...] = jnp.zeros_like(st_ref)

    ck = GLA_CHUNK
    half = GLA_HEADS // 2
    z = jnp.dot(al_ref[...].astype(BF16), wa_ref[...], preferred_element_type=F32) + ba_ref[...]
    log_a = (jnp.minimum(z, 0.0) - jnp.log(1.0 + jnp.exp(-jnp.abs(z)))) * (1.0 / GLA_TAU)
    ri = lax.broadcasted_iota(jnp.int32, (rows, rows), 0)
    ci = lax.broadcasted_iota(jnp.int32, (rows, rows), 1)
    same = (ri // ck) == (ci // ck)
    tri = (same & (ri >= ci)).astype(BF16)
    tot = same.astype(BF16)
    la_hi = log_a.astype(BF16)
    la_lo = (log_a - la_hi.astype(F32)).astype(BF16)
    b = jnp.dot(tri, la_hi, preferred_element_type=F32) + jnp.dot(tri, la_lo, preferred_element_type=F32)
    b_end = jnp.dot(tot, la_hi, preferred_element_type=F32) + jnp.dot(tot, la_lo, preferred_element_type=F32)
    q = q_ref[...].astype(F32) * (GLA_DK ** -0.5)
    k = k_ref[...].astype(F32)
    q_dec = (q * jnp.exp(b)).astype(BF16)
    k_inv = (k * jnp.exp(-b)).astype(BF16)
    k_end = (k * jnp.exp(b_end - b)).astype(BF16)
    decay = jnp.exp(b_end)
    r1 = lax.broadcasted_iota(jnp.int32, (ck, ck), 0)
    c1 = lax.broadcasted_iota(jnp.int32, (ck, ck), 1)
    causal = r1 >= c1
    nt = (((1,), (1,)), ((), ()))
    v_refs = (v0_ref, v1_ref)
    r_refs = (r0_ref, r1_ref)
    for c in range(rows // ck):
        rs = slice(c * ck, (c + 1) * ck)
        for h in range(GLA_HEADS):
            hk = slice(h * GLA_DK, (h + 1) * GLA_DK)
            hv = slice(h * GLA_DV, (h + 1) * GLA_DV)
            hb = slice((h % half) * GLA_DV, (h % half + 1) * GLA_DV)
            v = v_refs[h // half][rs, hb]
            attn = lax.dot_general(q_dec[rs, hk], k_inv[rs, hk], nt, preferred_element_type=F32)
            attn = jnp.where(causal, attn, 0.0).astype(BF16)
            st = st_ref[h]
            o = jnp.dot(attn, v, preferred_element_type=F32)
            o = o + lax.dot_general(q_dec[rs, hk], st.astype(BF16), nt, preferred_element_type=F32)
            v_t = v.astype(F32).T.astype(BF16)
            st_ref[h] = st * decay[c * ck:c * ck + 1, hk] + jnp.dot(v_t, k_end[rs, hk], preferred_element_type=F32)
            o = o * lax.rsqrt(jnp.mean(o * o, axis=-1, keepdims=True) + NORM_EPS) * gain_ref[:, hv]
            r = r_refs[h // half][rs, hb].astype(F32)
            y_ref[rs, hv] = (o * (r * _sigmoid(r))).astype(y_ref.dtype)


def _gla(proj, a_low, w_alpha2p, b_alpha2, gla_gain):
    t = proj.shape[0]
    rows = 256
    per_b = SEQ // rows
    kern = functools.partial(_gla_kernel, rows=rows)
    col = lambda base, j=0: (lambda b, s: (b * per_b + s, base // GLA_KEY + j))
    return pl.pallas_call(
        kern,
        out_shape=jax.ShapeDtypeStruct((t, GLA_VAL), BF16),
        grid=(t // SEQ, per_b),
        in_specs=[
            pl.BlockSpec((rows, GLA_KEY), col(COL_QB)),
            pl.BlockSpec((rows, GLA_KEY), col(COL_KB)),
            pl.BlockSpec((rows, GLA_KEY), col(COL_VB, 0)),
            pl.BlockSpec((rows, GLA_KEY), col(COL_VB, 1)),
            pl.BlockSpec((rows, GLA_KEY), col(COL_RB, 0)),
            pl.BlockSpec((rows, GLA_KEY), col(COL_RB, 1)),
            pl.BlockSpec((rows, LANES), lambda b, s: (b * per_b + s, 0)),
            pl.BlockSpec((LANES, GLA_KEY), lambda b, s: (0, 0)),
            pl.BlockSpec((1, GLA_KEY), lambda b, s: (0, 0)),
            pl.BlockSpec((1, GLA_VAL), lambda b, s: (0, 0)),
        ],
        out_specs=pl.BlockSpec((rows, GLA_VAL), lambda b, s: (b * per_b + s, 0)),
        scratch_shapes=[pltpu.VMEM((GLA_HEADS, GLA_DV, GLA_DK), F32)],
        compiler_params=_params(32),
        name="gla",
    )(proj, proj, proj, proj, proj, proj, a_low, w_alpha2p, b_alpha2, gla_gain)


def _post_mixer_kernel(x_ref, ga_ref, gb_ref, o0_ref, o1_ref, o2_ref, l0_ref, l1_ref, l2_ref, yb_ref, ada_ref,
                       woa_ref, wob_ref, wout_ref, g2_ref, wr_ref, br_ref,
                       xo_ref, h2_ref, e_ref, w_ref):
    ls = [l0_ref[...], l1_ref[...], l2_ref[...]]
    os_ = [o0_ref, o1_ref, o2_ref]
    mx = jnp.maximum(jnp.maximum(ls[0], ls[1]), ls[2])
    es = [jnp.exp(l - mx) for l in ls]
    zs = es[0] + es[1] + es[2]
    wt = [e / zs for e in es]
    parts = []
    for h in range(A_HEADS_PER_GROUP):
        hc = slice(h * HEAD_DIM, (h + 1) * HEAD_DIM)
        acc = wt[0][:, h:h + 1] * os_[0][:, hc].astype(F32)
        acc = acc + wt[1][:, h:h + 1] * os_[1][:, hc].astype(F32)
        acc = acc + wt[2][:, h:h + 1] * os_[2][:, hc].astype(F32)
        parts.append(acc)
    ya = jnp.concatenate(parts, axis=1).astype(BF16)
    ta = jnp.dot(ya, woa_ref[...], preferred_element_type=F32)
    tb = jnp.dot(yb_ref[...], wob_ref[...], preferred_element_type=F32)
    merged = _sigmoid(ga_ref[...].astype(F32)) * ta + _sigmoid(gb_ref[...].astype(F32)) * tb
    mix = jnp.dot(merged.astype(BF16), wout_ref[...], preferred_element_type=F32)
    xn = x_ref[...] + ada_ref[0, 2:3, :] * mix
    xo_ref[...] = xn
    y = xn * lax.rsqrt(jnp.mean(xn * xn, axis=-1, keepdims=True) + NORM_EPS) * g2_ref[...]
    h2 = y * (1.0 + ada_ref[0, 4:5, :]) + ada_ref[0, 3:4, :]
    h2_ref[...] = h2

    logits = lax.dot_general(wr_ref[...], h2.astype(BF16), (((1,), (1,)), ((), ())), preferred_element_type=F32)
    tm = logits.shape[1]
    aff_all = _sigmoid(logits)
    sel_all = aff_all + br_ref[...]
    sel = [sel_all[e:e + 1, :] for e in range(N_EXPERTS)]
    aff = [aff_all[e:e + 1, :] for e in range(N_EXPERTS)]
    neg = -jnp.inf

    def top2_sum(v0, v1, v2, v3):
        hi01, lo01 = jnp.maximum(v0, v1), jnp.minimum(v0, v1)
        hi23, lo23 = jnp.maximum(v2, v3), jnp.minimum(v2, v3)
        return jnp.maximum(hi01, hi23) + jnp.maximum(jnp.minimum(hi01, hi23), jnp.maximum(lo01, lo23))

    epg = EXPERTS_PER_GROUP
    scores = [top2_sum(*sel[g * epg:(g + 1) * epg]) for g in range(N_GROUPS)]
    best = jnp.zeros_like(scores[0])
    best_s = scores[0]
    for g in range(1, N_GROUPS):
        better = scores[g] > best_s
        best = jnp.where(better, float(g), best)
        best_s = jnp.where(better, scores[g], best_s)
    vals, affs = [], []
    for j in range(epg):
        vj, aj = sel[j], aff[j]
        for g in range(1, N_GROUPS):
            in_g = best == float(g)
            vj = jnp.where(in_g, sel[g * epg + j], vj)
            aj = jnp.where(in_g, aff[g * epg + j], aj)
        vals.append(vj)
        affs.append(aj)

    def first_argmax(vs):
        idx, m, a = jnp.zeros_like(vs[0]), vs[0], affs[0]
        for j in range(1, epg):
            gt = vs[j] > m
            idx = jnp.where(gt, float(j), idx)
            m = jnp.where(gt, vs[j], m)
            a = jnp.where(gt, affs[j], a)
        return idx, a

    i1, a1 = first_argmax(vals)
    i2, a2 = first_argmax([jnp.where(i1 == float(j), neg, vals[j]) for j in range(epg)])
    tot = a1 + a2
    srow = lax.broadcasted_iota(jnp.int32, (8, tm), 0)
    packed = jnp.where(srow == 0, best * epg + i1,
                       jnp.where(srow == 1, best * epg + i2,
                                 jnp.where(srow == 2, a1 / tot, jnp.where(srow == 3, a2 / tot, 0.0))))
    packed_t = jnp.concatenate([packed, jnp.zeros((LANES - 8, tm), F32)], axis=0).T
    e_ref[...] = packed_t[:, 0:TOP_K].astype(jnp.int32)
    w_ref[...] = packed_t[:, TOP_K:2 * TOP_K]


def _post_mixer(x2, gates, o_list, l_list, y_b, ada_l, woa, wob, wout, gain2, w_router_t, b_router_c):
    t, d = x2.shape
    tm = 256
    per_b = SEQ // tm
    const = lambda shape: pl.BlockSpec(shape, lambda i: (0,) * len(shape), pipeline_mode=pl.Buffered(1))
    rowblk = lambda w: pl.BlockSpec((tm, w), lambda i: (i, 0))
    return pl.pallas_call(
        _post_mixer_kernel,
        out_shape=(
            jax.ShapeDtypeStruct((t, d), F32),
            jax.ShapeDtypeStruct((t, d), F32),
            jax.ShapeDtypeStruct((t, TOP_K), jnp.int32),
            jax.ShapeDtypeStruct((t, TOP_K), F32),
        ),
        grid=(t // tm,),
        in_specs=[
            rowblk(d),
            pl.BlockSpec((tm, d), lambda i: (i, 0)),
            pl.BlockSpec((tm, d), lambda i: (i, 1)),
            rowblk(A_OUT), rowblk(A_OUT), rowblk(A_OUT),
            rowblk(LANES), rowblk(LANES), rowblk(LANES),
            rowblk(GLA_VAL),
            pl.BlockSpec((1, 6, d), lambda i: (i // per_b, 0, 0)),
            const((A_OUT, d)), const((GLA_VAL, d)), const((d, d)),
            const((1, d)), const((LANES, d)), const((LANES, 1)),
        ],
        out_specs=(rowblk(d), rowblk(d), rowblk(TOP_K), rowblk(TOP_K)),
        compiler_params=_params(56),
        name="post_mixer",
    )(x2, gates, gates, *o_list, *l_list, y_b, ada_l, woa, wob, wout, gain2, w_router_t, b_router_c)


def _row_copy(src_hbm, row, dst, slot, r, sem):
    return pltpu.make_async_copy(src_hbm.at[pl.ds(row, 1), :], dst.at[slot, pl.ds(r, 1), :], sem.at[slot])


def _moe_kernel(be_ref, nu_ref, dest_ref, h_hbm, zero_hbm, wg_ref, wu_ref, wd_ref, y_ref, xg, src_s, sem, zsem):
    i = pl.program_id(0)
    n_used = nu_ref[0]
    blk = MOE_BLOCK
    slot = i % 2

    @pl.when(i == 0)
    def _():
        fill = pltpu.make_async_copy(zero_hbm, src_s, zsem)
        fill.start()
        fill.wait()

        def scat(j, c):
            src_s[dest_ref[j]] = lax.shift_right_logical(j, 1)
            return c
        lax.fori_loop(0, dest_ref.shape[0], scat, 0, unroll=8)

    def gather(block, s, start):
        def body(r, c):
            cp = _row_copy(h_hbm, src_s[block * blk + r], xg, s, r, sem)
            if start:
                cp.start()
            else:
                cp.wait()
            return c
        lax.fori_loop(0, blk, body, 0, unroll=8)

    @pl.when(i == 0)
    def _():
        gather(0, 0, True)

    n_blk = pl.num_programs(0)

    @pl.when(i < n_used)
    def _():
        gather(i, slot, False)
        nxt = jnp.minimum(i + 1, n_blk - 1) * blk
        def issue(quarter):
            for r in range(quarter * (blk // 4), (quarter + 1) * (blk // 4)):
                _row_copy(h_hbm, src_s[nxt + r], xg, 1 - slot, r, sem).start()

        xb = xg[slot].astype(BF16)
        issue(0)
        g = jnp.dot(xb, wg_ref[0], preferred_element_type=F32)
        issue(1)
        u = jnp.dot(xb, wu_ref[0], preferred_element_type=F32)
        issue(2)
        a = (g * _sigmoid(g) * u).astype(BF16)
        y_ref[...] = jnp.dot(a, wd_ref[0], preferred_element_type=F32)
        issue(3)

        @pl.when(i == n_blk - 1)
        def _():
            gather(i, 1 - slot, False)

    @pl.when((i == n_used) & (i > 0))
    def _():
        gather(i, slot, False)

    @pl.when(i >= n_used)
    def _():
        y_ref[...] = jnp.zeros_like(y_ref)


def _moe_experts(blk_e, n_used, dest, h2, wg, wu, wd):
    t, d = h2.shape
    n_blk = blk_e.shape[0]
    f = wg.shape[2]
    grid_spec = pltpu.PrefetchScalarGridSpec(
        num_scalar_prefetch=3,
        grid=(n_blk,),
        in_specs=[
            pl.BlockSpec(memory_space=pl.ANY),
            pl.BlockSpec(memory_space=pl.ANY),
            pl.BlockSpec((1, d, f), lambda i, be, nu, sr: (be[i], 0, 0)),
            pl.BlockSpec((1, d, f), lambda i, be, nu, sr: (be[i], 0, 0)),
            pl.BlockSpec((1, f, d), lambda i, be, nu, sr: (be[i], 0, 0)),
        ],
        out_specs=pl.BlockSpec((MOE_BLOCK, d), lambda i, be, nu, sr: (i, 0)),
        scratch_shapes=[pltpu.VMEM((2, MOE_BLOCK, d), F32), pltpu.SMEM((n_blk * MOE_BLOCK,), jnp.int32),
                        pltpu.SemaphoreType.DMA((2,)), pltpu.SemaphoreType.DMA(())],
    )
    zero_rows = jnp.zeros((n_blk * MOE_BLOCK,), jnp.int32)
    return pl.pallas_call(
        _moe_kernel,
        out_shape=jax.ShapeDtypeStruct((n_blk * MOE_BLOCK, d), F32),
        grid_spec=grid_spec,
        compiler_params=_params(48, ("arbitrary",)),
        name="moe_experts",
    )(blk_e, n_used, dest, h2, zero_rows, wg, wu, wd)


def _pair_copy(src_hbm, row, dst, slot, k, r, sem):
    return pltpu.make_async_copy(src_hbm.at[pl.ds(row, 1), :], dst.at[slot, k, pl.ds(r, 1), :], sem.at[slot])


def _combine_kernel(pos_ref, y_hbm, x_ref, w_ref, ada_ref, fg_ref, o_ref, yg, sem, *, tm, final):
    i = pl.program_id(0)
    n = pl.num_programs(0)
    slot = i % 2

    def gather(tile, s, start):
        def body(r, c):
            for k in range(TOP_K):
                cp = _pair_copy(y_hbm, pos_ref[(tile * tm + r) * TOP_K + k], yg, s, k, r, sem)
                if start:
                    cp.start()
                else:
                    cp.wait()
            return c
        lax.fori_loop(0, tm, body, 0, unroll=8)

    @pl.when(i == 0)
    def _():
        gather(0, 0, True)

    @pl.when(i + 1 < n)
    def _():
        gather(i + 1, 1 - slot, True)

    gather(i, slot, False)
    w = w_ref[...]
    y = w[:, 0:1] * yg[slot, 0] + w[:, 1:2] * yg[slot, 1]
    out = x_ref[...] + ada_ref[0, 5:6, :] * y
    if final:
        out = out * lax.rsqrt(jnp.mean(out * out, axis=-1, keepdims=True) + NORM_EPS) * fg_ref[...]
    o_ref[...] = out


def _combine(pos, ybuf, x2, gw, ada_l, final_gain, final):
    t, d = x2.shape
    tm = 256
    per_b = SEQ // tm
    kern = functools.partial(_combine_kernel, tm=tm, final=final)
    grid_spec = pltpu.PrefetchScalarGridSpec(
        num_scalar_prefetch=1,
        grid=(t // tm,),
        in_specs=[
            pl.BlockSpec(memory_space=pl.ANY),
            pl.BlockSpec((tm, d), lambda i, p: (i, 0)),
            pl.BlockSpec((tm, TOP_K), lambda i, p: (i, 0)),
            pl.BlockSpec((1, 6, d), lambda i, p: (i // per_b, 0, 0)),
            pl.BlockSpec((1, d), lambda i, p: (0, 0)),
        ],
        out_specs=pl.BlockSpec((tm, d), lambda i, p: (i, 0)),
        scratch_shapes=[pltpu.VMEM((2, TOP_K, tm, d), F32), pltpu.SemaphoreType.DMA((2,))],
    )
    return pl.pallas_call(
        kern,
        out_shape=jax.ShapeDtypeStruct((t, d), F32),
        grid_spec=grid_spec,
        compiler_params=_params(40, ("arbitrary",)),
        name="moe_combine_final" if final else "moe_combine",
    )(pos, ybuf, x2, gw, ada_l, final_gain)


def _plan_kernel(e_ref, dest_ref, cnt_ref, carry, pstart_s, *, tm):
    ph = pl.program_id(0)
    i = pl.program_id(1)
    lane = lax.broadcasted_iota(jnp.int32, (tm, LANES), 1)
    e = e_ref[...]
    oh0 = e[:, 0:1] == lane
    oh1 = e[:, 1:2] == lane
    oh = oh0.astype(F32) + oh1.astype(F32)
    tile_counts = jnp.sum(oh, axis=0, keepdims=True)

    @pl.when((ph == 0) & (i == 0))
    def _():
        carry[...] = jnp.zeros_like(carry)

    @pl.when(ph == 0)
    def _():
        carry[...] += tile_counts

    @pl.when((ph == 1) & (i == 0))
    def _():
        counts = carry[...]
        cnt_ref[...] = counts.astype(jnp.int32)
        nblk = jnp.floor((counts + (MOE_BLOCK - 1)) * (1.0 / MOE_BLOCK))
        r = lax.broadcasted_iota(jnp.int32, (LANES, LANES), 0)
        c = lax.broadcasted_iota(jnp.int32, (LANES, LANES), 1)
        upper = (r < c).astype(BF16)
        nb8 = jnp.broadcast_to(nblk, (8, LANES)).astype(BF16)
        pstart_s[...] = jnp.dot(nb8, upper, preferred_element_type=F32)[0:1, :] * float(MOE_BLOCK)
        carry[...] = jnp.zeros_like(carry)

    @pl.when(ph == 1)
    def _():
        r = lax.broadcasted_iota(jnp.int32, (tm, tm), 0)
        c = lax.broadcasted_iota(jnp.int32, (tm, tm), 1)
        lower = (r > c).astype(BF16)
        before = jnp.dot(lower, oh.astype(BF16), preferred_element_type=F32) + carry[...]
        base = before + pstart_s[...]
        d0 = jnp.sum(jnp.where(oh0, base, 0.0), axis=1, keepdims=True)
        d1 = jnp.sum(jnp.where(oh1, base, 0.0), axis=1, keepdims=True)
        col = lax.broadcasted_iota(jnp.int32, (tm, TOP_K), 1)
        dest_ref[...] = jnp.where(col == 0, d0, d1).astype(jnp.int32)
        carry[...] += tile_counts


def _dispatch_plan(eidx, n_blk):
    t = eidx.shape[0]
    tm = 512
    dest, counts = pl.pallas_call(
        functools.partial(_plan_kernel, tm=tm),
        out_shape=(jax.ShapeDtypeStruct((t, TOP_K), jnp.int32), jax.ShapeDtypeStruct((1, LANES), jnp.int32)),
        grid=(2, t // tm),
        in_specs=[pl.BlockSpec((tm, TOP_K), lambda p, i: (i, 0))],
        out_specs=(pl.BlockSpec((tm, TOP_K), lambda p, i: (i * p, 0)), pl.BlockSpec((1, LANES), lambda p, i: (0, 0))),
        scratch_shapes=[pltpu.VMEM((1, LANES), F32), pltpu.VMEM((1, LANES), F32)],
        compiler_params=_params(32, ("arbitrary", "arbitrary")),
        name="dispatch_plan",
    )(eidx)
    counts = counts[0, :N_EXPERTS]
    pcounts = (counts + MOE_BLOCK - 1) // MOE_BLOCK * MOE_BLOCK
    pend = jnp.cumsum(pcounts)
    blk_start = jnp.arange(n_blk, dtype=jnp.int32) * MOE_BLOCK
    blk_e = jnp.sum((pend[None, :] <= blk_start[:, None]).astype(jnp.int32), axis=1)
    blk_e = jnp.minimum(blk_e, N_EXPERTS - 1).astype(jnp.int32)
    n_used = (pend[-1] // MOE_BLOCK).astype(jnp.int32).reshape(1)
    return blk_e, n_used, dest.reshape(-1)


def _rope_tables():
    pos = jnp.arange(SEQ, dtype=F32)
    inv_freq = ROPE_THETA ** (-jnp.arange(0, HEAD_DIM, 2, dtype=F32) / HEAD_DIM)
    ang = pos[:, None] * inv_freq[None, :]
    cos, sin = jnp.cos(ang), jnp.sin(ang)
    return jnp.concatenate([cos, cos], axis=1), jnp.concatenate([-sin, sin], axis=1)


def kernel(x, c, w_ada, b_ada, norm_mix, norm_ffn, w_in, w_alpha2, b_alpha2, gla_gain, w_out_a, w_out_b, w_out,
           w_router, b_router, w_gate_e, w_up_e, w_down_e, final_norm):
    b, s, d = x.shape
    depth = w_ada.shape[0]
    t = b * s
    n_blk = (t * TOP_K) // MOE_BLOCK + N_EXPERTS
    cosf, sinf = _rope_tables()
    ada = _ada(c, w_ada, b_ada).reshape(depth, b, 6, d)
    w_router_t = jnp.pad(w_router.T, ((0, LANES - N_EXPERTS), (0, 0))).astype(BF16)
    b_router_c = jnp.pad(b_router, (0, LANES - N_EXPERTS)).reshape(LANES, 1)
    final_gain = final_norm.reshape(1, d)
    x2 = x.reshape(t, d)
    for l in range(depth):
        w_alow = jnp.pad(w_in[l][:, COL_ALOW:COL_GATES], ((0, 0), (0, LANES - GLA_LOWRANK))).astype(BF16)
        w_gates = w_in[l][:, COL_GATES:].astype(BF16)
        w_alpha2p = jnp.pad(w_alpha2[l], ((0, LANES - GLA_LOWRANK), (0, 0))).astype(BF16)
        h, a_low = _norm_mod(x2, ada[l], norm_mix[l].reshape(1, d), w_alow)
        proj = _in_proj(h, w_in[l], COL_ALOW, "in_proj")
        gates = _in_proj(h, w_gates, 2 * D_MODEL, "in_proj_gates")
        o_list, l_list = [], []
        for g in range(len(DILATED_GROUPS)):
            o_g, l_g = _attention(proj, cosf, sinf, g)
            o_list.append(o_g)
            l_list.append(l_g)
        y_b = _gla(proj, a_low, w_alpha2p, b_alpha2[l].reshape(1, GLA_KEY), gla_gain[l].reshape(1, GLA_VAL))
        x2, h2, eidx, gw = _post_mixer(
            x2, gates, o_list, l_list, y_b, ada[l],
            w_out_a[l].astype(BF16), w_out_b[l].astype(BF16), w_out[l].astype(BF16),
            norm_ffn[l].reshape(1, d), w_router_t, b_router_c)
        blk_e, n_used, dest = _dispatch_plan(eidx, n_blk)
        ybuf = _moe_experts(blk_e, n_used, dest, h2,
                            w_gate_e[l].astype(BF16), w_up_e[l].astype(BF16), w_down_e[l].astype(BF16))
        x2 = _combine(dest, ybuf, x2, gw, ada[l], final_gain, final=(l == depth - 1))
    return x2.reshape(b, s, d)
```

```python
import functools

import jax
import jax.numpy as jnp
from jax import lax
from jax.experimental import pallas as pl
from jax.experimental.pallas import tpu as pltpu

F32 = jnp.float32
BF16 = jnp.bfloat16

D_MODEL = 2048
SEQ = 2048
HEAD_DIM = 128
A_HEADS_PER_GROUP = 4
DILATED_GROUPS = ((128, 1), (512, 4), (2048, 16))
A_WIDTH = 12 * HEAD_DIM
A_OUT = A_HEADS_PER_GROUP * HEAD_DIM
ATT_BLOCK = 128
ROPE_THETA = 10000.0
GLA_HEADS = 4
GLA_DK = 128
GLA_DV = 256
GLA_KEY = GLA_HEADS * GLA_DK
GLA_VAL = GLA_HEADS * GLA_DV
GLA_LOWRANK = 16
GLA_TAU = 16.0
GLA_CHUNK = 64
N_EXPERTS = 16
N_GROUPS = 4
EXPERTS_PER_GROUP = N_EXPERTS // N_GROUPS
TOP_K = 2
D_FF_EXPERT = 1024
NORM_EPS = 1e-6

LANES = 128
MOE_BLOCK = 256
PROJ_TILE = 512

COL_QA = 0
COL_KA = A_WIDTH
COL_VA = 2 * A_WIDTH
COL_QB = 3 * A_WIDTH
COL_KB = COL_QB + GLA_KEY
COL_VB = COL_KB + GLA_KEY
COL_RB = COL_VB + GLA_VAL
COL_ALOW = COL_RB + GLA_VAL
COL_GATES = COL_ALOW + GLA_LOWRANK


def _params(vmem_mib, sem=None):
    kw = dict(vmem_limit_bytes=int(vmem_mib) << 20)
    if sem is not None:
        kw["dimension_semantics"] = sem
    return pltpu.CompilerParams(**kw)


def _sigmoid(x):
    return 0.5 * jnp.tanh(0.5 * x) + 0.5


def _ada_kernel(c_ref, w_ref, b_ref, o_ref):
    @pl.when(pl.program_id(1) == 0)
    def _():
        o_ref[0] = jnp.broadcast_to(b_ref[0], o_ref.shape[1:])

    c = c_ref[...]
    ca = (c * _sigmoid(c)).astype(BF16)
    o_ref[0] += jnp.dot(ca, w_ref[0].astype(BF16), preferred_element_type=F32)


def _ada(c, w_ada, b_ada):
    depth, d, n = w_ada.shape
    b = c.shape[0]
    tk = 256
    return pl.pallas_call(
        _ada_kernel,
        out_shape=jax.ShapeDtypeStruct((depth, b, n), F32),
        grid=(depth, d // tk),
        in_specs=[
            pl.BlockSpec((b, tk), lambda l, k: (0, k)),
            pl.BlockSpec((1, tk, n), lambda l, k: (l, k, 0)),
            pl.BlockSpec((1, 1, n), lambda l, k: (l, 0, 0)),
        ],
        out_specs=pl.BlockSpec((1, b, n), lambda l, k: (l, 0, 0)),
        compiler_params=_params(40, ("arbitrary", "arbitrary")),
        name="ada",
    )(c, w_ada, b_ada.reshape(depth, 1, n))


def _norm_mod_kernel(x_ref, ada_ref, gain_ref, wal_ref, h_ref, al_ref):
    x = x_ref[...]
    y = x * lax.rsqrt(jnp.mean(x * x, axis=-1, keepdims=True) + NORM_EPS) * gain_ref[...]
    h = (y * (1.0 + ada_ref[0, 1:2, :]) + ada_ref[0, 0:1, :]).astype(BF16)
    h_ref[...] = h
    al_ref[...] = jnp.dot(h, wal_ref[...], preferred_element_type=F32)


def _norm_mod(x2, ada_l, gain, w_alow):
    t, d = x2.shape
    tm = 512
    per_b = SEQ // tm
    return pl.pallas_call(
        _norm_mod_kernel,
        out_shape=(jax.ShapeDtypeStruct((t, d), BF16), jax.ShapeDtypeStruct((t, LANES), F32)),
        grid=(t // tm,),
        in_specs=[
            pl.BlockSpec((tm, d), lambda i: (i, 0)),
            pl.BlockSpec((1, 6, d), lambda i: (i // per_b, 0, 0)),
            pl.BlockSpec((1, d), lambda i: (0, 0)),
            pl.BlockSpec((d, LANES), lambda i: (0, 0)),
        ],
        out_specs=(pl.BlockSpec((tm, d), lambda i: (i, 0)), pl.BlockSpec((tm, LANES), lambda i: (i, 0))),
        compiler_params=_params(40),
        name="norm_mod",
    )(x2, ada_l, gain, w_alow)


def _matmul_kernel(a_ref, b_ref, o_ref):
    o_ref[...] = jnp.dot(a_ref[...], b_ref[0].astype(BF16), preferred_element_type=F32).astype(o_ref.dtype)


def _in_proj(h, w_stack, layer, n_cols, name):
    t, d = h.shape
    tm, tn = 2048, PROJ_TILE
    return pl.pallas_call(
        _matmul_kernel,
        out_shape=jax.ShapeDtypeStruct((t, n_cols), BF16),
        grid=(t // tm, n_cols // tn),
        in_specs=[pl.BlockSpec((tm, d), lambda i, j: (i, 0)), pl.BlockSpec((1, d, tn), lambda i, j: (layer, 0, j))],
        out_specs=pl.BlockSpec((tm, tn), lambda i, j: (i, j)),
        compiler_params=_params(48),
        name=name,
    )(h, w_stack)


def _attn_kernel(q_ref, k_ref, v_ref, cos_ref, sin_ref, o_ref, lse_ref, q_s, k_s, v_s, o_s, l_s, *, dil, nb, wd):
    scale = HEAD_DIM ** -0.5
    blk = ATT_BLOCK
    nh = A_HEADS_PER_GROUP
    cosf = cos_ref[...]
    sinf = sin_ref[...]
    for h in range(nh):
        hc = slice(h * HEAD_DIM, (h + 1) * HEAD_DIM)
        tq = q_ref[:, hc].astype(F32)
        q_s[h] = (tq * cosf + pltpu.roll(tq, HEAD_DIM // 2, 1) * sinf) * scale
        tk = k_ref[:, hc].astype(F32)
        k_s[h] = tk * cosf + pltpu.roll(tk, HEAD_DIM // 2, 1) * sinf
        v_s[h] = v_ref[:, hc].astype(F32)

    def rows(start, size):
        return pl.ds(start, size) if dil == 1 else pl.ds(start, size, stride=dil)

    def head_mask(nk, offset):
        i = lax.broadcasted_iota(jnp.int32, (nh * blk, nk), 0) % blk
        j = lax.broadcasted_iota(jnp.int32, (nh * blk, nk), 1)
        dist = offset + i - j
        return (dist >= 0) & (dist <= wd)

    mask_first = head_mask(blk, 0)
    mask_later = head_mask(2 * blk, blk)
    lane = lax.broadcasted_iota(jnp.int32, (blk, LANES), 1)
    ones = jnp.ones((2 * blk, HEAD_DIM), BF16)

    def attend(qs, ks, nk, mask):
        s = jnp.concatenate(
            [lax.dot_general(q_s[h, rows(qs, blk), :].astype(BF16), k_s[h, rows(ks, nk), :].astype(BF16),
                             (((1,), (1,)), ((), ())), preferred_element_type=F32) for h in range(nh)], axis=0)
        s = jnp.where(mask, s, -jnp.inf)
        m = jnp.max(s, axis=-1, keepdims=True)
        p = jnp.exp(s - m).astype(BF16)
        lse_tile = jnp.zeros((blk, LANES), F32)
        for h in range(nh):
            v_aug = jnp.concatenate([v_s[h, rows(ks, nk), :].astype(BF16), ones[:nk]], axis=1)
            ov = jnp.dot(p[h * blk:(h + 1) * blk], v_aug, preferred_element_type=F32)
            den = ov[:, HEAD_DIM:]
            o_s[h, rows(qs, blk), :] = ov[:, :HEAD_DIM] / den
            lse_tile = jnp.where(lane == h, m[h * blk:(h + 1) * blk] + jnp.log(den), lse_tile)
        l_s[rows(qs, blk), :] = lse_tile

    def per_phase(ph, carry):
        attend(ph, ph, blk, mask_first)

        def per_blk(n, c):
            qs = ph + dil * blk * n
            ks = ph + dil * blk * (n - 1)
            if dil == 1:
                qs = pl.multiple_of(qs, blk)
                ks = pl.multiple_of(ks, blk)
            attend(qs, ks, 2 * blk, mask_later)
            return c

        if nb > 1:
            lax.fori_loop(1, nb, per_blk, 0, unroll=3)
        return carry

    lax.fori_loop(0, dil, per_phase, 0, unroll=4 if nb == 1 else 1)
    for h in range(nh):
        o_ref[:, h * HEAD_DIM:(h + 1) * HEAD_DIM] = o_s[h].astype(o_ref.dtype)
    lse_ref[...] = l_s[...]


def _attention(proj, cosf, sinf, group):
    window, dil = DILATED_GROUPS[group]
    t = proj.shape[0]
    nb = (SEQ // dil) // ATT_BLOCK
    col = lambda base: base // A_OUT + group
    kern = functools.partial(_attn_kernel, dil=dil, nb=nb, wd=window // dil)
    return pl.pallas_call(
        kern,
        out_shape=(jax.ShapeDtypeStruct((t, A_OUT), BF16), jax.ShapeDtypeStruct((t, LANES), F32)),
        grid=(t // SEQ,),
        in_specs=[
            pl.BlockSpec((SEQ, A_OUT), lambda b: (b, col(COL_QA))),
            pl.BlockSpec((SEQ, A_OUT), lambda b: (b, col(COL_KA))),
            pl.BlockSpec((SEQ, A_OUT), lambda b: (b, col(COL_VA))),
            pl.BlockSpec((SEQ, HEAD_DIM), lambda b: (0, 0)),
            pl.BlockSpec((SEQ, HEAD_DIM), lambda b: (0, 0)),
        ],
        out_specs=(pl.BlockSpec((SEQ, A_OUT), lambda b: (b, 0)), pl.BlockSpec((SEQ, LANES), lambda b: (b, 0))),
        scratch_shapes=[pltpu.VMEM((A_HEADS_PER_GROUP, SEQ, HEAD_DIM), F32) for _ in range(4)]
        + [pltpu.VMEM((SEQ, LANES), F32)],
        compiler_params=_params(48),
        name=f"dilated_attn_g{group}",
    )(proj, proj, proj, cosf, sinf)


def _gla_kernel(q_ref, k_ref, v0_ref, v1_ref, r0_ref, r1_ref, al_ref, wa_ref, ba_ref, gain_ref, y_ref, st_ref, *, rows):
    @pl.when(pl.program_id(1) == 0)
    def _():
        st_ref[...] = jnp.zeros_like(st_ref)

    ck = GLA_CHUNK
    half = GLA_HEADS // 2
    z = jnp.dot(al_ref[...].astype(BF16), wa_ref[...], preferred_element_type=F32) + ba_ref[...]
    log_a = (jnp.minimum(z, 0.0) - jnp.log(1.0 + jnp.exp(-jnp.abs(z)))) * (1.0 / GLA_TAU)
    ri = lax.broadcasted_iota(jnp.int32, (rows, rows), 0)
    ci = lax.broadcasted_iota(jnp.int32, (rows, rows), 1)
    same = (ri // ck) == (ci // ck)
    tri = (same & (ri >= ci)).astype(BF16)
    tot = same.astype(BF16)
    la_hi = log_a.astype(BF16)
    la_lo = (log_a - la_hi.astype(F32)).astype(BF16)
    b = jnp.dot(tri, la_hi, preferred_element_type=F32) + jnp.dot(tri, la_lo, preferred_element_type=F32)
    b_end = jnp.dot(tot, la_hi, preferred_element_type=F32) + jnp.dot(tot, la_lo, preferred_element_type=F32)
    q = q_ref[...].astype(F32) * (GLA_DK ** -0.5)
    k = k_ref[...].astype(F32)
    q_dec = (q * jnp.exp(b)).astype(BF16)
    k_inv = (k * jnp.exp(-b)).astype(BF16)
    k_end = (k * jnp.exp(b_end - b)).astype(BF16)
    decay = jnp.exp(b_end)
    r1 = lax.broadcasted_iota(jnp.int32, (ck, ck), 0)
    c1 = lax.broadcasted_iota(jnp.int32, (ck, ck), 1)
    causal = r1 >= c1
    nt = (((1,), (1,)), ((), ()))
    v_refs = (v0_ref, v1_ref)
    r_refs = (r0_ref, r1_ref)
    for c in range(rows // ck):
        rs = slice(c * ck, (c + 1) * ck)
        for h in range(GLA_HEADS):
            hk = slice(h * GLA_DK, (h + 1) * GLA_DK)
            hv = slice(h * GLA_DV, (h + 1) * GLA_DV)
            hb = slice((h % half) * GLA_DV, (h % half + 1) * GLA_DV)
            v = v_refs[h // half][rs, hb]
            attn = lax.dot_general(q_dec[rs, hk], k_inv[rs, hk], nt, preferred_element_type=F32)
            attn = jnp.where(causal, attn, 0.0).astype(BF16)
            st = st_ref[h]
            o = jnp.dot(attn, v, preferred_element_type=F32)
            o = o + lax.dot_general(q_dec[rs, hk], st.astype(BF16), nt, preferred_element_type=F32)
            v_t = v.astype(F32).T.astype(BF16)
            st_ref[h] = st * decay[c * ck:c * ck + 1, hk] + jnp.dot(v_t, k_end[rs, hk], preferred_element_type=F32)
            o = o * lax.rsqrt(jnp.mean(o * o, axis=-1, keepdims=True) + NORM_EPS) * gain_ref[:, hv]
            r = r_refs[h // half][rs, hb].astype(F32)
            y_ref[rs, hv] = (o * (r * _sigmoid(r))).astype(y_ref.dtype)


def _gla(proj, a_low, w_alpha2p, b_alpha2, gla_gain):
    t = proj.shape[0]
    rows = 256
    per_b = SEQ // rows
    kern = functools.partial(_gla_kernel, rows=rows)
    col = lambda base, j=0: (lambda b, s: (b * per_b + s, base // GLA_KEY + j))
    return pl.pallas_call(
        kern,
        out_shape=jax.ShapeDtypeStruct((t, GLA_VAL), BF16),
        grid=(t // SEQ, per_b),
        in_specs=[
            pl.BlockSpec((rows, GLA_KEY), col(COL_QB)),
            pl.BlockSpec((rows, GLA_KEY), col(COL_KB)),
            pl.BlockSpec((rows, GLA_KEY), col(COL_VB, 0)),
            pl.BlockSpec((rows, GLA_KEY), col(COL_VB, 1)),
            pl.BlockSpec((rows, GLA_KEY), col(COL_RB, 0)),
            pl.BlockSpec((rows, GLA_KEY), col(COL_RB, 1)),
            pl.BlockSpec((rows, LANES), lambda b, s: (b * per_b + s, 0)),
            pl.BlockSpec((LANES, GLA_KEY), lambda b, s: (0, 0)),
            pl.BlockSpec((1, GLA_KEY), lambda b, s: (0, 0)),
            pl.BlockSpec((1, GLA_VAL), lambda b, s: (0, 0)),
        ],
        out_specs=pl.BlockSpec((rows, GLA_VAL), lambda b, s: (b * per_b + s, 0)),
        scratch_shapes=[pltpu.VMEM((GLA_HEADS, GLA_DV, GLA_DK), F32)],
        compiler_params=_params(32),
        name="gla",
    )(proj, proj, proj, proj, proj, proj, a_low, w_alpha2p, b_alpha2, gla_gain)


def _post_mixer_kernel(x_ref, ga_ref, gb_ref, o0_ref, o1_ref, o2_ref, l0_ref, l1_ref, l2_ref, yb_ref, ada_ref,
                       woa_ref, wob_ref, wout_ref, g2_ref, wr_ref, br_ref,
                       xo_ref, h2_ref, e_ref, w_ref):
    ls = [l0_ref[...], l1_ref[...], l2_ref[...]]
    os_ = [o0_ref, o1_ref, o2_ref]
    mx = jnp.maximum(jnp.maximum(ls[0], ls[1]), ls[2])
    es = [jnp.exp(l - mx) for l in ls]
    zs = es[0] + es[1] + es[2]
    wt = [e / zs for e in es]
    parts = []
    for h in range(A_HEADS_PER_GROUP):
        hc = slice(h * HEAD_DIM, (h + 1) * HEAD_DIM)
        acc = wt[0][:, h:h + 1] * os_[0][:, hc].astype(F32)
        acc = acc + wt[1][:, h:h + 1] * os_[1][:, hc].astype(F32)
        acc = acc + wt[2][:, h:h + 1] * os_[2][:, hc].astype(F32)
        parts.append(acc)
    ya = jnp.concatenate(parts, axis=1).astype(BF16)
    ta = jnp.dot(ya, woa_ref[0], preferred_element_type=F32)
    tb = jnp.dot(yb_ref[...], wob_ref[0], preferred_element_type=F32)
    merged = _sigmoid(ga_ref[...].astype(F32)) * ta + _sigmoid(gb_ref[...].astype(F32)) * tb
    mix = jnp.dot(merged.astype(BF16), wout_ref[0], preferred_element_type=F32)
    xn = x_ref[...] + ada_ref[0, 2:3, :] * mix
    xo_ref[...] = xn
    y = xn * lax.rsqrt(jnp.mean(xn * xn, axis=-1, keepdims=True) + NORM_EPS) * g2_ref[...]
    h2 = y * (1.0 + ada_ref[0, 4:5, :]) + ada_ref[0, 3:4, :]
    h2_ref[...] = h2

    logits = lax.dot_general(wr_ref[...], h2.astype(BF16), (((1,), (1,)), ((), ())), preferred_element_type=F32)
    tm = logits.shape[1]
    aff_all = _sigmoid(logits)
    sel_all = aff_all + br_ref[...]
    sel = [sel_all[e:e + 1, :] for e in range(N_EXPERTS)]
    aff = [aff_all[e:e + 1, :] for e in range(N_EXPERTS)]
    neg = -jnp.inf

    def top2_sum(v0, v1, v2, v3):
        hi01, lo01 = jnp.maximum(v0, v1), jnp.minimum(v0, v1)
        hi23, lo23 = jnp.maximum(v2, v3), jnp.minimum(v2, v3)
        return jnp.maximum(hi01, hi23) + jnp.maximum(jnp.minimum(hi01, hi23), jnp.maximum(lo01, lo23))

    epg = EXPERTS_PER_GROUP
    scores = [top2_sum(*sel[g * epg:(g + 1) * epg]) for g in range(N_GROUPS)]
    best = jnp.zeros_like(scores[0])
    best_s = scores[0]
    for g in range(1, N_GROUPS):
        better = scores[g] > best_s
        best = jnp.where(better, float(g), best)
        best_s = jnp.where(better, scores[g], best_s)
    vals, affs = [], []
    for j in range(epg):
        vj, aj = sel[j], aff[j]
        for g in range(1, N_GROUPS):
            in_g = best == float(g)
            vj = jnp.where(in_g, sel[g * epg + j], vj)
            aj = jnp.where(in_g, aff[g * epg + j], aj)
        vals.append(vj)
        affs.append(aj)

    def first_argmax(vs):
        idx, m, a = jnp.zeros_like(vs[0]), vs[0], affs[0]
        for j in range(1, epg):
            gt = vs[j] > m
            idx = jnp.where(gt, float(j), idx)
            m = jnp.where(gt, vs[j], m)
            a = jnp.where(gt, affs[j], a)
        return idx, a

    i1, a1 = first_argmax(vals)
    i2, a2 = first_argmax([jnp.where(i1 == float(j), neg, vals[j]) for j in range(epg)])
    tot = a1 + a2
    srow = lax.broadcasted_iota(jnp.int32, (8, tm), 0)
    packed = jnp.where(srow == 0, best * epg + i1,
                       jnp.where(srow == 1, best * epg + i2,
                                 jnp.where(srow == 2, a1 / tot, jnp.where(srow == 3, a2 / tot, 0.0))))
    packed_t = jnp.concatenate([packed, jnp.zeros((LANES - 8, tm), F32)], axis=0).T
    e_ref[...] = packed_t[:, 0:TOP_K].astype(jnp.int32)
    w_ref[...] = packed_t[:, TOP_K:2 * TOP_K]


def _post_mixer(x2, gates, layer, o_list, l_list, y_b, ada_l, woa, wob, wout, gain2, w_router_t, b_router_c):
    t, d = x2.shape
    tm = 256
    per_b = SEQ // tm
    const = lambda shape: pl.BlockSpec(shape, lambda i: (0,) * len(shape), pipeline_mode=pl.Buffered(1))
    wblk = lambda k: pl.BlockSpec((1, k, d), lambda i: (layer, 0, 0), pipeline_mode=pl.Buffered(1))
    rowblk = lambda w: pl.BlockSpec((tm, w), lambda i: (i, 0))
    return pl.pallas_call(
        _post_mixer_kernel,
        out_shape=(
            jax.ShapeDtypeStruct((t, d), F32),
            jax.ShapeDtypeStruct((t, d), F32),
            jax.ShapeDtypeStruct((t, TOP_K), jnp.int32),
            jax.ShapeDtypeStruct((t, TOP_K), F32),
        ),
        grid=(t // tm,),
        in_specs=[
            rowblk(d),
            pl.BlockSpec((tm, d), lambda i: (i, 0)),
            pl.BlockSpec((tm, d), lambda i: (i, 1)),
            rowblk(A_OUT), rowblk(A_OUT), rowblk(A_OUT),
            rowblk(LANES), rowblk(LANES), rowblk(LANES),
            rowblk(GLA_VAL),
            pl.BlockSpec((1, 6, d), lambda i: (i // per_b, 0, 0)),
            wblk(A_OUT), wblk(GLA_VAL), wblk(d),
            const((1, d)), const((LANES, d)), const((LANES, 1)),
        ],
        out_specs=(rowblk(d), rowblk(d), rowblk(TOP_K), rowblk(TOP_K)),
        compiler_params=_params(56),
        name="post_mixer",
    )(x2, gates, gates, *o_list, *l_list, y_b, ada_l, woa, wob, wout, gain2, w_router_t, b_router_c)


def _row_copy(src_hbm, row, dst, slot, r, sem):
    return pltpu.make_async_copy(src_hbm.at[pl.ds(row, 1), :], dst.at[slot, pl.ds(r, 1), :], sem.at[slot])


def _moe_kernel(be_ref, nu_ref, dest_ref, h_hbm, zero_hbm, wg_ref, wu_ref, wd_ref, y_ref, xg, src_s, sem, zsem):
    i = pl.program_id(0)
    n_used = nu_ref[0]
    blk = MOE_BLOCK
    slot = i % 2

    @pl.when(i == 0)
    def _():
        fill = pltpu.make_async_copy(zero_hbm, src_s, zsem)
        fill.start()
        fill.wait()

        def scat(j, c):
            src_s[dest_ref[j]] = lax.shift_right_logical(j, 1)
            return c
        lax.fori_loop(0, dest_ref.shape[0], scat, 0, unroll=8)

    def gather(block, s, start):
        def body(r, c):
            cp = _row_copy(h_hbm, src_s[block * blk + r], xg, s, r, sem)
            if start:
                cp.start()
            else:
                cp.wait()
            return c
        lax.fori_loop(0, blk, body, 0, unroll=8)

    @pl.when(i == 0)
    def _():
        gather(0, 0, True)

    n_blk = pl.num_programs(0)

    @pl.when(i < n_used)
    def _():
        gather(i, slot, False)
        nxt = jnp.minimum(i + 1, n_blk - 1) * blk
        def issue(quarter):
            for r in range(quarter * (blk // 4), (quarter + 1) * (blk // 4)):
                _row_copy(h_hbm, src_s[nxt + r], xg, 1 - slot, r, sem).start(priority=r % 2)

        xb = xg[slot].astype(BF16)
        issue(0)
        g = jnp.dot(xb, wg_ref[0, 0], preferred_element_type=F32)
        issue(1)
        u = jnp.dot(xb, wu_ref[0, 0], preferred_element_type=F32)
        issue(2)
        a = (g * _sigmoid(g) * u).astype(BF16)
        y_ref[...] = jnp.dot(a, wd_ref[0, 0], preferred_element_type=F32)
        issue(3)

        @pl.when(i == n_blk - 1)
        def _():
            gather(i, 1 - slot, False)

    @pl.when((i == n_used) & (i > 0))
    def _():
        gather(i, slot, False)

    @pl.when(i >= n_used)
    def _():
        y_ref[...] = jnp.zeros_like(y_ref)


def _moe_experts(blk_e, n_used, dest, h2, layer, wg, wu, wd):
    t, d = h2.shape
    n_blk = blk_e.shape[0]
    f = wg.shape[3]
    grid_spec = pltpu.PrefetchScalarGridSpec(
        num_scalar_prefetch=3,
        grid=(n_blk,),
        in_specs=[
            pl.BlockSpec(memory_space=pl.ANY),
            pl.BlockSpec(memory_space=pl.ANY),
            pl.BlockSpec((1, 1, d, f), lambda i, be, nu, sr: (layer, be[i], 0, 0)),
            pl.BlockSpec((1, 1, d, f), lambda i, be, nu, sr: (layer, be[i], 0, 0)),
            pl.BlockSpec((1, 1, f, d), lambda i, be, nu, sr: (layer, be[i], 0, 0)),
        ],
        out_specs=pl.BlockSpec((MOE_BLOCK, d), lambda i, be, nu, sr: (i, 0)),
        scratch_shapes=[pltpu.VMEM((2, MOE_BLOCK, d), F32), pltpu.SMEM((n_blk * MOE_BLOCK,), jnp.int32),
                        pltpu.SemaphoreType.DMA((2,)), pltpu.SemaphoreType.DMA(())],
    )
    zero_rows = jnp.zeros((n_blk * MOE_BLOCK,), jnp.int32)
    return pl.pallas_call(
        _moe_kernel,
        out_shape=jax.ShapeDtypeStruct((n_blk * MOE_BLOCK, d), F32),
        grid_spec=grid_spec,
        compiler_params=_params(48, ("arbitrary",)),
        name="moe_experts",
    )(blk_e, n_used, dest, h2, zero_rows, wg, wu, wd)


def _pair_copy(src_hbm, row, dst, slot, k, r, sem):
    return pltpu.make_async_copy(src_hbm.at[pl.ds(row, 1), :], dst.at[slot, k, pl.ds(r, 1), :], sem.at[slot])


def _combine_kernel(pos_ref, y_hbm, x_ref, w_ref, ada_ref, fg_ref, o_ref, yg, sem, *, tm, final):
    i = pl.program_id(0)
    n = pl.num_programs(0)
    slot = i % 2

    def gather(tile, s, start):
        def body(r, c):
            for k in range(TOP_K):
                cp = _pair_copy(y_hbm, pos_ref[(tile * tm + r) * TOP_K + k], yg, s, k, r, sem)
                if start:
                    cp.start(priority=k % 2)
                else:
                    cp.wait()
            return c
        lax.fori_loop(0, tm, body, 0, unroll=8)

    @pl.when(i == 0)
    def _():
        gather(0, 0, True)

    @pl.when(i + 1 < n)
    def _():
        gather(i + 1, 1 - slot, True)

    gather(i, slot, False)
    w = w_ref[...]
    y = w[:, 0:1] * yg[slot, 0] + w[:, 1:2] * yg[slot, 1]
    out = x_ref[...] + ada_ref[0, 5:6, :] * y
    if final:
        out = out * lax.rsqrt(jnp.mean(out * out, axis=-1, keepdims=True) + NORM_EPS) * fg_ref[...]
    o_ref[...] = out


def _combine(pos, ybuf, x2, gw, ada_l, final_gain, final):
    t, d = x2.shape
    tm = 256
    per_b = SEQ // tm
    kern = functools.partial(_combine_kernel, tm=tm, final=final)
    grid_spec = pltpu.PrefetchScalarGridSpec(
        num_scalar_prefetch=1,
        grid=(t // tm,),
        in_specs=[
            pl.BlockSpec(memory_space=pl.ANY),
            pl.BlockSpec((tm, d), lambda i, p: (i, 0)),
            pl.BlockSpec((tm, TOP_K), lambda i, p: (i, 0)),
            pl.BlockSpec((1, 6, d), lambda i, p: (i // per_b, 0, 0)),
            pl.BlockSpec((1, d), lambda i, p: (0, 0)),
        ],
        out_specs=pl.BlockSpec((tm, d), lambda i, p: (i, 0)),
        scratch_shapes=[pltpu.VMEM((2, TOP_K, tm, d), F32), pltpu.SemaphoreType.DMA((2,))],
    )
    return pl.pallas_call(
        kern,
        out_shape=jax.ShapeDtypeStruct((t, d), F32),
        grid_spec=grid_spec,
        compiler_params=_params(40, ("arbitrary",)),
        name="moe_combine_final" if final else "moe_combine",
    )(pos, ybuf, x2, gw, ada_l, final_gain)


def _plan_kernel(e_ref, dest_ref, cnt_ref, carry, pstart_s, *, tm):
    ph = pl.program_id(0)
    i = pl.program_id(1)
    lane = lax.broadcasted_iota(jnp.int32, (tm, LANES), 1)
    e = e_ref[...]
    oh0 = e[:, 0:1] == lane
    oh1 = e[:, 1:2] == lane
    oh = oh0.astype(F32) + oh1.astype(F32)
    tile_counts = jnp.sum(oh, axis=0, keepdims=True)

    @pl.when((ph == 0) & (i == 0))
    def _():
        carry[...] = jnp.zeros_like(carry)

    @pl.when(ph == 0)
    def _():
        carry[...] += tile_counts

    @pl.when((ph == 1) & (i == 0))
    def _():
        counts = carry[...]
        cnt_ref[...] = counts.astype(jnp.int32)
        nblk = jnp.floor((counts + (MOE_BLOCK - 1)) * (1.0 / MOE_BLOCK))
        r = lax.broadcasted_iota(jnp.int32, (LANES, LANES), 0)
        c = lax.broadcasted_iota(jnp.int32, (LANES, LANES), 1)
        upper = (r < c).astype(BF16)
        nb8 = jnp.broadcast_to(nblk, (8, LANES)).astype(BF16)
        pstart_s[...] = jnp.dot(nb8, upper, preferred_element_type=F32)[0:1, :] * float(MOE_BLOCK)
        carry[...] = jnp.zeros_like(carry)

    @pl.when(ph == 1)
    def _():
        r = lax.broadcasted_iota(jnp.int32, (tm, tm), 0)
        c = lax.broadcasted_iota(jnp.int32, (tm, tm), 1)
        lower = (r > c).astype(BF16)
        before = jnp.dot(lower, oh.astype(BF16), preferred_element_type=F32) + carry[...]
        base = before + pstart_s[...]
        d0 = jnp.sum(jnp.where(oh0, base, 0.0), axis=1, keepdims=True)
        d1 = jnp.sum(jnp.where(oh1, base, 0.0), axis=1, keepdims=True)
        col = lax.broadcasted_iota(jnp.int32, (tm, TOP_K), 1)
        dest_ref[...] = jnp.where(col == 0, d0, d1).astype(jnp.int32)
        carry[...] += tile_counts


def _dispatch_plan(eidx, n_blk):
    t = eidx.shape[0]
    tm = 512
    dest, counts = pl.pallas_call(
        functools.partial(_plan_kernel, tm=tm),
        out_shape=(jax.ShapeDtypeStruct((t, TOP_K), jnp.int32), jax.ShapeDtypeStruct((1, LANES), jnp.int32)),
        grid=(2, t // tm),
        in_specs=[pl.BlockSpec((tm, TOP_K), lambda p, i: (i, 0))],
        out_specs=(pl.BlockSpec((tm, TOP_K), lambda p, i: (i * p, 0)), pl.BlockSpec((1, LANES), lambda p, i: (0, 0))),
        scratch_shapes=[pltpu.VMEM((1, LANES), F32), pltpu.VMEM((1, LANES), F32)],
        compiler_params=_params(32, ("arbitrary", "arbitrary")),
        name="dispatch_plan",
    )(eidx)
    counts = counts[0, :N_EXPERTS]
    pcounts = (counts + MOE_BLOCK - 1) // MOE_BLOCK * MOE_BLOCK
    pend = jnp.cumsum(pcounts)
    blk_start = jnp.arange(n_blk, dtype=jnp.int32) * MOE_BLOCK
    blk_e = jnp.sum((pend[None, :] <= blk_start[:, None]).astype(jnp.int32), axis=1)
    blk_e = jnp.minimum(blk_e, N_EXPERTS - 1).astype(jnp.int32)
    n_used = (pend[-1] // MOE_BLOCK).astype(jnp.int32).reshape(1)
    return blk_e, n_used, dest.reshape(-1)


def _rope_tables():
    pos = jnp.arange(SEQ, dtype=F32)
    inv_freq = ROPE_THETA ** (-jnp.arange(0, HEAD_DIM, 2, dtype=F32) / HEAD_DIM)
    ang = pos[:, None] * inv_freq[None, :]
    cos, sin = jnp.cos(ang), jnp.sin(ang)
    return jnp.concatenate([cos, cos], axis=1), jnp.concatenate([-sin, sin], axis=1)


def kernel(x, c, w_ada, b_ada, norm_mix, norm_ffn, w_in, w_alpha2, b_alpha2, gla_gain, w_out_a, w_out_b, w_out,
           w_router, b_router, w_gate_e, w_up_e, w_down_e, final_norm):
    b, s, d = x.shape
    depth = w_ada.shape[0]
    t = b * s
    n_blk = (t * TOP_K) // MOE_BLOCK + N_EXPERTS
    cosf, sinf = _rope_tables()
    ada = _ada(c, w_ada, b_ada).reshape(depth, b, 6, d)
    w_router_t = jnp.pad(w_router.T, ((0, LANES - N_EXPERTS), (0, 0))).astype(BF16)
    b_router_c = jnp.pad(b_router, (0, LANES - N_EXPERTS)).reshape(LANES, 1)
    final_gain = final_norm.reshape(1, d)
    w_alow_all = jnp.pad(w_in[:, :, COL_ALOW:COL_GATES], ((0, 0), (0, 0), (0, LANES - GLA_LOWRANK))).astype(BF16)
    w_gates_all = w_in[:, :, COL_GATES:].astype(BF16)
    w_oa, w_ob, w_o = w_out_a.astype(BF16), w_out_b.astype(BF16), w_out.astype(BF16)
    w_g, w_u, w_d = w_gate_e.astype(BF16), w_up_e.astype(BF16), w_down_e.astype(BF16)
    x2 = x.reshape(t, d)
    for l in range(depth):
        w_alpha2p = jnp.pad(w_alpha2[l], ((0, LANES - GLA_LOWRANK), (0, 0))).astype(BF16)
        h, a_low = _norm_mod(x2, ada[l], norm_mix[l].reshape(1, d), w_alow_all[l])
        proj = _in_proj(h, w_in, l, COL_ALOW, "in_proj")
        gates = _in_proj(h, w_gates_all, l, 2 * D_MODEL, "in_proj_gates")
        o_list, l_list = [], []
        for g in range(len(DILATED_GROUPS)):
            o_g, l_g = _attention(proj, cosf, sinf, g)
            o_list.append(o_g)
            l_list.append(l_g)
        y_b = _gla(proj, a_low, w_alpha2p, b_alpha2[l].reshape(1, GLA_KEY), gla_gain[l].reshape(1, GLA_VAL))
        x2, h2, eidx, gw = _post_mixer(
            x2, gates, l, o_list, l_list, y_b, ada[l], w_oa, w_ob, w_o,
            norm_ffn[l].reshape(1, d), w_router_t, b_router_c)
        blk_e, n_used, dest = _dispatch_plan(eidx, n_blk)
        ybuf = _moe_experts(blk_e, n_used, dest, h2, l, w_g, w_u, w_d)
        x2 = _combine(dest, ybuf, x2, gw, ada[l], final_gain, final=(l == depth - 1))
    return x2.reshape(b, s, d)
```

```python
import functools

import jax
import jax.numpy as jnp
from jax import lax
from jax.experimental import pallas as pl
from jax.experimental.pallas import tpu as pltpu

F32 = jnp.float32
BF16 = jnp.bfloat16

D_MODEL = 2048
SEQ = 2048
HEAD_DIM = 128
A_HEADS_PER_GROUP = 4
DILATED_GROUPS = ((128, 1), (512, 4), (2048, 16))
A_WIDTH = 12 * HEAD_DIM
A_OUT = A_HEADS_PER_GROUP * HEAD_DIM
ATT_BLOCK = 128
ROPE_THETA = 10000.0
GLA_HEADS = 4
GLA_DK = 128
GLA_DV = 256
GLA_KEY = GLA_HEADS * GLA_DK
GLA_VAL = GLA_HEADS * GLA_DV
GLA_LOWRANK = 16
GLA_TAU = 16.0
GLA_CHUNK = 64
N_EXPERTS = 16
N_GROUPS = 4
EXPERTS_PER_GROUP = N_EXPERTS // N_GROUPS
TOP_K = 2
D_FF_EXPERT = 1024
NORM_EPS = 1e-6

LANES = 128
MOE_BLOCK = 256
PROJ_TILE = 512

COL_QA = 0
COL_KA = A_WIDTH
COL_VA = 2 * A_WIDTH
COL_QB = 3 * A_WIDTH
COL_KB = COL_QB + GLA_KEY
COL_VB = COL_KB + GLA_KEY
COL_RB = COL_VB + GLA_VAL
COL_ALOW = COL_RB + GLA_VAL
COL_GATES = COL_ALOW + GLA_LOWRANK


def _params(vmem_mib, sem=None):
    kw = dict(vmem_limit_bytes=int(vmem_mib) << 20)
    if sem is not None:
        kw["dimension_semantics"] = sem
    return pltpu.CompilerParams(**kw)


def _sigmoid(x):
    return 0.5 * jnp.tanh(0.5 * x) + 0.5


def _ada_kernel(c_ref, w_ref, b_ref, o_ref):
    @pl.when(pl.program_id(1) == 0)
    def _():
        o_ref[0] = jnp.broadcast_to(b_ref[0], o_ref.shape[1:])

    c = c_ref[...]
    ca = (c * _sigmoid(c)).astype(BF16)
    o_ref[0] += jnp.dot(ca, w_ref[0].astype(BF16), preferred_element_type=F32)


def _ada(c, w_ada, b_ada):
    depth, d, n = w_ada.shape
    b = c.shape[0]
    tk = 256
    return pl.pallas_call(
        _ada_kernel,
        out_shape=jax.ShapeDtypeStruct((depth, b, n), F32),
        grid=(depth, d // tk),
        in_specs=[
            pl.BlockSpec((b, tk), lambda l, k: (0, k)),
            pl.BlockSpec((1, tk, n), lambda l, k: (l, k, 0)),
            pl.BlockSpec((1, 1, n), lambda l, k: (l, 0, 0)),
        ],
        out_specs=pl.BlockSpec((1, b, n), lambda l, k: (l, 0, 0)),
        compiler_params=_params(40, ("arbitrary", "arbitrary")),
        name="ada",
    )(c, w_ada, b_ada.reshape(depth, 1, n))


def _norm_mod_kernel(x_ref, ada_ref, gain_ref, wal_ref, h_ref, al_ref):
    x = x_ref[...]
    y = x * lax.rsqrt(jnp.mean(x * x, axis=-1, keepdims=True) + NORM_EPS) * gain_ref[...]
    h = (y * (1.0 + ada_ref[0, 1:2, :]) + ada_ref[0, 0:1, :]).astype(BF16)
    h_ref[...] = h
    al_ref[...] = lax.dot_general(h, wal_ref[...], (((1,), (1,)), ((), ())), preferred_element_type=F32)


def _norm_mod(x2, ada_l, gain, w_alow):
    t, d = x2.shape
    tm = 512
    per_b = SEQ // tm
    return pl.pallas_call(
        _norm_mod_kernel,
        out_shape=(jax.ShapeDtypeStruct((t, d), BF16), jax.ShapeDtypeStruct((t, LANES), F32)),
        grid=(t // tm,),
        in_specs=[
            pl.BlockSpec((tm, d), lambda i: (i, 0)),
            pl.BlockSpec((1, 6, d), lambda i: (i // per_b, 0, 0)),
            pl.BlockSpec((1, d), lambda i: (0, 0)),
            pl.BlockSpec((LANES, d), lambda i: (0, 0)),
        ],
        out_specs=(pl.BlockSpec((tm, d), lambda i: (i, 0)), pl.BlockSpec((tm, LANES), lambda i: (i, 0))),
        compiler_params=_params(40),
        name="norm_mod",
    )(x2, ada_l, gain, w_alow)


def _matmul_nt_kernel(a_ref, b_ref, o_ref):
    o_ref[...] = lax.dot_general(a_ref[...], b_ref[...].astype(BF16), (((1,), (1,)), ((), ())),
                                 preferred_element_type=F32).astype(o_ref.dtype)


def _in_proj(h, w_t, layer, col0, n_cols, name):
    t, d = h.shape
    tm, tn = 2048, PROJ_TILE
    sub = 8
    assert col0 % sub == 0 and tn % sub == 0
    w_spec = pl.BlockSpec((pl.Squeezed(), pl.Element(tn), pl.Element(d)),
                          lambda i, j: (layer, (col0 // sub + j * (tn // sub)) * sub, 0))
    return pl.pallas_call(
        _matmul_nt_kernel,
        out_shape=jax.ShapeDtypeStruct((t, n_cols), BF16),
        grid=(t // tm, n_cols // tn),
        in_specs=[pl.BlockSpec((tm, d), lambda i, j: (i, 0)), w_spec],
        out_specs=pl.BlockSpec((tm, tn), lambda i, j: (i, j)),
        compiler_params=_params(48),
        name=name,
    )(h, w_t)


def _attn_kernel(q_ref, k_ref, v_ref, cos_ref, sin_ref, o_ref, lse_ref, q_s, k_s, v_s, o_s, l_s, *, dil, nb, wd):
    scale = HEAD_DIM ** -0.5
    blk = ATT_BLOCK
    nh = A_HEADS_PER_GROUP
    cosf = cos_ref[...]
    sinf = sin_ref[...]
    for h in range(nh):
        hc = slice(h * HEAD_DIM, (h + 1) * HEAD_DIM)
        tq = q_ref[:, hc].astype(F32)
        q_s[h] = (tq * cosf + pltpu.roll(tq, HEAD_DIM // 2, 1) * sinf) * scale
        tk = k_ref[:, hc].astype(F32)
        k_s[h] = tk * cosf + pltpu.roll(tk, HEAD_DIM // 2, 1) * sinf
        v_s[h] = v_ref[:, hc].astype(F32)

    def rows(start, size):
        return pl.ds(start, size) if dil == 1 else pl.ds(start, size, stride=dil)

    def head_mask(nk, offset):
        i = lax.broadcasted_iota(jnp.int32, (nh * blk, nk), 0) % blk
        j = lax.broadcasted_iota(jnp.int32, (nh * blk, nk), 1)
        dist = offset + i - j
        return (dist >= 0) & (dist <= wd)

    mask_first = head_mask(blk, 0)
    mask_later = head_mask(2 * blk, blk)
    lane = lax.broadcasted_iota(jnp.int32, (blk, LANES), 1)
    ones = jnp.ones((2 * blk, HEAD_DIM), BF16)

    def attend(qs, ks, nk, mask):
        s = jnp.concatenate(
            [lax.dot_general(q_s[h, rows(qs, blk), :].astype(BF16), k_s[h, rows(ks, nk), :].astype(BF16),
                             (((1,), (1,)), ((), ())), preferred_element_type=F32) for h in range(nh)], axis=0)
        s = jnp.where(mask, s, -jnp.inf)
        m = jnp.max(s, axis=-1, keepdims=True)
        p = jnp.exp(s - m).astype(BF16)
        lse_tile = jnp.zeros((blk, LANES), F32)
        for h in range(nh):
            v_aug = jnp.concatenate([v_s[h, rows(ks, nk), :].astype(BF16), ones[:nk]], axis=1)
            ov = jnp.dot(p[h * blk:(h + 1) * blk], v_aug, preferred_element_type=F32)
            den = ov[:, HEAD_DIM:]
            o_s[h, rows(qs, blk), :] = ov[:, :HEAD_DIM] / den
            lse_tile = jnp.where(lane == h, m[h * blk:(h + 1) * blk] + jnp.log(den), lse_tile)
        l_s[rows(qs, blk), :] = lse_tile

    def per_phase(ph, carry):
        attend(ph, ph, blk, mask_first)

        def per_blk(n, c):
            qs = ph + dil * blk * n
            ks = ph + dil * blk * (n - 1)
            if dil == 1:
                qs = pl.multiple_of(qs, blk)
                ks = pl.multiple_of(ks, blk)
            attend(qs, ks, 2 * blk, mask_later)
            return c

        if nb > 1:
            lax.fori_loop(1, nb, per_blk, 0, unroll=3)
        return carry

    lax.fori_loop(0, dil, per_phase, 0, unroll=4 if nb == 1 else 1)
    for h in range(nh):
        o_ref[:, h * HEAD_DIM:(h + 1) * HEAD_DIM] = o_s[h].astype(o_ref.dtype)
    lse_ref[...] = l_s[...]


def _attention(proj, cosf, sinf, group):
    window, dil = DILATED_GROUPS[group]
    t = proj.shape[0]
    nb = (SEQ // dil) // ATT_BLOCK
    col = lambda base: base // A_OUT + group
    kern = functools.partial(_attn_kernel, dil=dil, nb=nb, wd=window // dil)
    return pl.pallas_call(
        kern,
        out_shape=(jax.ShapeDtypeStruct((t, A_OUT), BF16), jax.ShapeDtypeStruct((t, LANES), F32)),
        grid=(t // SEQ,),
        in_specs=[
            pl.BlockSpec((SEQ, A_OUT), lambda b: (b, col(COL_QA))),
            pl.BlockSpec((SEQ, A_OUT), lambda b: (b, col(COL_KA))),
            pl.BlockSpec((SEQ, A_OUT), lambda b: (b, col(COL_VA))),
            pl.BlockSpec((SEQ, HEAD_DIM), lambda b: (0, 0)),
            pl.BlockSpec((SEQ, HEAD_DIM), lambda b: (0, 0)),
        ],
        out_specs=(pl.BlockSpec((SEQ, A_OUT), lambda b: (b, 0)), pl.BlockSpec((SEQ, LANES), lambda b: (b, 0))),
        scratch_shapes=[pltpu.VMEM((A_HEADS_PER_GROUP, SEQ, HEAD_DIM), F32) for _ in range(4)]
        + [pltpu.VMEM((SEQ, LANES), F32)],
        compiler_params=_params(48),
        name=f"dilated_attn_g{group}",
    )(proj, proj, proj, cosf, sinf)


def _gla_kernel(q_ref, k_ref, v0_ref, v1_ref, r0_ref, r1_ref, al_ref, wa_ref, ba_ref, gain_ref, y_ref, st_ref, *, rows):
    @pl.when(pl.program_id(1) == 0)
    def _():
        st_ref[...] = jnp.zeros_like(st_ref)

    ck = GLA_CHUNK
    half = GLA_HEADS // 2
    z = jnp.dot(al_ref[...].astype(BF16), wa_ref[...], preferred_element_type=F32) + ba_ref[...]
    log_a = (jnp.minimum(z, 0.0) - jnp.log(1.0 + jnp.exp(-jnp.abs(z)))) * (1.0 / GLA_TAU)
    ri = lax.broadcasted_iota(jnp.int32, (rows, rows), 0)
    ci = lax.broadcasted_iota(jnp.int32, (rows, rows), 1)
    same = (ri // ck) == (ci // ck)
    tri = (same & (ri >= ci)).astype(BF16)
    tot = same.astype(BF16)
    la_hi = log_a.astype(BF16)
    la_lo = (log_a - la_hi.astype(F32)).astype(BF16)
    b = jnp.dot(tri, la_hi, preferred_element_type=F32) + jnp.dot(tri, la_lo, preferred_element_type=F32)
    b_end = jnp.dot(tot, la_hi, preferred_element_type=F32) + jnp.dot(tot, la_lo, preferred_element_type=F32)
    q = q_ref[...].astype(F32) * (GLA_DK ** -0.5)
    k = k_ref[...].astype(F32)
    q_dec = (q * jnp.exp(b)).astype(BF16)
    k_inv = (k * jnp.exp(-b)).astype(BF16)
    k_end = (k * jnp.exp(b_end - b)).astype(BF16)
    decay = jnp.exp(b_end)
    r1 = lax.broadcasted_iota(jnp.int32, (ck, ck), 0)
    c1 = lax.broadcasted_iota(jnp.int32, (ck, ck), 1)
    causal = r1 >= c1
    nt = (((1,), (1,)), ((), ()))
    v_refs = (v0_ref, v1_ref)
    r_refs = (r0_ref, r1_ref)
    for c in range(rows // ck):
        rs = slice(c * ck, (c + 1) * ck)
        for h in range(GLA_HEADS):
            hk = slice(h * GLA_DK, (h + 1) * GLA_DK)
            hv = slice(h * GLA_DV, (h + 1) * GLA_DV)
            hb = slice((h % half) * GLA_DV, (h % half + 1) * GLA_DV)
            v = v_refs[h // half][rs, hb]
            attn = lax.dot_general(q_dec[rs, hk], k_inv[rs, hk], nt, preferred_element_type=F32)
            attn = jnp.where(causal, attn, 0.0).astype(BF16)
            st = st_ref[h]
            o = jnp.dot(attn, v, preferred_element_type=F32)
            o = o + lax.dot_general(q_dec[rs, hk], st.astype(BF16), nt, preferred_element_type=F32)
            v_t = v.astype(F32).T.astype(BF16)
            st_ref[h] = st * decay[c * ck:c * ck + 1, hk] + jnp.dot(v_t, k_end[rs, hk], preferred_element_type=F32)
            o = o * lax.rsqrt(jnp.mean(o * o, axis=-1, keepdims=True) + NORM_EPS) * gain_ref[:, hv]
            r = r_refs[h // half][rs, hb].astype(F32)
            y_ref[rs, hv] = (o * (r * _sigmoid(r))).astype(y_ref.dtype)


def _gla(proj, a_low, w_alpha2p, b_alpha2, gla_gain):
    t = proj.shape[0]
    rows = 256
    per_b = SEQ // rows
    kern = functools.partial(_gla_kernel, rows=rows)
    col = lambda base, j=0: (lambda b, s: (b * per_b + s, base // GLA_KEY + j))
    return pl.pallas_call(
        kern,
        out_shape=jax.ShapeDtypeStruct((t, GLA_VAL), BF16),
        grid=(t // SEQ, per_b),
        in_specs=[
            pl.BlockSpec((rows, GLA_KEY), col(COL_QB)),
            pl.BlockSpec((rows, GLA_KEY), col(COL_KB)),
            pl.BlockSpec((rows, GLA_KEY), col(COL_VB, 0)),
            pl.BlockSpec((rows, GLA_KEY), col(COL_VB, 1)),
            pl.BlockSpec((rows, GLA_KEY), col(COL_RB, 0)),
            pl.BlockSpec((rows, GLA_KEY), col(COL_RB, 1)),
            pl.BlockSpec((rows, LANES), lambda b, s: (b * per_b + s, 0)),
            pl.BlockSpec((LANES, GLA_KEY), lambda b, s: (0, 0)),
            pl.BlockSpec((1, GLA_KEY), lambda b, s: (0, 0)),
            pl.BlockSpec((1, GLA_VAL), lambda b, s: (0, 0)),
        ],
        out_specs=pl.BlockSpec((rows, GLA_VAL), lambda b, s: (b * per_b + s, 0)),
        scratch_shapes=[pltpu.VMEM((GLA_HEADS, GLA_DV, GLA_DK), F32)],
        compiler_params=_params(32),
        name="gla",
    )(proj, proj, proj, proj, proj, proj, a_low, w_alpha2p, b_alpha2, gla_gain)


def _post_mixer_kernel(x_ref, ga_ref, gb_ref, o0_ref, o1_ref, o2_ref, l0_ref, l1_ref, l2_ref, yb_ref, ada_ref,
                       woa_ref, wob_ref, wout_ref, g2_ref, wr_ref, br_ref,
                       xo_ref, h2_ref, e_ref, w_ref):
    ls = [l0_ref[...], l1_ref[...], l2_ref[...]]
    os_ = [o0_ref, o1_ref, o2_ref]
    mx = jnp.maximum(jnp.maximum(ls[0], ls[1]), ls[2])
    es = [jnp.exp(l - mx) for l in ls]
    zs = es[0] + es[1] + es[2]
    wt = [e / zs for e in es]
    parts = []
    for h in range(A_HEADS_PER_GROUP):
        hc = slice(h * HEAD_DIM, (h + 1) * HEAD_DIM)
        acc = wt[0][:, h:h + 1] * os_[0][:, hc].astype(F32)
        acc = acc + wt[1][:, h:h + 1] * os_[1][:, hc].astype(F32)
        acc = acc + wt[2][:, h:h + 1] * os_[2][:, hc].astype(F32)
        parts.append(acc)
    ya = jnp.concatenate(parts, axis=1).astype(BF16)
    ta = jnp.dot(ya, woa_ref[0], preferred_element_type=F32)
    tb = jnp.dot(yb_ref[...], wob_ref[0], preferred_element_type=F32)
    merged = _sigmoid(ga_ref[...].astype(F32)) * ta + _sigmoid(gb_ref[...].astype(F32)) * tb
    mix = jnp.dot(merged.astype(BF16), wout_ref[0], preferred_element_type=F32)
    xn = x_ref[...] + ada_ref[0, 2:3, :] * mix
    xo_ref[...] = xn
    y = xn * lax.rsqrt(jnp.mean(xn * xn, axis=-1, keepdims=True) + NORM_EPS) * g2_ref[...]
    h2 = y * (1.0 + ada_ref[0, 4:5, :]) + ada_ref[0, 3:4, :]
    h2_ref[...] = h2

    logits = lax.dot_general(wr_ref[...], h2.astype(BF16), (((1,), (1,)), ((), ())), preferred_element_type=F32)
    tm = logits.shape[1]
    aff_all = _sigmoid(logits)
    sel_all = aff_all + br_ref[...]
    sel = [sel_all[e:e + 1, :] for e in range(N_EXPERTS)]
    aff = [aff_all[e:e + 1, :] for e in range(N_EXPERTS)]
    neg = -jnp.inf

    def top2_sum(v0, v1, v2, v3):
        hi01, lo01 = jnp.maximum(v0, v1), jnp.minimum(v0, v1)
        hi23, lo23 = jnp.maximum(v2, v3), jnp.minimum(v2, v3)
        return jnp.maximum(hi01, hi23) + jnp.maximum(jnp.minimum(hi01, hi23), jnp.maximum(lo01, lo23))

    epg = EXPERTS_PER_GROUP
    scores = [top2_sum(*sel[g * epg:(g + 1) * epg]) for g in range(N_GROUPS)]
    best = jnp.zeros_like(scores[0])
    best_s = scores[0]
    for g in range(1, N_GROUPS):
        better = scores[g] > best_s
        best = jnp.where(better, float(g), best)
        best_s = jnp.where(better, scores[g], best_s)
    vals, affs = [], []
    for j in range(epg):
        vj, aj = sel[j], aff[j]
        for g in range(1, N_GROUPS):
            in_g = best == float(g)
            vj = jnp.where(in_g, sel[g * epg + j], vj)
            aj = jnp.where(in_g, aff[g * epg + j], aj)
        vals.append(vj)
        affs.append(aj)

    def first_argmax(vs):
        idx, m, a = jnp.zeros_like(vs[0]), vs[0], affs[0]
        for j in range(1, epg):
            gt = vs[j] > m
            idx = jnp.where(gt, float(j), idx)
            m = jnp.where(gt, vs[j], m)
            a = jnp.where(gt, affs[j], a)
        return idx, a

    i1, a1 = first_argmax(vals)
    i2, a2 = first_argmax([jnp.where(i1 == float(j), neg, vals[j]) for j in range(epg)])
    tot = a1 + a2
    srow = lax.broadcasted_iota(jnp.int32, (8, tm), 0)
    packed = jnp.where(srow == 0, best * epg + i1,
                       jnp.where(srow == 1, best * epg + i2,
                                 jnp.where(srow == 2, a1 / tot, jnp.where(srow == 3, a2 / tot, 0.0))))
    packed_t = jnp.concatenate([packed, jnp.zeros((LANES - 8, tm), F32)], axis=0).T
    e_ref[...] = packed_t[:, 0:TOP_K].astype(jnp.int32)
    w_ref[...] = packed_t[:, TOP_K:2 * TOP_K]


def _post_mixer(x2, gates, layer, o_list, l_list, y_b, ada_l, woa, wob, wout, gain2, w_router_t, b_router_c):
    t, d = x2.shape
    tm = 256
    per_b = SEQ // tm
    const = lambda shape: pl.BlockSpec(shape, lambda i: (0,) * len(shape), pipeline_mode=pl.Buffered(1))
    wblk = lambda k: pl.BlockSpec((1, k, d), lambda i: (layer, 0, 0), pipeline_mode=pl.Buffered(1))
    rowblk = lambda w: pl.BlockSpec((tm, w), lambda i: (i, 0))
    return pl.pallas_call(
        _post_mixer_kernel,
        out_shape=(
            jax.ShapeDtypeStruct((t, d), F32),
            jax.ShapeDtypeStruct((t, d), F32),
            jax.ShapeDtypeStruct((t, TOP_K), jnp.int32),
            jax.ShapeDtypeStruct((t, TOP_K), F32),
        ),
        grid=(t // tm,),
        in_specs=[
            rowblk(d),
            pl.BlockSpec((tm, d), lambda i: (i, 0)),
            pl.BlockSpec((tm, d), lambda i: (i, 1)),
            rowblk(A_OUT), rowblk(A_OUT), rowblk(A_OUT),
            rowblk(LANES), rowblk(LANES), rowblk(LANES),
            rowblk(GLA_VAL),
            pl.BlockSpec((1, 6, d), lambda i: (i // per_b, 0, 0)),
            wblk(A_OUT), wblk(GLA_VAL), wblk(d),
            const((1, d)), const((LANES, d)), const((LANES, 1)),
        ],
        out_specs=(rowblk(d), rowblk(d), rowblk(TOP_K), rowblk(TOP_K)),
        compiler_params=_params(56),
        name="post_mixer",
    )(x2, gates, gates, *o_list, *l_list, y_b, ada_l, woa, wob, wout, gain2, w_router_t, b_router_c)


DISPATCH_CHUNK = 512


def _dispatch_kernel(dest_ref, cnt_ref, h_hbm, x_hbm, sem):
    n_tok = h_hbm.shape[0]
    n_rows = x_hbm.shape[0]

    def copy(tok, row):
        return pltpu.make_async_copy(h_hbm.at[pl.ds(tok, 1), :], x_hbm.at[pl.ds(row, 1), :], sem)

    def chunk(c, start):
        def body(j, carry):
            tok = c * DISPATCH_CHUNK + j
            for k in range(TOP_K):
                cp = copy(tok, dest_ref[tok * TOP_K + k])
                if start:
                    cp.start(priority=k % 2)
                else:
                    cp.wait()
            return carry
        lax.fori_loop(0, DISPATCH_CHUNK, body, 0, unroll=4)

    n_chunks = n_tok // DISPATCH_CHUNK
    chunk(0, True)

    def pipelined(c, carry):
        chunk(c, True)
        chunk(c - 1, False)
        return carry
    lax.fori_loop(1, n_chunks, pipelined, 0)
    chunk(n_chunks - 1, False)

    def fill(lo, hi, start):
        def body(r, carry):
            cp = copy(0, r)
            if start:
                cp.start()
            else:
                cp.wait()
            return carry
        lax.fori_loop(lo, hi, body, 0)

    def pads(start):
        def per_expert(e, base):
            cnt = cnt_ref[e]
            padded = (cnt + (MOE_BLOCK - 1)) // MOE_BLOCK * MOE_BLOCK
            fill(base + cnt, base + padded, start)
            return base + padded
        end = lax.fori_loop(0, N_EXPERTS, per_expert, 0)
        fill(end, n_rows, start)

    pads(True)
    pads(False)


def _dispatch(dest, counts, h2, n_rows):
    t, d = h2.shape
    grid_spec = pltpu.PrefetchScalarGridSpec(
        num_scalar_prefetch=2,
        grid=(1,),
        in_specs=[pl.BlockSpec(memory_space=pl.ANY)],
        out_specs=pl.BlockSpec(memory_space=pl.ANY),
        scratch_shapes=[pltpu.SemaphoreType.DMA(())],
    )
    return pl.pallas_call(
        _dispatch_kernel,
        out_shape=jax.ShapeDtypeStruct((n_rows, d), h2.dtype),
        grid_spec=grid_spec,
        compiler_params=_params(16, ("arbitrary",)),
        name="moe_dispatch",
    )(dest, counts, h2)


def _moe_kernel(be_ref, nu_ref, x_ref, wg_ref, wu_ref, wd_ref, y_ref):
    i = pl.program_id(0)
    n_used = nu_ref[0]

    @pl.when(i < n_used)
    def _():
        xb = x_ref[...].astype(BF16)
        g = jnp.dot(xb, wg_ref[0, 0], preferred_element_type=F32)
        u = jnp.dot(xb, wu_ref[0, 0], preferred_element_type=F32)
        a = (g * _sigmoid(g) * u).astype(BF16)
        y_ref[...] = jnp.dot(a, wd_ref[0, 0], preferred_element_type=F32)

    @pl.when(i >= n_used)
    def _():
        y_ref[...] = jnp.zeros_like(y_ref)


def _moe_experts(blk_e, n_used, xbuf, layer, wg, wu, wd):
    n_rows, d = xbuf.shape
    n_blk = blk_e.shape[0]
    f = wg.shape[3]
    grid_spec = pltpu.PrefetchScalarGridSpec(
        num_scalar_prefetch=2,
        grid=(n_blk,),
        in_specs=[
            pl.BlockSpec((MOE_BLOCK, d), lambda i, be, nu: (jnp.minimum(i, nu[0] - 1), 0)),
            pl.BlockSpec((1, 1, d, f), lambda i, be, nu: (layer, be[i], 0, 0)),
            pl.BlockSpec((1, 1, d, f), lambda i, be, nu: (layer, be[i], 0, 0)),
            pl.BlockSpec((1, 1, f, d), lambda i, be, nu: (layer, be[i], 0, 0)),
        ],
        out_specs=pl.BlockSpec((MOE_BLOCK, d), lambda i, be, nu: (i, 0)),
    )
    return pl.pallas_call(
        _moe_kernel,
        out_shape=jax.ShapeDtypeStruct((n_rows, d), F32),
        grid_spec=grid_spec,
        compiler_params=_params(48, ("arbitrary",)),
        name="moe_experts",
    )(blk_e, n_used, xbuf, wg, wu, wd)


def _pair_copy(src_hbm, row, dst, slot, k, r, sem):
    return pltpu.make_async_copy(src_hbm.at[pl.ds(row, 1), :], dst.at[slot, k, pl.ds(r, 1), :], sem.at[slot])


def _combine_kernel(pos_ref, y_hbm, x_ref, w_ref, ada_ref, fg_ref, o_ref, yg, sem, *, tm, final):
    i = pl.program_id(0)
    n = pl.num_programs(0)
    slot = i % 2

    def gather(tile, s, start):
        def body(r, c):
            for k in range(TOP_K):
                cp = _pair_copy(y_hbm, pos_ref[(tile * tm + r) * TOP_K + k], yg, s, k, r, sem)
                if start:
                    cp.start(priority=k % 2)
                else:
                    cp.wait()
            return c
        lax.fori_loop(0, tm, body, 0, unroll=8)

    @pl.when(i == 0)
    def _():
        gather(0, 0, True)

    @pl.when(i + 1 < n)
    def _():
        gather(i + 1, 1 - slot, True)

    gather(i, slot, False)
    w = w_ref[...]
    y = w[:, 0:1] * yg[slot, 0] + w[:, 1:2] * yg[slot, 1]
    out = x_ref[...] + ada_ref[0, 5:6, :] * y
    if final:
        out = out * lax.rsqrt(jnp.mean(out * out, axis=-1, keepdims=True) + NORM_EPS) * fg_ref[...]
    o_ref[...] = out


def _combine(pos, ybuf, x2, gw, ada_l, final_gain, final):
    t, d = x2.shape
    tm = 256
    per_b = SEQ // tm
    kern = functools.partial(_combine_kernel, tm=tm, final=final)
    grid_spec = pltpu.PrefetchScalarGridSpec(
        num_scalar_prefetch=1,
        grid=(t // tm,),
        in_specs=[
            pl.BlockSpec(memory_space=pl.ANY),
            pl.BlockSpec((tm, d), lambda i, p: (i, 0)),
            pl.BlockSpec((tm, TOP_K), lambda i, p: (i, 0)),
            pl.BlockSpec((1, 6, d), lambda i, p: (i // per_b, 0, 0)),
            pl.BlockSpec((1, d), lambda i, p: (0, 0)),
        ],
        out_specs=pl.BlockSpec((tm, d), lambda i, p: (i, 0)),
        scratch_shapes=[pltpu.VMEM((2, TOP_K, tm, d), F32), pltpu.SemaphoreType.DMA((2,))],
    )
    return pl.pallas_call(
        kern,
        out_shape=jax.ShapeDtypeStruct((t, d), F32),
        grid_spec=grid_spec,
        compiler_params=_params(40, ("arbitrary",)),
        name="moe_combine_final" if final else "moe_combine",
    )(pos, ybuf, x2, gw, ada_l, final_gain)


def _plan_kernel(e_ref, dest_ref, cnt_ref, carry, pstart_s, *, tm):
    ph = pl.program_id(0)
    i = pl.program_id(1)
    lane = lax.broadcasted_iota(jnp.int32, (tm, LANES), 1)
    e = e_ref[...]
    oh0 = e[:, 0:1] == lane
    oh1 = e[:, 1:2] == lane
    oh = oh0.astype(F32) + oh1.astype(F32)
    tile_counts = jnp.sum(oh, axis=0, keepdims=True)

    @pl.when((ph == 0) & (i == 0))
    def _():
        carry[...] = jnp.zeros_like(carry)

    @pl.when(ph == 0)
    def _():
        carry[...] += tile_counts

    @pl.when((ph == 1) & (i == 0))
    def _():
        counts = carry[...]
        cnt_ref[...] = counts.astype(jnp.int32)
        nblk = jnp.floor((counts + (MOE_BLOCK - 1)) * (1.0 / MOE_BLOCK))
        r = lax.broadcasted_iota(jnp.int32, (LANES, LANES), 0)
        c = lax.broadcasted_iota(jnp.int32, (LANES, LANES), 1)
        upper = (r < c).astype(BF16)
        nb8 = jnp.broadcast_to(nblk, (8, LANES)).astype(BF16)
        pstart_s[...] = jnp.dot(nb8, upper, preferred_element_type=F32)[0:1, :] * float(MOE_BLOCK)
        carry[...] = jnp.zeros_like(carry)

    @pl.when(ph == 1)
    def _():
        r = lax.broadcasted_iota(jnp.int32, (tm, tm), 0)
        c = lax.broadcasted_iota(jnp.int32, (tm, tm), 1)
        lower = (r > c).astype(BF16)
        before = jnp.dot(lower, oh.astype(BF16), preferred_element_type=F32) + carry[...]
        base = before + pstart_s[...]
        d0 = jnp.sum(jnp.where(oh0, base, 0.0), axis=1, keepdims=True)
        d1 = jnp.sum(jnp.where(oh1, base, 0.0), axis=1, keepdims=True)
        col = lax.broadcasted_iota(jnp.int32, (tm, TOP_K), 1)
        dest_ref[...] = jnp.where(col == 0, d0, d1).astype(jnp.int32)
        carry[...] += tile_counts


def _dispatch_plan(eidx, n_blk):
    t = eidx.shape[0]
    tm = 512
    dest, counts = pl.pallas_call(
        functools.partial(_plan_kernel, tm=tm),
        out_shape=(jax.ShapeDtypeStruct((t, TOP_K), jnp.int32), jax.ShapeDtypeStruct((1, LANES), jnp.int32)),
        grid=(2, t // tm),
        in_specs=[pl.BlockSpec((tm, TOP_K), lambda p, i: (i, 0))],
        out_specs=(pl.BlockSpec((tm, TOP_K), lambda p, i: (i * p, 0)), pl.BlockSpec((1, LANES), lambda p, i: (0, 0))),
        scratch_shapes=[pltpu.VMEM((1, LANES), F32), pltpu.VMEM((1, LANES), F32)],
        compiler_params=_params(32, ("arbitrary", "arbitrary")),
        name="dispatch_plan",
    )(eidx)
    counts = counts[0, :N_EXPERTS]
    pcounts = (counts + MOE_BLOCK - 1) // MOE_BLOCK * MOE_BLOCK
    pend = jnp.cumsum(pcounts)
    blk_start = jnp.arange(n_blk, dtype=jnp.int32) * MOE_BLOCK
    blk_e = jnp.sum((pend[None, :] <= blk_start[:, None]).astype(jnp.int32), axis=1)
    blk_e = jnp.minimum(blk_e, N_EXPERTS - 1).astype(jnp.int32)
    n_used = (pend[-1] // MOE_BLOCK).astype(jnp.int32).reshape(1)
    return blk_e, n_used, dest.reshape(-1), counts


def _rope_tables():
    pos = jnp.arange(SEQ, dtype=F32)
    inv_freq = ROPE_THETA ** (-jnp.arange(0, HEAD_DIM, 2, dtype=F32) / HEAD_DIM)
    ang = pos[:, None] * inv_freq[None, :]
    cos, sin = jnp.cos(ang), jnp.sin(ang)
    return jnp.concatenate([cos, cos], axis=1), jnp.concatenate([-sin, sin], axis=1)


def kernel(x, c, w_ada, b_ada, norm_mix, norm_ffn, w_in, w_alpha2, b_alpha2, gla_gain, w_out_a, w_out_b, w_out,
           w_router, b_router, w_gate_e, w_up_e, w_down_e, final_norm):
    b, s, d = x.shape
    depth = w_ada.shape[0]
    t = b * s
    n_blk = (t * TOP_K) // MOE_BLOCK + N_EXPERTS
    cosf, sinf = _rope_tables()
    ada = _ada(c, w_ada, b_ada).reshape(depth, b, 6, d)
    w_router_t = jnp.pad(w_router.T, ((0, LANES - N_EXPERTS), (0, 0))).astype(BF16)
    b_router_c = jnp.pad(b_router, (0, LANES - N_EXPERTS)).reshape(LANES, 1)
    final_gain = final_norm.reshape(1, d)
    w_in_t = jnp.swapaxes(w_in, 1, 2)
    w_alow_all = jnp.pad(w_in_t[:, COL_ALOW:COL_GATES, :], ((0, 0), (0, LANES - GLA_LOWRANK), (0, 0))).astype(BF16)
    w_oa, w_ob, w_o = w_out_a.astype(BF16), w_out_b.astype(BF16), w_out.astype(BF16)
    w_g, w_u, w_d = w_gate_e.astype(BF16), w_up_e.astype(BF16), w_down_e.astype(BF16)
    x2 = x.reshape(t, d)
    for l in range(depth):
        w_alpha2p = jnp.pad(w_alpha2[l], ((0, LANES - GLA_LOWRANK), (0, 0))).astype(BF16)
        h, a_low = _norm_mod(x2, ada[l], norm_mix[l].reshape(1, d), w_alow_all[l])
        proj = _in_proj(h, w_in_t, l, 0, COL_ALOW, "in_proj")
        gates = _in_proj(h, w_in_t, l, COL_GATES, 2 * D_MODEL, "in_proj_gates")
        o_list, l_list = [], []
        for g in range(len(DILATED_GROUPS)):
            o_g, l_g = _attention(proj, cosf, sinf, g)
            o_list.append(o_g)
            l_list.append(l_g)
        y_b = _gla(proj, a_low, w_alpha2p, b_alpha2[l].reshape(1, GLA_KEY), gla_gain[l].reshape(1, GLA_VAL))
        x2, h2, eidx, gw = _post_mixer(
            x2, gates, l, o_list, l_list, y_b, ada[l], w_oa, w_ob, w_o,
            norm_ffn[l].reshape(1, d), w_router_t, b_router_c)
        blk_e, n_used, dest, counts = _dispatch_plan(eidx, n_blk)
        xbuf = _dispatch(dest, counts, h2, n_blk * MOE_BLOCK)
        ybuf = _moe_experts(blk_e, n_used, xbuf, l, w_g, w_u, w_d)
        x2 = _combine(dest, ybuf, x2, gw, ada[l], final_gain, final=(l == depth - 1))
    return x2.reshape(b, s, d)
```

```python
import functools

import jax
import jax.numpy as jnp
from jax import lax
from jax.experimental import pallas as pl
from jax.experimental.pallas import tpu as pltpu

F32 = jnp.float32
BF16 = jnp.bfloat16

D_MODEL = 2048
SEQ = 2048
HEAD_DIM = 128
A_HEADS_PER_GROUP = 4
DILATED_GROUPS = ((128, 1), (512, 4), (2048, 16))
A_WIDTH = 12 * HEAD_DIM
A_OUT = A_HEADS_PER_GROUP * HEAD_DIM
ATT_BLOCK = 128
ROPE_THETA = 10000.0
GLA_HEADS = 4
GLA_DK = 128
GLA_DV = 256
GLA_KEY = GLA_HEADS * GLA_DK
GLA_VAL = GLA_HEADS * GLA_DV
GLA_LOWRANK = 16
GLA_TAU = 16.0
GLA_CHUNK = 64
N_EXPERTS = 16
N_GROUPS = 4
EXPERTS_PER_GROUP = N_EXPERTS // N_GROUPS
TOP_K = 2
D_FF_EXPERT = 1024
NORM_EPS = 1e-6

LANES = 128
MOE_BLOCK = 256
PROJ_TILE = 512

COL_QA = 0
COL_KA = A_WIDTH
COL_VA = 2 * A_WIDTH
COL_QB = 3 * A_WIDTH
COL_KB = COL_QB + GLA_KEY
COL_VB = COL_KB + GLA_KEY
COL_RB = COL_VB + GLA_VAL
COL_ALOW = COL_RB + GLA_VAL
COL_GATES = COL_ALOW + GLA_LOWRANK


def _params(vmem_mib, sem=None):
    kw = dict(vmem_limit_bytes=int(vmem_mib) << 20)
    if sem is not None:
        kw["dimension_semantics"] = sem
    return pltpu.CompilerParams(**kw)


def _sigmoid(x):
    return 0.5 * jnp.tanh(0.5 * x) + 0.5


def _ada_kernel(c_ref, w_ref, b_ref, o_ref):
    @pl.when(pl.program_id(1) == 0)
    def _():
        o_ref[0] = jnp.broadcast_to(b_ref[0], o_ref.shape[1:])

    c = c_ref[...]
    ca = (c * _sigmoid(c)).astype(BF16)
    o_ref[0] += jnp.dot(ca, w_ref[0].astype(BF16), preferred_element_type=F32)


def _ada(c, w_ada, b_ada):
    depth, d, n = w_ada.shape
    b = c.shape[0]
    tk = 256
    return pl.pallas_call(
        _ada_kernel,
        out_shape=jax.ShapeDtypeStruct((depth, b, n), F32),
        grid=(depth, d // tk),
        in_specs=[
            pl.BlockSpec((b, tk), lambda l, k: (0, k)),
            pl.BlockSpec((1, tk, n), lambda l, k: (l, k, 0)),
            pl.BlockSpec((1, 1, n), lambda l, k: (l, 0, 0)),
        ],
        out_specs=pl.BlockSpec((1, b, n), lambda l, k: (l, 0, 0)),
        compiler_params=_params(40, ("arbitrary", "arbitrary")),
        name="ada",
    )(c, w_ada, b_ada.reshape(depth, 1, n))


def _norm_mod_kernel(x_ref, ada_ref, gain_ref, wal_ref, h_ref, al_ref):
    x = x_ref[...]
    y = x * lax.rsqrt(jnp.mean(x * x, axis=-1, keepdims=True) + NORM_EPS) * gain_ref[...]
    h = (y * (1.0 + ada_ref[0, 1:2, :]) + ada_ref[0, 0:1, :]).astype(BF16)
    h_ref[...] = h
    al_ref[...] = lax.dot_general(h, wal_ref[...], (((1,), (1,)), ((), ())), preferred_element_type=F32)


def _norm_mod(x2, ada_l, gain, w_alow):
    t, d = x2.shape
    tm = 512
    per_b = SEQ // tm
    return pl.pallas_call(
        _norm_mod_kernel,
        out_shape=(jax.ShapeDtypeStruct((t, d), BF16), jax.ShapeDtypeStruct((t, LANES), F32)),
        grid=(t // tm,),
        in_specs=[
            pl.BlockSpec((tm, d), lambda i: (i, 0)),
            pl.BlockSpec((1, 6, d), lambda i: (i // per_b, 0, 0)),
            pl.BlockSpec((1, d), lambda i: (0, 0)),
            pl.BlockSpec((LANES, d), lambda i: (0, 0)),
        ],
        out_specs=(pl.BlockSpec((tm, d), lambda i: (i, 0)), pl.BlockSpec((tm, LANES), lambda i: (i, 0))),
        compiler_params=_params(40),
        name="norm_mod",
    )(x2, ada_l, gain, w_alow)


def _matmul_nt_kernel(a_ref, b_ref, o_ref):
    o_ref[...] = lax.dot_general(a_ref[...], b_ref[...].astype(BF16), (((1,), (1,)), ((), ())),
                                 preferred_element_type=F32).astype(o_ref.dtype)


def _in_proj(h, w_t, layer, col0, n_cols, name):
    t, d = h.shape
    tm, tn = 2048, PROJ_TILE
    sub = 8
    assert col0 % sub == 0 and tn % sub == 0
    w_spec = pl.BlockSpec((pl.Squeezed(), pl.Element(tn), pl.Element(d)),
                          lambda i, j: (layer, (col0 // sub + j * (tn // sub)) * sub, 0))
    return pl.pallas_call(
        _matmul_nt_kernel,
        out_shape=jax.ShapeDtypeStruct((t, n_cols), BF16),
        grid=(t // tm, n_cols // tn),
        in_specs=[pl.BlockSpec((tm, d), lambda i, j: (i, 0)), w_spec],
        out_specs=pl.BlockSpec((tm, tn), lambda i, j: (i, j)),
        compiler_params=_params(48),
        name=name,
    )(h, w_t)


def _attn_kernel(q_ref, k_ref, v_ref, cos_ref, sin_ref, o_ref, lse_ref, q_s, k_s, v_s, o_s, l_s, *, dil, nb, wd):
    scale = HEAD_DIM ** -0.5
    blk = ATT_BLOCK
    nh = A_HEADS_PER_GROUP
    cosf = cos_ref[...]
    sinf = sin_ref[...]
    for h in range(nh):
        hc = slice(h * HEAD_DIM, (h + 1) * HEAD_DIM)
        tq = q_ref[:, hc].astype(F32)
        q_s[h] = (tq * cosf + pltpu.roll(tq, HEAD_DIM // 2, 1) * sinf) * scale
        tk = k_ref[:, hc].astype(F32)
        k_s[h] = tk * cosf + pltpu.roll(tk, HEAD_DIM // 2, 1) * sinf
        v_s[h] = v_ref[:, hc].astype(F32)

    def rows(start, size):
        return pl.ds(start, size) if dil == 1 else pl.ds(start, size, stride=dil)

    def head_mask(nk, offset):
        i = lax.broadcasted_iota(jnp.int32, (nh * blk, nk), 0) % blk
        j = lax.broadcasted_iota(jnp.int32, (nh * blk, nk), 1)
        dist = offset + i - j
        return (dist >= 0) & (dist <= wd)

    mask_first = head_mask(blk, 0)
    mask_later = head_mask(2 * blk, blk)
    lane = lax.broadcasted_iota(jnp.int32, (blk, LANES), 1)
    ones = jnp.ones((2 * blk, HEAD_DIM), BF16)

    def attend(qs, ks, nk, mask):
        s = jnp.concatenate(
            [lax.dot_general(q_s[h, rows(qs, blk), :].astype(BF16), k_s[h, rows(ks, nk), :].astype(BF16),
                             (((1,), (1,)), ((), ())), preferred_element_type=F32) for h in range(nh)], axis=0)
        s = jnp.where(mask, s, -jnp.inf)
        m = jnp.max(s, axis=-1, keepdims=True)
        p = jnp.exp(s - m).astype(BF16)
        lse_tile = jnp.zeros((blk, LANES), F32)
        for h in range(nh):
            v_aug = jnp.concatenate([v_s[h, rows(ks, nk), :].astype(BF16), ones[:nk]], axis=1)
            ov = jnp.dot(p[h * blk:(h + 1) * blk], v_aug, preferred_element_type=F32)
            den = ov[:, HEAD_DIM:]
            o_s[h, rows(qs, blk), :] = ov[:, :HEAD_DIM] / den
            lse_tile = jnp.where(lane == h, m[h * blk:(h + 1) * blk] + jnp.log(den), lse_tile)
        l_s[rows(qs, blk), :] = lse_tile

    def per_phase(ph, carry):
        attend(ph, ph, blk, mask_first)

        def per_blk(n, c):
            qs = ph + dil * blk * n
            ks = ph + dil * blk * (n - 1)
            if dil == 1:
                qs = pl.multiple_of(qs, blk)
                ks = pl.multiple_of(ks, blk)
            attend(qs, ks, 2 * blk, mask_later)
            return c

        if nb > 1:
            lax.fori_loop(1, nb, per_blk, 0, unroll=3)
        return carry

    lax.fori_loop(0, dil, per_phase, 0, unroll=4 if nb == 1 else 1)
    for h in range(nh):
        o_ref[:, h * HEAD_DIM:(h + 1) * HEAD_DIM] = o_s[h].astype(o_ref.dtype)
    lse_ref[...] = l_s[...]


def _attention(proj, cosf, sinf, group):
    window, dil = DILATED_GROUPS[group]
    t = proj.shape[0]
    nb = (SEQ // dil) // ATT_BLOCK
    col = lambda base: base // A_OUT + group
    kern = functools.partial(_attn_kernel, dil=dil, nb=nb, wd=window // dil)
    return pl.pallas_call(
        kern,
        out_shape=(jax.ShapeDtypeStruct((t, A_OUT), BF16), jax.ShapeDtypeStruct((t, LANES), F32)),
        grid=(t // SEQ,),
        in_specs=[
            pl.BlockSpec((SEQ, A_OUT), lambda b: (b, col(COL_QA))),
            pl.BlockSpec((SEQ, A_OUT), lambda b: (b, col(COL_KA))),
            pl.BlockSpec((SEQ, A_OUT), lambda b: (b, col(COL_VA))),
            pl.BlockSpec((SEQ, HEAD_DIM), lambda b: (0, 0)),
            pl.BlockSpec((SEQ, HEAD_DIM), lambda b: (0, 0)),
        ],
        out_specs=(pl.BlockSpec((SEQ, A_OUT), lambda b: (b, 0)), pl.BlockSpec((SEQ, LANES), lambda b: (b, 0))),
        scratch_shapes=[pltpu.VMEM((A_HEADS_PER_GROUP, SEQ, HEAD_DIM), F32) for _ in range(4)]
        + [pltpu.VMEM((SEQ, LANES), F32)],
        compiler_params=_params(48),
        name=f"dilated_attn_g{group}",
    )(proj, proj, proj, cosf, sinf)


def _gla_kernel(q_ref, k_ref, v0_ref, v1_ref, r0_ref, r1_ref, al_ref, wa_ref, ba_ref, gain_ref, y_ref, st_ref, *, rows):
    @pl.when(pl.program_id(1) == 0)
    def _():
        st_ref[...] = jnp.zeros_like(st_ref)

    ck = GLA_CHUNK
    half = GLA_HEADS // 2
    z = jnp.dot(al_ref[...].astype(BF16), wa_ref[...], preferred_element_type=F32) + ba_ref[...]
    log_a = (jnp.minimum(z, 0.0) - jnp.log(1.0 + jnp.exp(-jnp.abs(z)))) * (1.0 / GLA_TAU)
    ri = lax.broadcasted_iota(jnp.int32, (rows, rows), 0)
    ci = lax.broadcasted_iota(jnp.int32, (rows, rows), 1)
    same = (ri // ck) == (ci // ck)
    tri = (same & (ri >= ci)).astype(BF16)
    tot = same.astype(BF16)
    la_hi = log_a.astype(BF16)
    la_lo = (log_a - la_hi.astype(F32)).astype(BF16)
    b = jnp.dot(tri, la_hi, preferred_element_type=F32) + jnp.dot(tri, la_lo, preferred_element_type=F32)
    b_end = jnp.dot(tot, la_hi, preferred_element_type=F32) + jnp.dot(tot, la_lo, preferred_element_type=F32)
    q = q_ref[...].astype(F32) * (GLA_DK ** -0.5)
    k = k_ref[...].astype(F32)
    q_dec = (q * jnp.exp(b)).astype(BF16)
    k_inv = (k * jnp.exp(-b)).astype(BF16)
    k_end = (k * jnp.exp(b_end - b)).astype(BF16)
    decay = jnp.exp(b_end)
    r1 = lax.broadcasted_iota(jnp.int32, (ck, ck), 0)
    c1 = lax.broadcasted_iota(jnp.int32, (ck, ck), 1)
    causal = r1 >= c1
    nt = (((1,), (1,)), ((), ()))
    v_refs = (v0_ref, v1_ref)
    r_refs = (r0_ref, r1_ref)
    for c in range(rows // ck):
        rs = slice(c * ck, (c + 1) * ck)
        for h in range(GLA_HEADS):
            hk = slice(h * GLA_DK, (h + 1) * GLA_DK)
            hv = slice(h * GLA_DV, (h + 1) * GLA_DV)
            hb = slice((h % half) * GLA_DV, (h % half + 1) * GLA_DV)
            v = v_refs[h // half][rs, hb]
            attn = lax.dot_general(q_dec[rs, hk], k_inv[rs, hk], nt, preferred_element_type=F32)
            attn = jnp.where(causal, attn, 0.0).astype(BF16)
            st = st_ref[h]
            o = jnp.dot(attn, v, preferred_element_type=F32)
            o = o + lax.dot_general(q_dec[rs, hk], st.astype(BF16), nt, preferred_element_type=F32)
            v_t = v.astype(F32).T.astype(BF16)
            st_ref[h] = st * decay[c * ck:c * ck + 1, hk] + jnp.dot(v_t, k_end[rs, hk], preferred_element_type=F32)
            o = o * lax.rsqrt(jnp.mean(o * o, axis=-1, keepdims=True) + NORM_EPS) * gain_ref[:, hv]
            r = r_refs[h // half][rs, hb].astype(F32)
            y_ref[rs, hv] = (o * (r * _sigmoid(r))).astype(y_ref.dtype)


def _gla(proj, a_low, w_alpha2p, b_alpha2, gla_gain):
    t = proj.shape[0]
    rows = 256
    per_b = SEQ // rows
    kern = functools.partial(_gla_kernel, rows=rows)
    col = lambda base, j=0: (lambda b, s: (b * per_b + s, base // GLA_KEY + j))
    return pl.pallas_call(
        kern,
        out_shape=jax.ShapeDtypeStruct((t, GLA_VAL), BF16),
        grid=(t // SEQ, per_b),
        in_specs=[
            pl.BlockSpec((rows, GLA_KEY), col(COL_QB)),
            pl.BlockSpec((rows, GLA_KEY), col(COL_KB)),
            pl.BlockSpec((rows, GLA_KEY), col(COL_VB, 0)),
            pl.BlockSpec((rows, GLA_KEY), col(COL_VB, 1)),
            pl.BlockSpec((rows, GLA_KEY), col(COL_RB, 0)),
            pl.BlockSpec((rows, GLA_KEY), col(COL_RB, 1)),
            pl.BlockSpec((rows, LANES), lambda b, s: (b * per_b + s, 0)),
            pl.BlockSpec((LANES, GLA_KEY), lambda b, s: (0, 0)),
            pl.BlockSpec((1, GLA_KEY), lambda b, s: (0, 0)),
            pl.BlockSpec((1, GLA_VAL), lambda b, s: (0, 0)),
        ],
        out_specs=pl.BlockSpec((rows, GLA_VAL), lambda b, s: (b * per_b + s, 0)),
        scratch_shapes=[pltpu.VMEM((GLA_HEADS, GLA_DV, GLA_DK), F32)],
        compiler_params=_params(32),
        name="gla",
    )(proj, proj, proj, proj, proj, proj, a_low, w_alpha2p, b_alpha2, gla_gain)


def _post_mixer_kernel(x_ref, ga_ref, gb_ref, o0_ref, o1_ref, o2_ref, l0_ref, l1_ref, l2_ref, yb_ref, ada_ref,
                       woa_ref, wob_ref, wout_ref, g2_ref, wr_ref, br_ref,
                       xo_ref, h2_ref, e_ref, w_ref):
    ls = [l0_ref[...], l1_ref[...], l2_ref[...]]
    os_ = [o0_ref, o1_ref, o2_ref]
    mx = jnp.maximum(jnp.maximum(ls[0], ls[1]), ls[2])
    es = [jnp.exp(l - mx) for l in ls]
    zs = es[0] + es[1] + es[2]
    wt = [e / zs for e in es]
    parts = []
    for h in range(A_HEADS_PER_GROUP):
        hc = slice(h * HEAD_DIM, (h + 1) * HEAD_DIM)
        acc = wt[0][:, h:h + 1] * os_[0][:, hc].astype(F32)
        acc = acc + wt[1][:, h:h + 1] * os_[1][:, hc].astype(F32)
        acc = acc + wt[2][:, h:h + 1] * os_[2][:, hc].astype(F32)
        parts.append(acc)
    ya = jnp.concatenate(parts, axis=1).astype(BF16)
    ta = jnp.dot(ya, woa_ref[0], preferred_element_type=F32)
    tb = jnp.dot(yb_ref[...], wob_ref[0], preferred_element_type=F32)
    merged = _sigmoid(ga_ref[...].astype(F32)) * ta + _sigmoid(gb_ref[...].astype(F32)) * tb
    mix = jnp.dot(merged.astype(BF16), wout_ref[0], preferred_element_type=F32)
    xn = x_ref[...] + ada_ref[0, 2:3, :] * mix
    xo_ref[...] = xn
    y = xn * lax.rsqrt(jnp.mean(xn * xn, axis=-1, keepdims=True) + NORM_EPS) * g2_ref[...]
    h2 = y * (1.0 + ada_ref[0, 4:5, :]) + ada_ref[0, 3:4, :]
    h2_ref[...] = h2

    logits = lax.dot_general(wr_ref[...], h2.astype(BF16), (((1,), (1,)), ((), ())), preferred_element_type=F32)
    tm = logits.shape[1]
    aff_all = _sigmoid(logits)
    sel_all = aff_all + br_ref[...]
    sel = [sel_all[e:e + 1, :] for e in range(N_EXPERTS)]
    aff = [aff_all[e:e + 1, :] for e in range(N_EXPERTS)]
    neg = -jnp.inf

    def top2_sum(v0, v1, v2, v3):
        hi01, lo01 = jnp.maximum(v0, v1), jnp.minimum(v0, v1)
        hi23, lo23 = jnp.maximum(v2, v3), jnp.minimum(v2, v3)
        return jnp.maximum(hi01, hi23) + jnp.maximum(jnp.minimum(hi01, hi23), jnp.maximum(lo01, lo23))

    epg = EXPERTS_PER_GROUP
    scores = [top2_sum(*sel[g * epg:(g + 1) * epg]) for g in range(N_GROUPS)]
    best = jnp.zeros_like(scores[0])
    best_s = scores[0]
    for g in range(1, N_GROUPS):
        better = scores[g] > best_s
        best = jnp.where(better, float(g), best)
        best_s = jnp.where(better, scores[g], best_s)
    vals, affs = [], []
    for j in range(epg):
        vj, aj = sel[j], aff[j]
        for g in range(1, N_GROUPS):
            in_g = best == float(g)
            vj = jnp.where(in_g, sel[g * epg + j], vj)
            aj = jnp.where(in_g, aff[g * epg + j], aj)
        vals.append(vj)
        affs.append(aj)

    def first_argmax(vs):
        idx, m, a = jnp.zeros_like(vs[0]), vs[0], affs[0]
        for j in range(1, epg):
            gt = vs[j] > m
            idx = jnp.where(gt, float(j), idx)
            m = jnp.where(gt, vs[j], m)
            a = jnp.where(gt, affs[j], a)
        return idx, a

    i1, a1 = first_argmax(vals)
    i2, a2 = first_argmax([jnp.where(i1 == float(j), neg, vals[j]) for j in range(epg)])
    tot = a1 + a2
    srow = lax.broadcasted_iota(jnp.int32, (8, tm), 0)
    packed = jnp.where(srow == 0, best * epg + i1,
                       jnp.where(srow == 1, best * epg + i2,
                                 jnp.where(srow == 2, a1 / tot, jnp.where(srow == 3, a2 / tot, 0.0))))
    packed_t = jnp.concatenate([packed, jnp.zeros((LANES - 8, tm), F32)], axis=0).T
    e_ref[...] = packed_t[:, 0:TOP_K].astype(jnp.int32)
    w_ref[...] = packed_t[:, TOP_K:2 * TOP_K]


def _post_mixer(x2, gates, layer, o_list, l_list, y_b, ada_l, woa, wob, wout, gain2, w_router_t, b_router_c):
    t, d = x2.shape
    tm = 256
    per_b = SEQ // tm
    const = lambda shape: pl.BlockSpec(shape, lambda i: (0,) * len(shape), pipeline_mode=pl.Buffered(1))
    wblk = lambda k: pl.BlockSpec((1, k, d), lambda i: (layer, 0, 0), pipeline_mode=pl.Buffered(1))
    rowblk = lambda w: pl.BlockSpec((tm, w), lambda i: (i, 0))
    return pl.pallas_call(
        _post_mixer_kernel,
        out_shape=(
            jax.ShapeDtypeStruct((t, d), F32),
            jax.ShapeDtypeStruct((t, d), F32),
            jax.ShapeDtypeStruct((t, TOP_K), jnp.int32),
            jax.ShapeDtypeStruct((t, TOP_K), F32),
        ),
        grid=(t // tm,),
        in_specs=[
            rowblk(d),
            pl.BlockSpec((tm, d), lambda i: (i, 0)),
            pl.BlockSpec((tm, d), lambda i: (i, 1)),
            rowblk(A_OUT), rowblk(A_OUT), rowblk(A_OUT),
            rowblk(LANES), rowblk(LANES), rowblk(LANES),
            rowblk(GLA_VAL),
            pl.BlockSpec((1, 6, d), lambda i: (i // per_b, 0, 0)),
            wblk(A_OUT), wblk(GLA_VAL), wblk(d),
            const((1, d)), const((LANES, d)), const((LANES, 1)),
        ],
        out_specs=(rowblk(d), rowblk(d), rowblk(TOP_K), rowblk(TOP_K)),
        compiler_params=_params(56),
        name="post_mixer",
    )(x2, gates, gates, *o_list, *l_list, y_b, ada_l, woa, wob, wout, gain2, w_router_t, b_router_c)


def _dispatch_kernel(dest_ref, cnt_ref, h_ref, x_hbm, sem, *, tm):
    i = pl.program_id(0)
    n_rows = x_hbm.shape[0]

    def copy(r, row):
        return pltpu.make_async_copy(h_ref.at[pl.ds(r, 1), :], x_hbm.at[pl.ds(row, 1), :], sem)

    def rows(start):
        def body(r, carry):
            for k in range(TOP_K):
                cp = copy(r, dest_ref[(i * tm + r) * TOP_K + k])
                if start:
                    cp.start(priority=k % 2)
                else:
                    cp.wait()
            return carry
        lax.fori_loop(0, tm, body, 0, unroll=4)

    def fill(lo, hi, start):
        def body(r, carry):
            cp = copy(0, r)
            if start:
                cp.start()
            else:
                cp.wait()
            return carry
        lax.fori_loop(lo, hi, body, 0)

    def pads(start):
        def per_expert(e, base):
            cnt = cnt_ref[e]
            padded = (cnt + (MOE_BLOCK - 1)) // MOE_BLOCK * MOE_BLOCK
            fill(base + cnt, base + padded, start)
            return base + padded
        end = lax.fori_loop(0, N_EXPERTS, per_expert, 0)
        fill(end, n_rows, start)

    rows(True)

    @pl.when(i == 0)
    def _():
        pads(True)
        pads(False)

    rows(False)


def _dispatch(dest, counts, h2, n_rows):
    t, d = h2.shape
    tm = 256
    grid_spec = pltpu.PrefetchScalarGridSpec(
        num_scalar_prefetch=2,
        grid=(t // tm,),
        in_specs=[pl.BlockSpec((tm, d), lambda i, dst, cnt: (i, 0))],
        out_specs=pl.BlockSpec(memory_space=pl.ANY),
        scratch_shapes=[pltpu.SemaphoreType.DMA(())],
    )
    return pl.pallas_call(
        functools.partial(_dispatch_kernel, tm=tm),
        out_shape=jax.ShapeDtypeStruct((n_rows, d), h2.dtype),
        grid_spec=grid_spec,
        compiler_params=_params(16, ("arbitrary",)),
        name="moe_dispatch",
    )(dest, counts, h2)


def _moe_kernel(be_ref, nu_ref, x_ref, wg_ref, wu_ref, wd_ref, y_ref):
    i = pl.program_id(0)
    n_used = nu_ref[0]

    @pl.when(i < n_used)
    def _():
        xb = x_ref[...].astype(BF16)
        g = jnp.dot(xb, wg_ref[0, 0], preferred_element_type=F32)
        u = jnp.dot(xb, wu_ref[0, 0], preferred_element_type=F32)
        a = (g * _sigmoid(g) * u).astype(BF16)
        y_ref[...] = jnp.dot(a, wd_ref[0, 0], preferred_element_type=F32)

    @pl.when(i >= n_used)
    def _():
        y_ref[...] = jnp.zeros_like(y_ref)


def _moe_experts(blk_e, n_used, xbuf, layer, wg, wu, wd):
    n_rows, d = xbuf.shape
    n_blk = blk_e.shape[0]
    f = wg.shape[3]
    grid_spec = pltpu.PrefetchScalarGridSpec(
        num_scalar_prefetch=2,
        grid=(n_blk,),
        in_specs=[
            pl.BlockSpec((MOE_BLOCK, d), lambda i, be, nu: (jnp.minimum(i, nu[0] - 1), 0)),
            pl.BlockSpec((1, 1, d, f), lambda i, be, nu: (layer, be[i], 0, 0)),
            pl.BlockSpec((1, 1, d, f), lambda i, be, nu: (layer, be[i], 0, 0)),
            pl.BlockSpec((1, 1, f, d), lambda i, be, nu: (layer, be[i], 0, 0)),
        ],
        out_specs=pl.BlockSpec((MOE_BLOCK, d), lambda i, be, nu: (i, 0)),
    )
    return pl.pallas_call(
        _moe_kernel,
        out_shape=jax.ShapeDtypeStruct((n_rows, d), F32),
        grid_spec=grid_spec,
        compiler_params=_params(48, ("arbitrary",)),
        name="moe_experts",
    )(blk_e, n_used, xbuf, wg, wu, wd)


def _pair_copy(src_hbm, row, dst, slot, k, r, sem):
    return pltpu.make_async_copy(src_hbm.at[pl.ds(row, 1), :], dst.at[slot, k, pl.ds(r, 1), :], sem.at[slot])


def _combine_kernel(pos_ref, y_hbm, x_ref, w_ref, ada_ref, fg_ref, o_ref, yg, sem, *, tm, final):
    i = pl.program_id(0)
    n = pl.num_programs(0)
    slot = i % 2

    def gather(tile, s, start):
        def body(r, c):
            for k in range(TOP_K):
                cp = _pair_copy(y_hbm, pos_ref[(tile * tm + r) * TOP_K + k], yg, s, k, r, sem)
                if start:
                    cp.start(priority=k % 2)
                else:
                    cp.wait()
            return c
        lax.fori_loop(0, tm, body, 0, unroll=8)

    @pl.when(i == 0)
    def _():
        gather(0, 0, True)

    @pl.when(i + 1 < n)
    def _():
        gather(i + 1, 1 - slot, True)

    gather(i, slot, False)
    w = w_ref[...]
    y = w[:, 0:1] * yg[slot, 0] + w[:, 1:2] * yg[slot, 1]
    out = x_ref[...] + ada_ref[0, 5:6, :] * y
    if final:
        out = out * lax.rsqrt(jnp.mean(out * out, axis=-1, keepdims=True) + NORM_EPS) * fg_ref[...]
    o_ref[...] = out


def _combine(pos, ybuf, x2, gw, ada_l, final_gain, final):
    t, d = x2.shape
    tm = 256
    per_b = SEQ // tm
    kern = functools.partial(_combine_kernel, tm=tm, final=final)
    grid_spec = pltpu.PrefetchScalarGridSpec(
        num_scalar_prefetch=1,
        grid=(t // tm,),
        in_specs=[
            pl.BlockSpec(memory_space=pl.ANY),
            pl.BlockSpec((tm, d), lambda i, p: (i, 0)),
            pl.BlockSpec((tm, TOP_K), lambda i, p: (i, 0)),
            pl.BlockSpec((1, 6, d), lambda i, p: (i // per_b, 0, 0)),
            pl.BlockSpec((1, d), lambda i, p: (0, 0)),
        ],
        out_specs=pl.BlockSpec((tm, d), lambda i, p: (i, 0)),
        scratch_shapes=[pltpu.VMEM((2, TOP_K, tm, d), F32), pltpu.SemaphoreType.DMA((2,))],
    )
    return pl.pallas_call(
        kern,
        out_shape=jax.ShapeDtypeStruct((t, d), F32),
        grid_spec=grid_spec,
        compiler_params=_params(40, ("arbitrary",)),
        name="moe_combine_final" if final else "moe_combine",
    )(pos, ybuf, x2, gw, ada_l, final_gain)


def _plan_kernel(e_ref, dest_ref, cnt_ref, carry, pstart_s, *, tm):
    ph = pl.program_id(0)
    i = pl.program_id(1)
    lane = lax.broadcasted_iota(jnp.int32, (tm, LANES), 1)
    e = e_ref[...]
    oh0 = e[:, 0:1] == lane
    oh1 = e[:, 1:2] == lane
    oh = oh0.astype(F32) + oh1.astype(F32)
    tile_counts = jnp.sum(oh, axis=0, keepdims=True)

    @pl.when((ph == 0) & (i == 0))
    def _():
        carry[...] = jnp.zeros_like(carry)

    @pl.when(ph == 0)
    def _():
        carry[...] += tile_counts

    @pl.when((ph == 1) & (i == 0))
    def _():
        counts = carry[...]
        cnt_ref[...] = counts.astype(jnp.int32)
        nblk = jnp.floor((counts + (MOE_BLOCK - 1)) * (1.0 / MOE_BLOCK))
        r = lax.broadcasted_iota(jnp.int32, (LANES, LANES), 0)
        c = lax.broadcasted_iota(jnp.int32, (LANES, LANES), 1)
        upper = (r < c).astype(BF16)
        nb8 = jnp.broadcast_to(nblk, (8, LANES)).astype(BF16)
        pstart_s[...] = jnp.dot(nb8, upper, preferred_element_type=F32)[0:1, :] * float(MOE_BLOCK)
        carry[...] = jnp.zeros_like(carry)

    @pl.when(ph == 1)
    def _():
        r = lax.broadcasted_iota(jnp.int32, (tm, tm), 0)
        c = lax.broadcasted_iota(jnp.int32, (tm, tm), 1)
        lower = (r > c).astype(BF16)
        before = jnp.dot(lower, oh.astype(BF16), preferred_element_type=F32) + carry[...]
        base = before + pstart_s[...]
        d0 = jnp.sum(jnp.where(oh0, base, 0.0), axis=1, keepdims=True)
        d1 = jnp.sum(jnp.where(oh1, base, 0.0), axis=1, keepdims=True)
        col = lax.broadcasted_iota(jnp.int32, (tm, TOP_K), 1)
        dest_ref[...] = jnp.where(col == 0, d0, d1).astype(jnp.int32)
        carry[...] += tile_counts


def _dispatch_plan(eidx, n_blk):
    t = eidx.shape[0]
    tm = 512
    dest, counts = pl.pallas_call(
        functools.partial(_plan_kernel, tm=tm),
        out_shape=(jax.ShapeDtypeStruct((t, TOP_K), jnp.int32), jax.ShapeDtypeStruct((1, LANES), jnp.int32)),
        grid=(2, t // tm),
        in_specs=[pl.BlockSpec((tm, TOP_K), lambda p, i: (i, 0))],
        out_specs=(pl.BlockSpec((tm, TOP_K), lambda p, i: (i * p, 0)), pl.BlockSpec((1, LANES), lambda p, i: (0, 0))),
        scratch_shapes=[pltpu.VMEM((1, LANES), F32), pltpu.VMEM((1, LANES), F32)],
        compiler_params=_params(32, ("arbitrary", "arbitrary")),
        name="dispatch_plan",
    )(eidx)
    counts = counts[0, :N_EXPERTS]
    pcounts = (counts + MOE_BLOCK - 1) // MOE_BLOCK * MOE_BLOCK
    pend = jnp.cumsum(pcounts)
    blk_start = jnp.arange(n_blk, dtype=jnp.int32) * MOE_BLOCK
    blk_e = jnp.sum((pend[None, :] <= blk_start[:, None]).astype(jnp.int32), axis=1)
    blk_e = jnp.minimum(blk_e, N_EXPERTS - 1).astype(jnp.int32)
    n_used = (pend[-1] // MOE_BLOCK).astype(jnp.int32).reshape(1)
    return blk_e, n_used, dest.reshape(-1), counts


def _rope_tables():
    pos = jnp.arange(SEQ, dtype=F32)
    inv_freq = ROPE_THETA ** (-jnp.arange(0, HEAD_DIM, 2, dtype=F32) / HEAD_DIM)
    ang = pos[:, None] * inv_freq[None, :]
    cos, sin = jnp.cos(ang), jnp.sin(ang)
    return jnp.concatenate([cos, cos], axis=1), jnp.concatenate([-sin, sin], axis=1)


def kernel(x, c, w_ada, b_ada, norm_mix, norm_ffn, w_in, w_alpha2, b_alpha2, gla_gain, w_out_a, w_out_b, w_out,
           w_router, b_router, w_gate_e, w_up_e, w_down_e, final_norm):
    b, s, d = x.shape
    depth = w_ada.shape[0]
    t = b * s
    n_blk = (t * TOP_K) // MOE_BLOCK + N_EXPERTS
    cosf, sinf = _rope_tables()
    ada = _ada(c, w_ada, b_ada).reshape(depth, b, 6, d)
    w_router_t = jnp.pad(w_router.T, ((0, LANES - N_EXPERTS), (0, 0))).astype(BF16)
    b_router_c = jnp.pad(b_router, (0, LANES - N_EXPERTS)).reshape(LANES, 1)
    final_gain = final_norm.reshape(1, d)
    w_in_t = jnp.swapaxes(w_in, 1, 2)
    w_alow_all = jnp.pad(w_in_t[:, COL_ALOW:COL_GATES, :], ((0, 0), (0, LANES - GLA_LOWRANK), (0, 0))).astype(BF16)
    w_oa, w_ob, w_o = w_out_a.astype(BF16), w_out_b.astype(BF16), w_out.astype(BF16)
    w_g, w_u, w_d = w_gate_e.astype(BF16), w_up_e.astype(BF16), w_down_e.astype(BF16)
    x2 = x.reshape(t, d)
    for l in range(depth):
        w_alpha2p = jnp.pad(w_alpha2[l], ((0, LANES - GLA_LOWRANK), (0, 0))).astype(BF16)
        h, a_low = _norm_mod(x2, ada[l], norm_mix[l].reshape(1, d), w_alow_all[l])
        proj = _in_proj(h, w_in_t, l, 0, COL_ALOW, "in_proj")
        gates = _in_proj(h, w_in_t, l, COL_GATES, 2 * D_MODEL, "in_proj_gates")
        o_list, l_list = [], []
        for g in range(len(DILATED_GROUPS)):
            o_g, l_g = _attention(proj, cosf, sinf, g)
            o_list.append(o_g)
            l_list.append(l_g)
        y_b = _gla(proj, a_low, w_alpha2p, b_alpha2[l].reshape(1, GLA_KEY), gla_gain[l].reshape(1, GLA_VAL))
        x2, h2, eidx, gw = _post_mixer(
            x2, gates, l, o_list, l_list, y_b, ada[l], w_oa, w_ob, w_o,
            norm_ffn[l].reshape(1, d), w_router_t, b_router_c)
        blk_e, n_used, dest, counts = _dispatch_plan(eidx, n_blk)
        xbuf = _dispatch(dest, counts, h2, n_blk * MOE_BLOCK)
        ybuf = _moe_experts(blk_e, n_used, xbuf, l, w_g, w_u, w_d)
        x2 = _combine(dest, ybuf, x2, gw, ada[l], final_gain, final=(l == depth - 1))
    return x2.reshape(b, s, d)
```

```python
import functools

import jax
import jax.numpy as jnp
from jax import lax
from jax.experimental import pallas as pl
from jax.experimental.pallas import tpu as pltpu

F32 = jnp.float32
BF16 = jnp.bfloat16

D_MODEL = 2048
SEQ = 2048
HEAD_DIM = 128
A_HEADS_PER_GROUP = 4
DILATED_GROUPS = ((128, 1), (512, 4), (2048, 16))
A_WIDTH = 12 * HEAD_DIM
A_OUT = A_HEADS_PER_GROUP * HEAD_DIM
ATT_BLOCK = 128
ROPE_THETA = 10000.0
GLA_HEADS = 4
GLA_DK = 128
GLA_DV = 256
GLA_KEY = GLA_HEADS * GLA_DK
GLA_VAL = GLA_HEADS * GLA_DV
GLA_LOWRANK = 16
GLA_TAU = 16.0
GLA_CHUNK = 64
N_EXPERTS = 16
N_GROUPS = 4
EXPERTS_PER_GROUP = N_EXPERTS // N_GROUPS
TOP_K = 2
D_FF_EXPERT = 1024
NORM_EPS = 1e-6

LANES = 128
MOE_BLOCK = 256
PROJ_TILE = 512

COL_QA = 0
COL_KA = A_WIDTH
COL_VA = 2 * A_WIDTH
COL_QB = 3 * A_WIDTH
COL_KB = COL_QB + GLA_KEY
COL_VB = COL_KB + GLA_KEY
COL_RB = COL_VB + GLA_VAL
COL_ALOW = COL_RB + GLA_VAL
COL_GATES = COL_ALOW + GLA_LOWRANK


def _params(vmem_mib, sem=None):
    kw = dict(vmem_limit_bytes=int(vmem_mib) << 20)
    if sem is not None:
        kw["dimension_semantics"] = sem
    return pltpu.CompilerParams(**kw)


def _sigmoid(x):
    return 0.5 * jnp.tanh(0.5 * x) + 0.5


def _ada_kernel(c_ref, w_ref, b_ref, o_ref):
    @pl.when(pl.program_id(1) == 0)
    def _():
        o_ref[0] = jnp.broadcast_to(b_ref[0], o_ref.shape[1:])

    c = c_ref[...]
    ca = (c * _sigmoid(c)).astype(BF16)
    o_ref[0] += jnp.dot(ca, w_ref[0].astype(BF16), preferred_element_type=F32)


def _ada(c, w_ada, b_ada):
    depth, d, n = w_ada.shape
    b = c.shape[0]
    tk = 256
    return pl.pallas_call(
        _ada_kernel,
        out_shape=jax.ShapeDtypeStruct((depth, b, n), F32),
        grid=(depth, d // tk),
        in_specs=[
            pl.BlockSpec((b, tk), lambda l, k: (0, k)),
            pl.BlockSpec((1, tk, n), lambda l, k: (l, k, 0)),
            pl.BlockSpec((1, 1, n), lambda l, k: (l, 0, 0)),
        ],
        out_specs=pl.BlockSpec((1, b, n), lambda l, k: (l, 0, 0)),
        compiler_params=_params(40, ("arbitrary", "arbitrary")),
        name="ada",
    )(c, w_ada, b_ada.reshape(depth, 1, n))


def _norm_mod_kernel(x_ref, ada_ref, gain_ref, wal_ref, h_ref, al_ref):
    x = x_ref[...]
    y = x * lax.rsqrt(jnp.mean(x * x, axis=-1, keepdims=True) + NORM_EPS) * gain_ref[...]
    h = (y * (1.0 + ada_ref[0, 1:2, :]) + ada_ref[0, 0:1, :]).astype(BF16)
    h_ref[...] = h
    al_ref[...] = lax.dot_general(h, wal_ref[...], (((1,), (1,)), ((), ())), preferred_element_type=F32)


def _norm_mod(x2, ada_l, gain, w_alow):
    t, d = x2.shape
    tm = 512
    per_b = SEQ // tm
    return pl.pallas_call(
        _norm_mod_kernel,
        out_shape=(jax.ShapeDtypeStruct((t, d), BF16), jax.ShapeDtypeStruct((t, LANES), F32)),
        grid=(t // tm,),
        in_specs=[
            pl.BlockSpec((tm, d), lambda i: (i, 0)),
            pl.BlockSpec((1, 6, d), lambda i: (i // per_b, 0, 0)),
            pl.BlockSpec((1, d), lambda i: (0, 0)),
            pl.BlockSpec((LANES, d), lambda i: (0, 0)),
        ],
        out_specs=(pl.BlockSpec((tm, d), lambda i: (i, 0)), pl.BlockSpec((tm, LANES), lambda i: (i, 0))),
        compiler_params=_params(40),
        name="norm_mod",
    )(x2, ada_l, gain, w_alow)


def _matmul_nt_kernel(a_ref, b_ref, o_ref):
    o_ref[...] = lax.dot_general(a_ref[...], b_ref[...].astype(BF16), (((1,), (1,)), ((), ())),
                                 preferred_element_type=F32).astype(o_ref.dtype)


def _in_proj(h, w_t, layer, col0, n_cols, name):
    t, d = h.shape
    tm, tn = 2048, PROJ_TILE
    sub = 8
    assert col0 % sub == 0 and tn % sub == 0
    w_spec = pl.BlockSpec((pl.Squeezed(), pl.Element(tn), pl.Element(d)),
                          lambda i, j: (layer, (col0 // sub + j * (tn // sub)) * sub, 0))
    return pl.pallas_call(
        _matmul_nt_kernel,
        out_shape=jax.ShapeDtypeStruct((t, n_cols), BF16),
        grid=(t // tm, n_cols // tn),
        in_specs=[pl.BlockSpec((tm, d), lambda i, j: (i, 0)), w_spec],
        out_specs=pl.BlockSpec((tm, tn), lambda i, j: (i, j)),
        compiler_params=_params(48),
        name=name,
    )(h, w_t)


def _attn_kernel(q_ref, k_ref, v_ref, cos_ref, sin_ref, o_ref, lse_ref, q_s, k_s, v_s, o_s, l_s, *, dil, nb, wd):
    scale = HEAD_DIM ** -0.5
    blk = ATT_BLOCK
    nh = A_HEADS_PER_GROUP
    cosf = cos_ref[...]
    sinf = sin_ref[...]
    for h in range(nh):
        hc = slice(h * HEAD_DIM, (h + 1) * HEAD_DIM)
        tq = q_ref[:, hc].astype(F32)
        q_s[h] = (tq * cosf + pltpu.roll(tq, HEAD_DIM // 2, 1) * sinf) * scale
        tk = k_ref[:, hc].astype(F32)
        k_s[h] = tk * cosf + pltpu.roll(tk, HEAD_DIM // 2, 1) * sinf
        v_s[h] = v_ref[:, hc].astype(F32)

    def rows(start, size):
        return pl.ds(start, size) if dil == 1 else pl.ds(start, size, stride=dil)

    def head_mask(nk, offset):
        i = lax.broadcasted_iota(jnp.int32, (nh * blk, nk), 0) % blk
        j = lax.broadcasted_iota(jnp.int32, (nh * blk, nk), 1)
        dist = offset + i - j
        return (dist >= 0) & (dist <= wd)

    mask_first = head_mask(blk, 0)
    mask_later = head_mask(2 * blk, blk)
    lane = lax.broadcasted_iota(jnp.int32, (blk, LANES), 1)
    ones = jnp.ones((2 * blk, HEAD_DIM), BF16)

    def attend(qs, ks, nk, mask):
        s = jnp.concatenate(
            [lax.dot_general(q_s[h, rows(qs, blk), :].astype(BF16), k_s[h, rows(ks, nk), :].astype(BF16),
                             (((1,), (1,)), ((), ())), preferred_element_type=F32) for h in range(nh)], axis=0)
        s = jnp.where(mask, s, -jnp.inf)
        m = jnp.max(s, axis=-1, keepdims=True)
        p = jnp.exp(s - m).astype(BF16)
        lse_tile = jnp.zeros((blk, LANES), F32)
        for h in range(nh):
            v_aug = jnp.concatenate([v_s[h, rows(ks, nk), :].astype(BF16), ones[:nk]], axis=1)
            ov = jnp.dot(p[h * blk:(h + 1) * blk], v_aug, preferred_element_type=F32)
            den = ov[:, HEAD_DIM:]
            o_s[h, rows(qs, blk), :] = ov[:, :HEAD_DIM] / den
            lse_tile = jnp.where(lane == h, m[h * blk:(h + 1) * blk] + jnp.log(den), lse_tile)
        l_s[rows(qs, blk), :] = lse_tile

    def per_phase(ph, carry):
        attend(ph, ph, blk, mask_first)

        def per_blk(n, c):
            qs = ph + dil * blk * n
            ks = ph + dil * blk * (n - 1)
            if dil == 1:
                qs = pl.multiple_of(qs, blk)
                ks = pl.multiple_of(ks, blk)
            attend(qs, ks, 2 * blk, mask_later)
            return c

        if nb > 1:
            lax.fori_loop(1, nb, per_blk, 0, unroll=3)
        return carry

    lax.fori_loop(0, dil, per_phase, 0, unroll=4 if nb == 1 else 1)
    for h in range(nh):
        o_ref[:, h * HEAD_DIM:(h + 1) * HEAD_DIM] = o_s[h].astype(o_ref.dtype)
    lse_ref[...] = l_s[...]


def _attention(proj, cosf, sinf, group):
    window, dil = DILATED_GROUPS[group]
    t = proj.shape[0]
    nb = (SEQ // dil) // ATT_BLOCK
    col = lambda base: base // A_OUT + group
    kern = functools.partial(_attn_kernel, dil=dil, nb=nb, wd=window // dil)
    return pl.pallas_call(
        kern,
        out_shape=(jax.ShapeDtypeStruct((t, A_OUT), BF16), jax.ShapeDtypeStruct((t, LANES), F32)),
        grid=(t // SEQ,),
        in_specs=[
            pl.BlockSpec((SEQ, A_OUT), lambda b: (b, col(COL_QA))),
            pl.BlockSpec((SEQ, A_OUT), lambda b: (b, col(COL_KA))),
            pl.BlockSpec((SEQ, A_OUT), lambda b: (b, col(COL_VA))),
            pl.BlockSpec((SEQ, HEAD_DIM), lambda b: (0, 0)),
            pl.BlockSpec((SEQ, HEAD_DIM), lambda b: (0, 0)),
        ],
        out_specs=(pl.BlockSpec((SEQ, A_OUT), lambda b: (b, 0)), pl.BlockSpec((SEQ, LANES), lambda b: (b, 0))),
        scratch_shapes=[pltpu.VMEM((A_HEADS_PER_GROUP, SEQ, HEAD_DIM), F32) for _ in range(4)]
        + [pltpu.VMEM((SEQ, LANES), F32)],
        compiler_params=_params(48),
        name=f"dilated_attn_g{group}",
    )(proj, proj, proj, cosf, sinf)


def _gla_kernel(q_ref, k_ref, v0_ref, v1_ref, r0_ref, r1_ref, al_ref, wa_ref, ba_ref, gain_ref, y_ref, st_ref, *, rows):
    @pl.when(pl.program_id(1) == 0)
    def _():
        st_ref[...] = jnp.zeros_like(st_ref)

    ck = GLA_CHUNK
    nseq = q_ref.shape[0]
    half = GLA_HEADS // 2
    ri = lax.broadcasted_iota(jnp.int32, (rows, rows), 0)
    ci = lax.broadcasted_iota(jnp.int32, (rows, rows), 1)
    same = (ri // ck) == (ci // ck)
    tri = (same & (ri >= ci)).astype(BF16)
    tot = same.astype(BF16)

    def prepare(sq):
        z = jnp.dot(al_ref[sq].astype(BF16), wa_ref[...], preferred_element_type=F32) + ba_ref[...]
        log_a = (jnp.minimum(z, 0.0) - jnp.log(1.0 + jnp.exp(-jnp.abs(z)))) * (1.0 / GLA_TAU)
        la_hi = log_a.astype(BF16)
        la_lo = (log_a - la_hi.astype(F32)).astype(BF16)
        b = jnp.dot(tri, la_hi, preferred_element_type=F32) + jnp.dot(tri, la_lo, preferred_element_type=F32)
        b_end = jnp.dot(tot, la_hi, preferred_element_type=F32) + jnp.dot(tot, la_lo, preferred_element_type=F32)
        q = q_ref[sq].astype(F32) * (GLA_DK ** -0.5)
        k = k_ref[sq].astype(F32)
        return ((q * jnp.exp(b)).astype(BF16), (k * jnp.exp(-b)).astype(BF16),
                (k * jnp.exp(b_end - b)).astype(BF16), jnp.exp(b_end))

    prepared = [prepare(sq) for sq in range(nseq)]
    r1 = lax.broadcasted_iota(jnp.int32, (ck, ck), 0)
    c1 = lax.broadcasted_iota(jnp.int32, (ck, ck), 1)
    causal = r1 >= c1
    nt = (((1,), (1,)), ((), ()))
    v_refs = (v0_ref, v1_ref)
    r_refs = (r0_ref, r1_ref)
    for c in range(rows // ck):
        rs = slice(c * ck, (c + 1) * ck)
        for sq in range(nseq):
            q_dec, k_inv, k_end, decay = prepared[sq]
            for h in range(GLA_HEADS):
                hk = slice(h * GLA_DK, (h + 1) * GLA_DK)
                hv = slice(h * GLA_DV, (h + 1) * GLA_DV)
                hb = slice((h % half) * GLA_DV, (h % half + 1) * GLA_DV)
                v = v_refs[h // half][sq, rs, hb]
                attn = lax.dot_general(q_dec[rs, hk], k_inv[rs, hk], nt, preferred_element_type=F32)
                attn = jnp.where(causal, attn, 0.0).astype(BF16)
                st = st_ref[sq * GLA_HEADS + h]
                o = jnp.dot(attn, v, preferred_element_type=F32)
                o = o + lax.dot_general(q_dec[rs, hk], st.astype(BF16), nt, preferred_element_type=F32)
                v_t = v.astype(F32).T.astype(BF16)
                st_ref[sq * GLA_HEADS + h] = (st * decay[c * ck:c * ck + 1, hk]
                                              + jnp.dot(v_t, k_end[rs, hk], preferred_element_type=F32))
                o = o * lax.rsqrt(jnp.mean(o * o, axis=-1, keepdims=True) + NORM_EPS) * gain_ref[:, hv]
                r = r_refs[h // half][sq, rs, hb].astype(F32)
                y_ref[sq, rs, hv] = (o * (r * _sigmoid(r))).astype(y_ref.dtype)


def _gla(proj, a_low, w_alpha2p, b_alpha2, gla_gain):
    t, width = proj.shape
    nb = t // SEQ
    rows = 256
    nseq = 2 if nb % 2 == 0 else 1
    kern = functools.partial(_gla_kernel, rows=rows)
    col = lambda base, j=0: (lambda b, s: (b, s, base // GLA_KEY + j))
    blk = lambda w: (nseq, rows, w)
    proj3 = proj.reshape(nb, SEQ, width)
    y = pl.pallas_call(
        kern,
        out_shape=jax.ShapeDtypeStruct((nb, SEQ, GLA_VAL), BF16),
        grid=(nb // nseq, SEQ // rows),
        in_specs=[
            pl.BlockSpec(blk(GLA_KEY), col(COL_QB)),
            pl.BlockSpec(blk(GLA_KEY), col(COL_KB)),
            pl.BlockSpec(blk(GLA_KEY), col(COL_VB, 0)),
            pl.BlockSpec(blk(GLA_KEY), col(COL_VB, 1)),
            pl.BlockSpec(blk(GLA_KEY), col(COL_RB, 0)),
            pl.BlockSpec(blk(GLA_KEY), col(COL_RB, 1)),
            pl.BlockSpec(blk(LANES), lambda b, s: (b, s, 0)),
            pl.BlockSpec((LANES, GLA_KEY), lambda b, s: (0, 0)),
            pl.BlockSpec((1, GLA_KEY), lambda b, s: (0, 0)),
            pl.BlockSpec((1, GLA_VAL), lambda b, s: (0, 0)),
        ],
        out_specs=pl.BlockSpec(blk(GLA_VAL), lambda b, s: (b, s, 0)),
        scratch_shapes=[pltpu.VMEM((nseq * GLA_HEADS, GLA_DV, GLA_DK), F32)],
        compiler_params=_params(32),
        name="gla",
    )(proj3, proj3, proj3, proj3, proj3, proj3, a_low.reshape(nb, SEQ, LANES), w_alpha2p, b_alpha2, gla_gain)
    return y.reshape(t, GLA_VAL)


def _post_mixer_kernel(x_ref, ga_ref, gb_ref, o0_ref, o1_ref, o2_ref, l0_ref, l1_ref, l2_ref, yb_ref, ada_ref,
                       woa_ref, wob_ref, wout_ref, g2_ref, wr_ref, br_ref,
                       xo_ref, h2_ref, e_ref, w_ref):
    ls = [l0_ref[...], l1_ref[...], l2_ref[...]]
    os_ = [o0_ref, o1_ref, o2_ref]
    mx = jnp.maximum(jnp.maximum(ls[0], ls[1]), ls[2])
    es = [jnp.exp(l - mx) for l in ls]
    zs = es[0] + es[1] + es[2]
    wt = [e / zs for e in es]
    parts = []
    for h in range(A_HEADS_PER_GROUP):
        hc = slice(h * HEAD_DIM, (h + 1) * HEAD_DIM)
        acc = wt[0][:, h:h + 1] * os_[0][:, hc].astype(F32)
        acc = acc + wt[1][:, h:h + 1] * os_[1][:, hc].astype(F32)
        acc = acc + wt[2][:, h:h + 1] * os_[2][:, hc].astype(F32)
        parts.append(acc)
    ya = jnp.concatenate(parts, axis=1).astype(BF16)
    ta = jnp.dot(ya, woa_ref[0], preferred_element_type=F32)
    tb = jnp.dot(yb_ref[...], wob_ref[0], preferred_element_type=F32)
    merged = _sigmoid(ga_ref[...].astype(F32)) * ta + _sigmoid(gb_ref[...].astype(F32)) * tb
    mix = jnp.dot(merged.astype(BF16), wout_ref[0], preferred_element_type=F32)
    xn = x_ref[...] + ada_ref[0, 2:3, :] * mix
    xo_ref[...] = xn
    y = xn * lax.rsqrt(jnp.mean(xn * xn, axis=-1, keepdims=True) + NORM_EPS) * g2_ref[...]
    h2 = y * (1.0 + ada_ref[0, 4:5, :]) + ada_ref[0, 3:4, :]
    h2_ref[...] = h2

    logits = lax.dot_general(wr_ref[...], h2.astype(BF16), (((1,), (1,)), ((), ())), preferred_element_type=F32)
    tm = logits.shape[1]
    aff_all = _sigmoid(logits)
    sel_all = aff_all + br_ref[...]
    sel = [sel_all[e:e + 1, :] for e in range(N_EXPERTS)]
    aff = [aff_all[e:e + 1, :] for e in range(N_EXPERTS)]
    neg = -jnp.inf

    def top2_sum(v0, v1, v2, v3):
        hi01, lo01 = jnp.maximum(v0, v1), jnp.minimum(v0, v1)
        hi23, lo23 = jnp.maximum(v2, v3), jnp.minimum(v2, v3)
        return jnp.maximum(hi01, hi23) + jnp.maximum(jnp.minimum(hi01, hi23), jnp.maximum(lo01, lo23))

    epg = EXPERTS_PER_GROUP
    scores = [top2_sum(*sel[g * epg:(g + 1) * epg]) for g in range(N_GROUPS)]
    best = jnp.zeros_like(scores[0])
    best_s = scores[0]
    for g in range(1, N_GROUPS):
        better = scores[g] > best_s
        best = jnp.where(better, float(g), best)
        best_s = jnp.where(better, scores[g], best_s)
    vals, affs = [], []
    for j in range(epg):
        vj, aj = sel[j], aff[j]
        for g in range(1, N_GROUPS):
            in_g = best == float(g)
            vj = jnp.where(in_g, sel[g * epg + j], vj)
            aj = jnp.where(in_g, aff[g * epg + j], aj)
        vals.append(vj)
        affs.append(aj)

    def first_argmax(vs):
        idx, m, a = jnp.zeros_like(vs[0]), vs[0], affs[0]
        for j in range(1, epg):
            gt = vs[j] > m
            idx = jnp.where(gt, float(j), idx)
            m = jnp.where(gt, vs[j], m)
            a = jnp.where(gt, affs[j], a)
        return idx, a

    i1, a1 = first_argmax(vals)
    i2, a2 = first_argmax([jnp.where(i1 == float(j), neg, vals[j]) for j in range(epg)])
    tot = a1 + a2
    srow = lax.broadcasted_iota(jnp.int32, (8, tm), 0)
    packed = jnp.where(srow == 0, best * epg + i1,
                       jnp.where(srow == 1, best * epg + i2,
                                 jnp.where(srow == 2, a1 / tot, jnp.where(srow == 3, a2 / tot, 0.0))))
    packed_t = jnp.concatenate([packed, jnp.zeros((LANES - 8, tm), F32)], axis=0).T
    e_ref[...] = packed_t[:, 0:TOP_K].astype(jnp.int32)
    w_ref[...] = packed_t[:, TOP_K:2 * TOP_K]


def _post_mixer(x2, gates, layer, o_list, l_list, y_b, ada_l, woa, wob, wout, gain2, w_router_t, b_router_c):
    t, d = x2.shape
    tm = 256
    per_b = SEQ // tm
    const = lambda shape: pl.BlockSpec(shape, lambda i: (0,) * len(shape), pipeline_mode=pl.Buffered(1))
    wblk = lambda k: pl.BlockSpec((1, k, d), lambda i: (layer, 0, 0), pipeline_mode=pl.Buffered(1))
    rowblk = lambda w: pl.BlockSpec((tm, w), lambda i: (i, 0))
    return pl.pallas_call(
        _post_mixer_kernel,
        out_shape=(
            jax.ShapeDtypeStruct((t, d), F32),
            jax.ShapeDtypeStruct((t, d), F32),
            jax.ShapeDtypeStruct((t, TOP_K), jnp.int32),
            jax.ShapeDtypeStruct((t, TOP_K), F32),
        ),
        grid=(t // tm,),
        in_specs=[
            rowblk(d),
            pl.BlockSpec((tm, d), lambda i: (i, 0)),
            pl.BlockSpec((tm, d), lambda i: (i, 1)),
            rowblk(A_OUT), rowblk(A_OUT), rowblk(A_OUT),
            rowblk(LANES), rowblk(LANES), rowblk(LANES),
            rowblk(GLA_VAL),
            pl.BlockSpec((1, 6, d), lambda i: (i // per_b, 0, 0)),
            wblk(A_OUT), wblk(GLA_VAL), wblk(d),
            const((1, d)), const((LANES, d)), const((LANES, 1)),
        ],
        out_specs=(rowblk(d), rowblk(d), rowblk(TOP_K), rowblk(TOP_K)),
        compiler_params=_params(56),
        name="post_mixer",
    )(x2, gates, gates, *o_list, *l_list, y_b, ada_l, woa, wob, wout, gain2, w_router_t, b_router_c)


def _dispatch_kernel(dest_ref, cnt_ref, h_ref, x_hbm, sem, *, tm):
    i = pl.program_id(0)
    n_rows = x_hbm.shape[0]

    def copy(r, row):
        return pltpu.make_async_copy(h_ref.at[pl.ds(r, 1), :], x_hbm.at[pl.ds(row, 1), :], sem)

    def rows(start):
        def body(r, carry):
            for k in range(TOP_K):
                cp = copy(r, dest_ref[(i * tm + r) * TOP_K + k])
                if start:
                    cp.start(priority=k % 2)
                else:
                    cp.wait()
            return carry
        if start:
            for r in range(tm):
                body(r, 0)
        else:
            lax.fori_loop(0, tm, body, 0, unroll=4)

    def fill(lo, hi, start):
        def body(r, carry):
            cp = copy(0, r)
            if start:
                cp.start()
            else:
                cp.wait()
            return carry
        lax.fori_loop(lo, hi, body, 0)

    def pads(start):
        def per_expert(e, base):
            cnt = cnt_ref[e]
            padded = (cnt + (MOE_BLOCK - 1)) // MOE_BLOCK * MOE_BLOCK
            fill(base + cnt, base + padded, start)
            return base + padded
        end = lax.fori_loop(0, N_EXPERTS, per_expert, 0)
        fill(end, n_rows, start)

    rows(True)

    @pl.when(i == 0)
    def _():
        pads(True)
        pads(False)

    rows(False)


def _dispatch(dest, counts, h2, n_rows):
    t, d = h2.shape
    tm = 256
    grid_spec = pltpu.PrefetchScalarGridSpec(
        num_scalar_prefetch=2,
        grid=(t // tm,),
        in_specs=[pl.BlockSpec((tm, d), lambda i, dst, cnt: (i, 0))],
        out_specs=pl.BlockSpec(memory_space=pl.ANY),
        scratch_shapes=[pltpu.SemaphoreType.DMA(())],
    )
    return pl.pallas_call(
        functools.partial(_dispatch_kernel, tm=tm),
        out_shape=jax.ShapeDtypeStruct((n_rows, d), h2.dtype),
        grid_spec=grid_spec,
        compiler_params=_params(16, ("arbitrary",)),
        name="moe_dispatch",
    )(dest, counts, h2)


def _moe_kernel(be_ref, nu_ref, x_ref, wg_ref, wu_ref, wd_ref, y_ref):
    i = pl.program_id(0)
    n_used = nu_ref[0]

    @pl.when(i < n_used)
    def _():
        xb = x_ref[...].astype(BF16)
        g = jnp.dot(xb, wg_ref[0, 0], preferred_element_type=F32)
        u = jnp.dot(xb, wu_ref[0, 0], preferred_element_type=F32)
        a = (g * _sigmoid(g) * u).astype(BF16)
        y_ref[...] = jnp.dot(a, wd_ref[0, 0], preferred_element_type=F32)

    @pl.when(i >= n_used)
    def _():
        y_ref[...] = jnp.zeros_like(y_ref)


def _moe_experts(blk_e, n_used, xbuf, layer, wg, wu, wd):
    n_rows, d = xbuf.shape
    n_blk = blk_e.shape[0]
    f = wg.shape[3]
    grid_spec = pltpu.PrefetchScalarGridSpec(
        num_scalar_prefetch=2,
        grid=(n_blk,),
        in_specs=[
            pl.BlockSpec((MOE_BLOCK, d), lambda i, be, nu: (jnp.minimum(i, nu[0] - 1), 0)),
            pl.BlockSpec((1, 1, d, f), lambda i, be, nu: (layer, be[i], 0, 0)),
            pl.BlockSpec((1, 1, d, f), lambda i, be, nu: (layer, be[i], 0, 0)),
            pl.BlockSpec((1, 1, f, d), lambda i, be, nu: (layer, be[i], 0, 0)),
        ],
        out_specs=pl.BlockSpec((MOE_BLOCK, d), lambda i, be, nu: (i, 0)),
    )
    return pl.pallas_call(
        _moe_kernel,
        out_shape=jax.ShapeDtypeStruct((n_rows, d), F32),
        grid_spec=grid_spec,
        compiler_params=_params(48, ("arbitrary",)),
        name="moe_experts",
    )(blk_e, n_used, xbuf, wg, wu, wd)


def _pair_copy(src_hbm, row, dst, slot, k, r, sem):
    return pltpu.make_async_copy(src_hbm.at[pl.ds(row, 1), :], dst.at[slot, k, pl.ds(r, 1), :], sem.at[slot])


def _combine_kernel(pos_ref, y_hbm, x_ref, w_ref, ada_ref, gain_ref, adan_ref, wal_ref, *rest, tm, final):
    if final:
        o_ref, yg, sem = rest
    else:
        o_ref, h_ref, al_ref, yg, sem = rest
    i = pl.program_id(0)
    n = pl.num_programs(0)
    slot = i % 2

    def gather(tile, s, start, static_rows=False):
        def body(r, c):
            for k in range(TOP_K):
                cp = _pair_copy(y_hbm, pos_ref[(tile * tm + r) * TOP_K + k], yg, s, k, r, sem)
                if start:
                    cp.start(priority=k % 2)
                else:
                    cp.wait()
            return c
        if static_rows:
            for r in range(tm):
                body(r, 0)
        else:
            lax.fori_loop(0, tm, body, 0, unroll=8)

    @pl.when(i == 0)
    def _():
        gather(0, 0, True)

    @pl.when(i + 1 < n)
    def _():
        gather(i + 1, 1 - slot, True, static_rows=True)

    gather(i, slot, False)
    w = w_ref[...]
    y = w[:, 0:1] * yg[slot, 0] + w[:, 1:2] * yg[slot, 1]
    out = x_ref[...] + ada_ref[0, 5:6, :] * y
    normed = out * lax.rsqrt(jnp.mean(out * out, axis=-1, keepdims=True) + NORM_EPS) * gain_ref[...]
    if final:
        o_ref[...] = normed
    else:
        o_ref[...] = out
        h = (normed * (1.0 + adan_ref[0, 1:2, :]) + adan_ref[0, 0:1, :]).astype(BF16)
        h_ref[...] = h
        al_ref[...] = lax.dot_general(h, wal_ref[...], (((1,), (1,)), ((), ())), preferred_element_type=F32)


def _combine(pos, ybuf, x2, gw, ada_l, gain, ada_next, w_alow_next, final):
    t, d = x2.shape
    tm = 256
    per_b = SEQ // tm
    kern = functools.partial(_combine_kernel, tm=tm, final=final)
    rowblk = lambda w: pl.BlockSpec((tm, w), lambda i, p: (i, 0))
    out_shape = [jax.ShapeDtypeStruct((t, d), F32)]
    out_specs = [rowblk(d)]
    if not final:
        out_shape += [jax.ShapeDtypeStruct((t, d), BF16), jax.ShapeDtypeStruct((t, LANES), F32)]
        out_specs += [rowblk(d), rowblk(LANES)]
    grid_spec = pltpu.PrefetchScalarGridSpec(
        num_scalar_prefetch=1,
        grid=(t // tm,),
        in_specs=[
            pl.BlockSpec(memory_space=pl.ANY),
            rowblk(d),
            rowblk(TOP_K),
            pl.BlockSpec((1, 6, d), lambda i, p: (i // per_b, 0, 0)),
            pl.BlockSpec((1, d), lambda i, p: (0, 0)),
            pl.BlockSpec((1, 6, d), lambda i, p: (i // per_b, 0, 0)),
            pl.BlockSpec((LANES, d), lambda i, p: (0, 0)),
        ],
        out_specs=tuple(out_specs),
        scratch_shapes=[pltpu.VMEM((2, TOP_K, tm, d), F32), pltpu.SemaphoreType.DMA((2,))],
    )
    return pl.pallas_call(
        kern,
        out_shape=tuple(out_shape),
        grid_spec=grid_spec,
        compiler_params=_params(48, ("arbitrary",)),
        name="moe_combine_final" if final else "moe_combine",
    )(pos, ybuf, x2, gw, ada_l, gain, ada_next, w_alow_next)


def _plan_kernel(e_ref, dest_ref, cnt_ref, carry, pstart_s, *, tm):
    ph = pl.program_id(0)
    i = pl.program_id(1)
    lane = lax.broadcasted_iota(jnp.int32, (tm, LANES), 1)
    e = e_ref[...]
    oh0 = e[:, 0:1] == lane
    oh1 = e[:, 1:2] == lane
    oh = oh0.astype(F32) + oh1.astype(F32)
    tile_counts = jnp.sum(oh, axis=0, keepdims=True)

    @pl.when((ph == 0) & (i == 0))
    def _():
        carry[...] = jnp.zeros_like(carry)

    @pl.when(ph == 0)
    def _():
        carry[...] += tile_counts

    @pl.when((ph == 1) & (i == 0))
    def _():
        counts = carry[...]
        cnt_ref[...] = counts.astype(jnp.int32)
        nblk = jnp.floor((counts + (MOE_BLOCK - 1)) * (1.0 / MOE_BLOCK))
        r = lax.broadcasted_iota(jnp.int32, (LANES, LANES), 0)
        c = lax.broadcasted_iota(jnp.int32, (LANES, LANES), 1)
        upper = (r < c).astype(BF16)
        nb8 = jnp.broadcast_to(nblk, (8, LANES)).astype(BF16)
        pstart_s[...] = jnp.dot(nb8, upper, preferred_element_type=F32)[0:1, :] * float(MOE_BLOCK)
        carry[...] = jnp.zeros_like(carry)

    @pl.when(ph == 1)
    def _():
        r = lax.broadcasted_iota(jnp.int32, (tm, tm), 0)
        c = lax.broadcasted_iota(jnp.int32, (tm, tm), 1)
        lower = (r > c).astype(BF16)
        before = jnp.dot(lower, oh.astype(BF16), preferred_element_type=F32) + carry[...]
        base = before + pstart_s[...]
        d0 = jnp.sum(jnp.where(oh0, base, 0.0), axis=1, keepdims=True)
        d1 = jnp.sum(jnp.where(oh1, base, 0.0), axis=1, keepdims=True)
        col = lax.broadcasted_iota(jnp.int32, (tm, TOP_K), 1)
        dest_ref[...] = jnp.where(col == 0, d0, d1).astype(jnp.int32)
        carry[...] += tile_counts


def _dispatch_plan(eidx, n_blk):
    t = eidx.shape[0]
    tm = 512
    dest, counts = pl.pallas_call(
        functools.partial(_plan_kernel, tm=tm),
        out_shape=(jax.ShapeDtypeStruct((t, TOP_K), jnp.int32), jax.ShapeDtypeStruct((1, LANES), jnp.int32)),
        grid=(2, t // tm),
        in_specs=[pl.BlockSpec((tm, TOP_K), lambda p, i: (i, 0))],
        out_specs=(pl.BlockSpec((tm, TOP_K), lambda p, i: (i * p, 0)), pl.BlockSpec((1, LANES), lambda p, i: (0, 0))),
        scratch_shapes=[pltpu.VMEM((1, LANES), F32), pltpu.VMEM((1, LANES), F32)],
        compiler_params=_params(32, ("arbitrary", "arbitrary")),
        name="dispatch_plan",
    )(eidx)
    counts = counts[0, :N_EXPERTS]
    pcounts = (counts + MOE_BLOCK - 1) // MOE_BLOCK * MOE_BLOCK
    pend = jnp.cumsum(pcounts)
    blk_start = jnp.arange(n_blk, dtype=jnp.int32) * MOE_BLOCK
    blk_e = jnp.sum((pend[None, :] <= blk_start[:, None]).astype(jnp.int32), axis=1)
    blk_e = jnp.minimum(blk_e, N_EXPERTS - 1).astype(jnp.int32)
    n_used = (pend[-1] // MOE_BLOCK).astype(jnp.int32).reshape(1)
    return blk_e, n_used, dest.reshape(-1), counts


def _rope_tables():
    pos = jnp.arange(SEQ, dtype=F32)
    inv_freq = ROPE_THETA ** (-jnp.arange(0, HEAD_DIM, 2, dtype=F32) / HEAD_DIM)
    ang = pos[:, None] * inv_freq[None, :]
    cos, sin = jnp.cos(ang), jnp.sin(ang)
    return jnp.concatenate([cos, cos], axis=1), jnp.concatenate([-sin, sin], axis=1)


def kernel(x, c, w_ada, b_ada, norm_mix, norm_ffn, w_in, w_alpha2, b_alpha2, gla_gain, w_out_a, w_out_b, w_out,
           w_router, b_router, w_gate_e, w_up_e, w_down_e, final_norm):
    b, s, d = x.shape
    depth = w_ada.shape[0]
    t = b * s
    n_blk = (t * TOP_K) // MOE_BLOCK + N_EXPERTS
    cosf, sinf = _rope_tables()
    ada = _ada(c, w_ada, b_ada).reshape(depth, b, 6, d)
    w_router_t = jnp.pad(w_router.T, ((0, LANES - N_EXPERTS), (0, 0))).astype(BF16)
    b_router_c = jnp.pad(b_router, (0, LANES - N_EXPERTS)).reshape(LANES, 1)
    final_gain = final_norm.reshape(1, d)
    w_in_t = jnp.swapaxes(w_in, 1, 2)
    w_alow_all = jnp.pad(w_in_t[:, COL_ALOW:COL_GATES, :], ((0, 0), (0, LANES - GLA_LOWRANK), (0, 0))).astype(BF16)
    w_oa, w_ob, w_o = w_out_a.astype(BF16), w_out_b.astype(BF16), w_out.astype(BF16)
    w_g, w_u, w_d = w_gate_e.astype(BF16), w_up_e.astype(BF16), w_down_e.astype(BF16)
    x2 = x.reshape(t, d)
    h, a_low = _norm_mod(x2, ada[0], norm_mix[0].reshape(1, d), w_alow_all[0])
    for l in range(depth):
        w_alpha2p = jnp.pad(w_alpha2[l], ((0, LANES - GLA_LOWRANK), (0, 0))).astype(BF16)
        proj = _in_proj(h, w_in_t, l, 0, COL_ALOW, "in_proj")
        gates = _in_proj(h, w_in_t, l, COL_GATES, 2 * D_MODEL, "in_proj_gates")
        o_list, l_list = [], []
        for g in range(len(DILATED_GROUPS)):
            o_g, l_g = _attention(proj, cosf, sinf, g)
            o_list.append(o_g)
            l_list.append(l_g)
        y_b = _gla(proj, a_low, w_alpha2p, b_alpha2[l].reshape(1, GLA_KEY), gla_gain[l].reshape(1, GLA_VAL))
        x2, h2, eidx, gw = _post_mixer(
            x2, gates, l, o_list, l_list, y_b, ada[l], w_oa, w_ob, w_o,
            norm_ffn[l].reshape(1, d), w_router_t, b_router_c)
        blk_e, n_used, dest, counts = _dispatch_plan(eidx, n_blk)
        xbuf = _dispatch(dest, counts, h2, n_blk * MOE_BLOCK)
        ybuf = _moe_experts(blk_e, n_used, xbuf, l, w_g, w_u, w_d)
        if l == depth - 1:
            (x2,) = _combine(dest, ybuf, x2, gw, ada[l], final_gain, ada[l], w_alow_all[l], final=True)
        else:
            x2, h, a_low = _combine(dest, ybuf, x2, gw, ada[l], norm_mix[l + 1].reshape(1, d), ada[l + 1],
                                    w_alow_all[l + 1], final=False)
    return x2.reshape(b, s, d)
```

```python
import functools

import jax
import jax.numpy as jnp
from jax import lax
from jax.experimental import pallas as pl
from jax.experimental.pallas import tpu as pltpu

F32 = jnp.float32
BF16 = jnp.bfloat16

D_MODEL = 2048
SEQ = 2048
HEAD_DIM = 128
A_HEADS_PER_GROUP = 4
DILATED_GROUPS = ((128, 1), (512, 4), (2048, 16))
A_WIDTH = 12 * HEAD_DIM
A_OUT = A_HEADS_PER_GROUP * HEAD_DIM
ATT_BLOCK = 128
ROPE_THETA = 10000.0
GLA_HEADS = 4
GLA_DK = 128
GLA_DV = 256
GLA_KEY = GLA_HEADS * GLA_DK
GLA_VAL = GLA_HEADS * GLA_DV
GLA_LOWRANK = 16
GLA_TAU = 16.0
GLA_CHUNK = 64
N_EXPERTS = 16
N_GROUPS = 4
EXPERTS_PER_GROUP = N_EXPERTS // N_GROUPS
TOP_K = 2
D_FF_EXPERT = 1024
NORM_EPS = 1e-6

LANES = 128
MOE_BLOCK = 256
PROJ_TILE = 512

COL_QA = 0
COL_KA = A_WIDTH
COL_VA = 2 * A_WIDTH
COL_QB = 3 * A_WIDTH
COL_KB = COL_QB + GLA_KEY
COL_VB = COL_KB + GLA_KEY
COL_RB = COL_VB + GLA_VAL
COL_ALOW = COL_RB + GLA_VAL
COL_GATES = COL_ALOW + GLA_LOWRANK


def _params(vmem_mib, sem=None):
    kw = dict(vmem_limit_bytes=int(vmem_mib) << 20)
    if sem is not None:
        kw["dimension_semantics"] = sem
    return pltpu.CompilerParams(**kw)


def _sigmoid(x):
    return 0.5 * jnp.tanh(0.5 * x) + 0.5


def _ada_kernel(c_ref, w_ref, b_ref, o_ref):
    @pl.when(pl.program_id(1) == 0)
    def _():
        o_ref[0] = jnp.broadcast_to(b_ref[0], o_ref.shape[1:])

    c = c_ref[...]
    ca = (c * _sigmoid(c)).astype(BF16)
    o_ref[0] += jnp.dot(ca, w_ref[0].astype(BF16), preferred_element_type=F32)


def _ada(c, w_ada, b_ada):
    depth, d, n = w_ada.shape
    b = c.shape[0]
    tk = 256
    return pl.pallas_call(
        _ada_kernel,
        out_shape=jax.ShapeDtypeStruct((depth, b, n), F32),
        grid=(depth, d // tk),
        in_specs=[
            pl.BlockSpec((b, tk), lambda l, k: (0, k)),
            pl.BlockSpec((1, tk, n), lambda l, k: (l, k, 0)),
            pl.BlockSpec((1, 1, n), lambda l, k: (l, 0, 0)),
        ],
        out_specs=pl.BlockSpec((1, b, n), lambda l, k: (l, 0, 0)),
        compiler_params=_params(40, ("arbitrary", "arbitrary")),
        name="ada",
    )(c, w_ada, b_ada.reshape(depth, 1, n))


def _norm_mod_kernel(x_ref, ada_ref, gain_ref, wal_ref, h_ref, al_ref):
    x = x_ref[...]
    y = x * lax.rsqrt(jnp.mean(x * x, axis=-1, keepdims=True) + NORM_EPS) * gain_ref[...]
    h = (y * (1.0 + ada_ref[0, 1:2, :]) + ada_ref[0, 0:1, :]).astype(BF16)
    h_ref[...] = h
    al_ref[...] = lax.dot_general(h, wal_ref[...], (((1,), (1,)), ((), ())), preferred_element_type=F32)


def _norm_mod(x2, ada_l, gain, w_alow):
    t, d = x2.shape
    tm = 512
    per_b = SEQ // tm
    return pl.pallas_call(
        _norm_mod_kernel,
        out_shape=(jax.ShapeDtypeStruct((t, d), BF16), jax.ShapeDtypeStruct((t, LANES), F32)),
        grid=(t // tm,),
        in_specs=[
            pl.BlockSpec((tm, d), lambda i: (i, 0)),
            pl.BlockSpec((1, 6, d), lambda i: (i // per_b, 0, 0)),
            pl.BlockSpec((1, d), lambda i: (0, 0)),
            pl.BlockSpec((LANES, d), lambda i: (0, 0)),
        ],
        out_specs=(pl.BlockSpec((tm, d), lambda i: (i, 0)), pl.BlockSpec((tm, LANES), lambda i: (i, 0))),
        compiler_params=_params(40),
        name="norm_mod",
    )(x2, ada_l, gain, w_alow)


def _matmul_nt_kernel(a_ref, b_ref, *rest, n_side):
    side_in, o_ref, side_out = rest[:n_side], rest[n_side], rest[n_side + 1:]
    o_ref[...] = lax.dot_general(a_ref[...], b_ref[...].astype(BF16), (((1,), (1,)), ((), ())),
                                 preferred_element_type=F32).astype(o_ref.dtype)
    for s_ref, so_ref in zip(side_in, side_out):
        so_ref[...] = s_ref[...].astype(so_ref.dtype)


def _in_proj(h, w_t, layer, col0, n_cols, name, sides=()):
    t, d = h.shape
    tm, tn = 2048, PROJ_TILE
    sub = 8
    assert col0 % sub == 0 and tn % sub == 0
    w_spec = pl.BlockSpec((pl.Squeezed(), pl.Element(tn), pl.Element(d)),
                          lambda i, j: (layer, (col0 // sub + j * (tn // sub)) * sub, 0))
    n_j = n_cols // tn
    n_steps = (t // tm) * n_j
    bf16_rows = 16
    in_specs = [pl.BlockSpec((tm, d), lambda i, j: (i, 0)), w_spec]
    out_shape = [jax.ShapeDtypeStruct((t, n_cols), BF16)]
    out_specs = [pl.BlockSpec((tm, tn), lambda i, j: (i, j))]
    for s in sides:
        _, rows, cols = s.shape
        rb = -(-pl.cdiv(rows, n_steps) // bf16_rows) * bf16_rows
        last = pl.cdiv(rows, rb) - 1
        slab = lambda i, j, last=last: jnp.minimum(i * n_j + j, last)
        in_specs.append(pl.BlockSpec((pl.Squeezed(), rb, cols), lambda i, j, slab=slab: (layer, slab(i, j), 0)))
        out_shape.append(jax.ShapeDtypeStruct((rows, cols), BF16))
        out_specs.append(pl.BlockSpec((rb, cols), lambda i, j, slab=slab: (slab(i, j), 0)))
    return pl.pallas_call(
        functools.partial(_matmul_nt_kernel, n_side=len(sides)),
        out_shape=tuple(out_shape),
        grid=(t // tm, n_j),
        in_specs=in_specs,
        out_specs=tuple(out_specs),
        compiler_params=_params(56, ("arbitrary", "arbitrary")),
        name=name,
    )(h, w_t, *sides)


def _attn_kernel(q_ref, k_ref, v_ref, cos_ref, sin_ref, o_ref, lse_ref, q_s, k_s, v_s, o_s, l_s, *, dil, nb, wd):
    scale = HEAD_DIM ** -0.5
    blk = ATT_BLOCK
    nh = A_HEADS_PER_GROUP
    cosf = cos_ref[...]
    sinf = sin_ref[...]
    for h in range(nh):
        hc = slice(h * HEAD_DIM, (h + 1) * HEAD_DIM)
        tq = q_ref[:, hc].astype(F32)
        q_s[h] = (tq * cosf + pltpu.roll(tq, HEAD_DIM // 2, 1) * sinf) * scale
        tk = k_ref[:, hc].astype(F32)
        k_s[h] = tk * cosf + pltpu.roll(tk, HEAD_DIM // 2, 1) * sinf
        v_s[h] = v_ref[:, hc].astype(F32)

    def rows(start, size):
        return pl.ds(start, size) if dil == 1 else pl.ds(start, size, stride=dil)

    def head_mask(nk, offset):
        i = lax.broadcasted_iota(jnp.int32, (nh * blk, nk), 0) % blk
        j = lax.broadcasted_iota(jnp.int32, (nh * blk, nk), 1)
        dist = offset + i - j
        return (dist >= 0) & (dist <= wd)

    mask_first = head_mask(blk, 0)
    mask_later = head_mask(2 * blk, blk)
    lane = lax.broadcasted_iota(jnp.int32, (blk, LANES), 1)
    ones = jnp.ones((2 * blk, HEAD_DIM), BF16)

    def attend(qs, ks, nk, mask):
        s = jnp.concatenate(
            [lax.dot_general(q_s[h, rows(qs, blk), :].astype(BF16), k_s[h, rows(ks, nk), :].astype(BF16),
                             (((1,), (1,)), ((), ())), preferred_element_type=F32) for h in range(nh)], axis=0)
        s = jnp.where(mask, s, -jnp.inf)
        m = jnp.max(s, axis=-1, keepdims=True)
        p = jnp.exp(s - m).astype(BF16)
        lse_tile = jnp.zeros((blk, LANES), F32)
        for h in range(nh):
            v_aug = jnp.concatenate([v_s[h, rows(ks, nk), :].astype(BF16), ones[:nk]], axis=1)
            ov = jnp.dot(p[h * blk:(h + 1) * blk], v_aug, preferred_element_type=F32)
            den = ov[:, HEAD_DIM:]
            o_s[h, rows(qs, blk), :] = ov[:, :HEAD_DIM] / den
            lse_tile = jnp.where(lane == h, m[h * blk:(h + 1) * blk] + jnp.log(den), lse_tile)
        l_s[rows(qs, blk), :] = lse_tile

    def per_phase(ph, carry):
        attend(ph, ph, blk, mask_first)

        def per_blk(n, c):
            qs = ph + dil * blk * n
            ks = ph + dil * blk * (n - 1)
            if dil == 1:
                qs = pl.multiple_of(qs, blk)
                ks = pl.multiple_of(ks, blk)
            attend(qs, ks, 2 * blk, mask_later)
            return c

        if nb > 1:
            lax.fori_loop(1, nb, per_blk, 0, unroll=3)
        return carry

    lax.fori_loop(0, dil, per_phase, 0, unroll=4 if nb == 1 else 1)
    for h in range(nh):
        o_ref[:, h * HEAD_DIM:(h + 1) * HEAD_DIM] = o_s[h].astype(o_ref.dtype)
    lse_ref[...] = l_s[...]


def _attention(proj, cosf, sinf, group):
    window, dil = DILATED_GROUPS[group]
    t = proj.shape[0]
    nb = (SEQ // dil) // ATT_BLOCK
    col = lambda base: base // A_OUT + group
    kern = functools.partial(_attn_kernel, dil=dil, nb=nb, wd=window // dil)
    return pl.pallas_call(
        kern,
        out_shape=(jax.ShapeDtypeStruct((t, A_OUT), BF16), jax.ShapeDtypeStruct((t, LANES), F32)),
        grid=(t // SEQ,),
        in_specs=[
            pl.BlockSpec((SEQ, A_OUT), lambda b: (b, col(COL_QA))),
            pl.BlockSpec((SEQ, A_OUT), lambda b: (b, col(COL_KA))),
            pl.BlockSpec((SEQ, A_OUT), lambda b: (b, col(COL_VA))),
            pl.BlockSpec((SEQ, HEAD_DIM), lambda b: (0, 0)),
            pl.BlockSpec((SEQ, HEAD_DIM), lambda b: (0, 0)),
        ],
        out_specs=(pl.BlockSpec((SEQ, A_OUT), lambda b: (b, 0)), pl.BlockSpec((SEQ, LANES), lambda b: (b, 0))),
        scratch_shapes=[pltpu.VMEM((A_HEADS_PER_GROUP, SEQ, HEAD_DIM), F32) for _ in range(4)]
        + [pltpu.VMEM((SEQ, LANES), F32)],
        compiler_params=_params(48),
        name=f"dilated_attn_g{group}",
    )(proj, proj, proj, cosf, sinf)


def _gla_kernel(q_ref, k_ref, v0_ref, v1_ref, r0_ref, r1_ref, al_ref, wa_ref, ba_ref, gain_ref, y_ref, st_ref, *, rows):
    @pl.when(pl.program_id(1) == 0)
    def _():
        st_ref[...] = jnp.zeros_like(st_ref)

    ck = GLA_CHUNK
    nseq = q_ref.shape[0]
    half = GLA_HEADS // 2
    ri = lax.broadcasted_iota(jnp.int32, (rows, rows), 0)
    ci = lax.broadcasted_iota(jnp.int32, (rows, rows), 1)
    same = (ri // ck) == (ci // ck)
    tri = (same & (ri >= ci)).astype(BF16)
    tot = same.astype(BF16)

    def prepare(sq):
        z = jnp.dot(al_ref[sq].astype(BF16), wa_ref[...], preferred_element_type=F32) + ba_ref[...]
        log_a = (jnp.minimum(z, 0.0) - jnp.log(1.0 + jnp.exp(-jnp.abs(z)))) * (1.0 / GLA_TAU)
        la_hi = log_a.astype(BF16)
        la_lo = (log_a - la_hi.astype(F32)).astype(BF16)
        b = jnp.dot(tri, la_hi, preferred_element_type=F32) + jnp.dot(tri, la_lo, preferred_element_type=F32)
        b_end = jnp.dot(tot, la_hi, preferred_element_type=F32) + jnp.dot(tot, la_lo, preferred_element_type=F32)
        q = q_ref[sq].astype(F32) * (GLA_DK ** -0.5)
        k = k_ref[sq].astype(F32)
        return ((q * jnp.exp(b)).astype(BF16), (k * jnp.exp(-b)).astype(BF16),
                (k * jnp.exp(b_end - b)).astype(BF16), jnp.exp(b_end))

    prepared = [prepare(sq) for sq in range(nseq)]
    r1 = lax.broadcasted_iota(jnp.int32, (ck, ck), 0)
    c1 = lax.broadcasted_iota(jnp.int32, (ck, ck), 1)
    causal = r1 >= c1
    nt = (((1,), (1,)), ((), ()))
    v_refs = (v0_ref, v1_ref)
    r_refs = (r0_ref, r1_ref)
    for c in range(rows // ck):
        rs = slice(c * ck, (c + 1) * ck)
        for sq in range(nseq):
            q_dec, k_inv, k_end, decay = prepared[sq]
            for h in range(GLA_HEADS):
                hk = slice(h * GLA_DK, (h + 1) * GLA_DK)
                hv = slice(h * GLA_DV, (h + 1) * GLA_DV)
                hb = slice((h % half) * GLA_DV, (h % half + 1) * GLA_DV)
                v = v_refs[h // half][sq, rs, hb]
                attn = lax.dot_general(q_dec[rs, hk], k_inv[rs, hk], nt, preferred_element_type=F32)
                attn = jnp.where(causal, attn, 0.0).astype(BF16)
                st = st_ref[sq * GLA_HEADS + h]
                o = jnp.dot(attn, v, preferred_element_type=F32)
                o = o + lax.dot_general(q_dec[rs, hk], st.astype(BF16), nt, preferred_element_type=F32)
                v_t = v.astype(F32).T.astype(BF16)
                st_ref[sq * GLA_HEADS + h] = (st * decay[c * ck:c * ck + 1, hk]
                                              + jnp.dot(v_t, k_end[rs, hk], preferred_element_type=F32))
                o = o * lax.rsqrt(jnp.mean(o * o, axis=-1, keepdims=True) + NORM_EPS) * gain_ref[:, hv]
                r = r_refs[h // half][sq, rs, hb].astype(F32)
                y_ref[sq, rs, hv] = (o * (r * _sigmoid(r))).astype(y_ref.dtype)


def _gla(proj, a_low, w_alpha2p, b_alpha2, gla_gain):
    t, width = proj.shape
    nb = t // SEQ
    rows = 256
    nseq = 2 if nb % 2 == 0 else 1
    kern = functools.partial(_gla_kernel, rows=rows)
    col = lambda base, j=0: (lambda b, s: (b, s, base // GLA_KEY + j))
    blk = lambda w: (nseq, rows, w)
    proj3 = proj.reshape(nb, SEQ, width)
    y = pl.pallas_call(
        kern,
        out_shape=jax.ShapeDtypeStruct((nb, SEQ, GLA_VAL), BF16),
        grid=(nb // nseq, SEQ // rows),
        in_specs=[
            pl.BlockSpec(blk(GLA_KEY), col(COL_QB)),
            pl.BlockSpec(blk(GLA_KEY), col(COL_KB)),
            pl.BlockSpec(blk(GLA_KEY), col(COL_VB, 0)),
            pl.BlockSpec(blk(GLA_KEY), col(COL_VB, 1)),
            pl.BlockSpec(blk(GLA_KEY), col(COL_RB, 0)),
            pl.BlockSpec(blk(GLA_KEY), col(COL_RB, 1)),
            pl.BlockSpec(blk(LANES), lambda b, s: (b, s, 0)),
            pl.BlockSpec((LANES, GLA_KEY), lambda b, s: (0, 0)),
            pl.BlockSpec((1, GLA_KEY), lambda b, s: (0, 0)),
            pl.BlockSpec((1, GLA_VAL), lambda b, s: (0, 0)),
        ],
        out_specs=pl.BlockSpec(blk(GLA_VAL), lambda b, s: (b, s, 0)),
        scratch_shapes=[pltpu.VMEM((nseq * GLA_HEADS, GLA_DV, GLA_DK), F32)],
        compiler_params=_params(32),
        name="gla",
    )(proj3, proj3, proj3, proj3, proj3, proj3, a_low.reshape(nb, SEQ, LANES), w_alpha2p, b_alpha2, gla_gain)
    return y.reshape(t, GLA_VAL)


def _post_mixer_kernel(x_ref, ga_ref, gb_ref, o0_ref, o1_ref, o2_ref, l0_ref, l1_ref, l2_ref, yb_ref, ada_ref,
                       woa_ref, wob_ref, wout_ref, g2_ref, wr_ref, br_ref,
                       xo_ref, h2_ref, e_ref, w_ref):
    ls = [l0_ref[...], l1_ref[...], l2_ref[...]]
    os_ = [o0_ref, o1_ref, o2_ref]
    mx = jnp.maximum(jnp.maximum(ls[0], ls[1]), ls[2])
    es = [jnp.exp(l - mx) for l in ls]
    zs = es[0] + es[1] + es[2]
    wt = [e / zs for e in es]
    parts = []
    for h in range(A_HEADS_PER_GROUP):
        hc = slice(h * HEAD_DIM, (h + 1) * HEAD_DIM)
        acc = wt[0][:, h:h + 1] * os_[0][:, hc].astype(F32)
        acc = acc + wt[1][:, h:h + 1] * os_[1][:, hc].astype(F32)
        acc = acc + wt[2][:, h:h + 1] * os_[2][:, hc].astype(F32)
        parts.append(acc)
    ya = jnp.concatenate(parts, axis=1).astype(BF16)
    ta = jnp.dot(ya, woa_ref[0], preferred_element_type=F32)
    tb = jnp.dot(yb_ref[...], wob_ref[0], preferred_element_type=F32)
    merged = _sigmoid(ga_ref[...].astype(F32)) * ta + _sigmoid(gb_ref[...].astype(F32)) * tb
    mix = jnp.dot(merged.astype(BF16), wout_ref[0], preferred_element_type=F32)
    xn = x_ref[...] + ada_ref[0, 2:3, :] * mix
    xo_ref[...] = xn
    y = xn * lax.rsqrt(jnp.mean(xn * xn, axis=-1, keepdims=True) + NORM_EPS) * g2_ref[...]
    h2 = y * (1.0 + ada_ref[0, 4:5, :]) + ada_ref[0, 3:4, :]
    h2_ref[...] = h2

    logits = lax.dot_general(wr_ref[...], h2.astype(BF16), (((1,), (1,)), ((), ())), preferred_element_type=F32)
    tm = logits.shape[1]
    aff_all = _sigmoid(logits)
    sel_all = aff_all + br_ref[...]
    sel = [sel_all[e:e + 1, :] for e in range(N_EXPERTS)]
    aff = [aff_all[e:e + 1, :] for e in range(N_EXPERTS)]
    neg = -jnp.inf

    def top2_sum(v0, v1, v2, v3):
        hi01, lo01 = jnp.maximum(v0, v1), jnp.minimum(v0, v1)
        hi23, lo23 = jnp.maximum(v2, v3), jnp.minimum(v2, v3)
        return jnp.maximum(hi01, hi23) + jnp.maximum(jnp.minimum(hi01, hi23), jnp.maximum(lo01, lo23))

    epg = EXPERTS_PER_GROUP
    scores = [top2_sum(*sel[g * epg:(g + 1) * epg]) for g in range(N_GROUPS)]
    best = jnp.zeros_like(scores[0])
    best_s = scores[0]
    for g in range(1, N_GROUPS):
        better = scores[g] > best_s
        best = jnp.where(better, float(g), best)
        best_s = jnp.where(better, scores[g], best_s)
    vals, affs = [], []
    for j in range(epg):
        vj, aj = sel[j], aff[j]
        for g in range(1, N_GROUPS):
            in_g = best == float(g)
            vj = jnp.where(in_g, sel[g * epg + j], vj)
            aj = jnp.where(in_g, aff[g * epg + j], aj)
        vals.append(vj)
        affs.append(aj)

    def first_argmax(vs):
        idx, m, a = jnp.zeros_like(vs[0]), vs[0], affs[0]
        for j in range(1, epg):
            gt = vs[j] > m
            idx = jnp.where(gt, float(j), idx)
            m = jnp.where(gt, vs[j], m)
            a = jnp.where(gt, affs[j], a)
        return idx, a

    i1, a1 = first_argmax(vals)
    i2, a2 = first_argmax([jnp.where(i1 == float(j), neg, vals[j]) for j in range(epg)])
    tot = a1 + a2
    srow = lax.broadcasted_iota(jnp.int32, (8, tm), 0)
    packed = jnp.where(srow == 0, best * epg + i1,
                       jnp.where(srow == 1, best * epg + i2,
                                 jnp.where(srow == 2, a1 / tot, jnp.where(srow == 3, a2 / tot, 0.0))))
    packed_t = jnp.concatenate([packed, jnp.zeros((LANES - 8, tm), F32)], axis=0).T
    e_ref[...] = packed_t[:, 0:TOP_K].astype(jnp.int32)
    w_ref[...] = packed_t[:, TOP_K:2 * TOP_K]


def _post_mixer(x2, gates, layer, o_list, l_list, y_b, ada_l, woa, wob, wout, gain2, w_router_t, b_router_c):
    t, d = x2.shape
    tm = 256
    per_b = SEQ // tm
    const = lambda shape: pl.BlockSpec(shape, lambda i: (0,) * len(shape), pipeline_mode=pl.Buffered(1))
    wblk = lambda k: pl.BlockSpec((1, k, d), lambda i: (layer, 0, 0), pipeline_mode=pl.Buffered(1))
    rowblk = lambda w: pl.BlockSpec((tm, w), lambda i: (i, 0))
    return pl.pallas_call(
        _post_mixer_kernel,
        out_shape=(
            jax.ShapeDtypeStruct((t, d), F32),
            jax.ShapeDtypeStruct((t, d), F32),
            jax.ShapeDtypeStruct((t, TOP_K), jnp.int32),
            jax.ShapeDtypeStruct((t, TOP_K), F32),
        ),
        grid=(t // tm,),
        in_specs=[
            rowblk(d),
            pl.BlockSpec((tm, d), lambda i: (i, 0)),
            pl.BlockSpec((tm, d), lambda i: (i, 1)),
            rowblk(A_OUT), rowblk(A_OUT), rowblk(A_OUT),
            rowblk(LANES), rowblk(LANES), rowblk(LANES),
            rowblk(GLA_VAL),
            pl.BlockSpec((1, 6, d), lambda i: (i // per_b, 0, 0)),
            wblk(A_OUT), wblk(GLA_VAL), wblk(d),
            const((1, d)), const((LANES, d)), const((LANES, 1)),
        ],
        out_specs=(rowblk(d), rowblk(d), rowblk(TOP_K), rowblk(TOP_K)),
        compiler_params=_params(56),
        name="post_mixer",
    )(x2, gates, gates, *o_list, *l_list, y_b, ada_l, woa, wob, wout, gain2, w_router_t, b_router_c)


def _dispatch_kernel(dest_ref, cnt_ref, h_ref, x_hbm, sem, *, tm):
    i = pl.program_id(0)
    n_rows = x_hbm.shape[0]

    def copy(r, row):
        return pltpu.make_async_copy(h_ref.at[pl.ds(r, 1), :], x_hbm.at[pl.ds(row, 1), :], sem)

    def rows(start):
        def body(r, carry):
            for k in range(TOP_K):
                cp = copy(r, dest_ref[(i * tm + r) * TOP_K + k])
                if start:
                    cp.start(priority=k % 2)
                else:
                    cp.wait()
            return carry
        if start:
            for r in range(tm):
                body(r, 0)
        else:
            lax.fori_loop(0, tm, body, 0, unroll=4)

    def fill(lo, hi, start):
        def body(r, carry):
            cp = copy(0, r)
            if start:
                cp.start()
            else:
                cp.wait()
            return carry
        lax.fori_loop(lo, hi, body, 0)

    def pads(start):
        def per_expert(e, base):
            cnt = cnt_ref[e]
            padded = (cnt + (MOE_BLOCK - 1)) // MOE_BLOCK * MOE_BLOCK
            fill(base + cnt, base + padded, start)
            return base + padded
        end = lax.fori_loop(0, N_EXPERTS, per_expert, 0)
        fill(end, n_rows, start)

    rows(True)

    @pl.when(i == 0)
    def _():
        pads(True)
        pads(False)

    rows(False)


def _dispatch(dest, counts, h2, n_rows):
    t, d = h2.shape
    tm = 256
    grid_spec = pltpu.PrefetchScalarGridSpec(
        num_scalar_prefetch=2,
        grid=(t // tm,),
        in_specs=[pl.BlockSpec((tm, d), lambda i, dst, cnt: (i, 0))],
        out_specs=pl.BlockSpec(memory_space=pl.ANY),
        scratch_shapes=[pltpu.SemaphoreType.DMA(())],
    )
    return pl.pallas_call(
        functools.partial(_dispatch_kernel, tm=tm),
        out_shape=jax.ShapeDtypeStruct((n_rows, d), h2.dtype),
        grid_spec=grid_spec,
        compiler_params=_params(16, ("arbitrary",)),
        name="moe_dispatch",
    )(dest, counts, h2)


def _moe_kernel(be_ref, nu_ref, x_ref, wg_ref, wu_ref, wd_ref, y_ref):
    i = pl.program_id(0)
    n_used = nu_ref[0]

    @pl.when(i < n_used)
    def _():
        xb = x_ref[...].astype(BF16)
        g = jnp.dot(xb, wg_ref[0], preferred_element_type=F32)
        u = jnp.dot(xb, wu_ref[0], preferred_element_type=F32)
        a = (g * _sigmoid(g) * u).astype(BF16)
        y_ref[...] = jnp.dot(a, wd_ref[0], preferred_element_type=F32)

    @pl.when(i >= n_used)
    def _():
        y_ref[...] = jnp.zeros_like(y_ref)


def _moe_experts(blk_e, n_used, xbuf, wg, wu, wd):
    n_rows, d = xbuf.shape
    n_blk = blk_e.shape[0]
    f = wg.shape[2]
    grid_spec = pltpu.PrefetchScalarGridSpec(
        num_scalar_prefetch=2,
        grid=(n_blk,),
        in_specs=[
            pl.BlockSpec((MOE_BLOCK, d), lambda i, be, nu: (jnp.minimum(i, nu[0] - 1), 0)),
            pl.BlockSpec((1, d, f), lambda i, be, nu: (be[i], 0, 0)),
            pl.BlockSpec((1, d, f), lambda i, be, nu: (be[i], 0, 0)),
            pl.BlockSpec((1, f, d), lambda i, be, nu: (be[i], 0, 0)),
        ],
        out_specs=pl.BlockSpec((MOE_BLOCK, d), lambda i, be, nu: (i, 0)),
    )
    return pl.pallas_call(
        _moe_kernel,
        out_shape=jax.ShapeDtypeStruct((n_rows, d), F32),
        grid_spec=grid_spec,
        compiler_params=_params(48, ("arbitrary",)),
        name="moe_experts",
    )(blk_e, n_used, xbuf, wg, wu, wd)


def _pair_copy(src_hbm, row, dst, slot, k, r, sem):
    return pltpu.make_async_copy(src_hbm.at[pl.ds(row, 1), :], dst.at[slot, k, pl.ds(r, 1), :], sem.at[slot])


def _combine_kernel(pos_ref, y_hbm, x_ref, w_ref, ada_ref, gain_ref, adan_ref, wal_ref, *rest, tm, final):
    if final:
        o_ref, yg, sem = rest
    else:
        o_ref, h_ref, al_ref, yg, sem = rest
    i = pl.program_id(0)
    n = pl.num_programs(0)
    slot = i % 2

    def gather(tile, s, start, static_rows=False):
        def body(r, c):
            for k in range(TOP_K):
                cp = _pair_copy(y_hbm, pos_ref[(tile * tm + r) * TOP_K + k], yg, s, k, r, sem)
                if start:
                    cp.start(priority=k % 2)
                else:
                    cp.wait()
            return c
        if static_rows:
            for r in range(tm):
                body(r, 0)
        else:
            lax.fori_loop(0, tm, body, 0, unroll=8)

    @pl.when(i == 0)
    def _():
        gather(0, 0, True)

    @pl.when(i + 1 < n)
    def _():
        gather(i + 1, 1 - slot, True, static_rows=True)

    gather(i, slot, False)
    w = w_ref[...]
    y = w[:, 0:1] * yg[slot, 0] + w[:, 1:2] * yg[slot, 1]
    out = x_ref[...] + ada_ref[0, 5:6, :] * y
    normed = out * lax.rsqrt(jnp.mean(out * out, axis=-1, keepdims=True) + NORM_EPS) * gain_ref[...]
    if final:
        o_ref[...] = normed
    else:
        o_ref[...] = out
        h = (normed * (1.0 + adan_ref[0, 1:2, :]) + adan_ref[0, 0:1, :]).astype(BF16)
        h_ref[...] = h
        al_ref[...] = lax.dot_general(h, wal_ref[...], (((1,), (1,)), ((), ())), preferred_element_type=F32)


def _combine(pos, ybuf, x2, gw, ada_l, gain, ada_next, w_alow_next, final):
    t, d = x2.shape
    tm = 256
    per_b = SEQ // tm
    kern = functools.partial(_combine_kernel, tm=tm, final=final)
    rowblk = lambda w: pl.BlockSpec((tm, w), lambda i, p: (i, 0))
    out_shape = [jax.ShapeDtypeStruct((t, d), F32)]
    out_specs = [rowblk(d)]
    if not final:
        out_shape += [jax.ShapeDtypeStruct((t, d), BF16), jax.ShapeDtypeStruct((t, LANES), F32)]
        out_specs += [rowblk(d), rowblk(LANES)]
    grid_spec = pltpu.PrefetchScalarGridSpec(
        num_scalar_prefetch=1,
        grid=(t // tm,),
        in_specs=[
            pl.BlockSpec(memory_space=pl.ANY),
            rowblk(d),
            rowblk(TOP_K),
            pl.BlockSpec((1, 6, d), lambda i, p: (i // per_b, 0, 0)),
            pl.BlockSpec((1, d), lambda i, p: (0, 0)),
            pl.BlockSpec((1, 6, d), lambda i, p: (i // per_b, 0, 0)),
            pl.BlockSpec((LANES, d), lambda i, p: (0, 0)),
        ],
        out_specs=tuple(out_specs),
        scratch_shapes=[pltpu.VMEM((2, TOP_K, tm, d), F32), pltpu.SemaphoreType.DMA((2,))],
    )
    return pl.pallas_call(
        kern,
        out_shape=tuple(out_shape),
        grid_spec=grid_spec,
        compiler_params=_params(48, ("arbitrary",)),
        name="moe_combine_final" if final else "moe_combine",
    )(pos, ybuf, x2, gw, ada_l, gain, ada_next, w_alow_next)


def _plan_kernel(e_ref, dest_ref, cnt_ref, carry, pstart_s, *, tm):
    ph = pl.program_id(0)
    i = pl.program_id(1)
    lane = lax.broadcasted_iota(jnp.int32, (tm, LANES), 1)
    e = e_ref[...]
    oh0 = e[:, 0:1] == lane
    oh1 = e[:, 1:2] == lane
    oh = oh0.astype(F32) + oh1.astype(F32)
    tile_counts = jnp.sum(oh, axis=0, keepdims=True)

    @pl.when((ph == 0) & (i == 0))
    def _():
        carry[...] = jnp.zeros_like(carry)

    @pl.when(ph == 0)
    def _():
        carry[...] += tile_counts

    @pl.when((ph == 1) & (i == 0))
    def _():
        counts = carry[...]
        cnt_ref[...] = counts.astype(jnp.int32)
        nblk = jnp.floor((counts + (MOE_BLOCK - 1)) * (1.0 / MOE_BLOCK))
        r = lax.broadcasted_iota(jnp.int32, (LANES, LANES), 0)
        c = lax.broadcasted_iota(jnp.int32, (LANES, LANES), 1)
        upper = (r < c).astype(BF16)
        nb8 = jnp.broadcast_to(nblk, (8, LANES)).astype(BF16)
        pstart_s[...] = jnp.dot(nb8, upper, preferred_element_type=F32)[0:1, :] * float(MOE_BLOCK)
        carry[...] = jnp.zeros_like(carry)

    @pl.when(ph == 1)
    def _():
        r = lax.broadcasted_iota(jnp.int32, (tm, tm), 0)
        c = lax.broadcasted_iota(jnp.int32, (tm, tm), 1)
        lower = (r > c).astype(BF16)
        before = jnp.dot(lower, oh.astype(BF16), preferred_element_type=F32) + carry[...]
        base = before + pstart_s[...]
        d0 = jnp.sum(jnp.where(oh0, base, 0.0), axis=1, keepdims=True)
        d1 = jnp.sum(jnp.where(oh1, base, 0.0), axis=1, keepdims=True)
        col = lax.broadcasted_iota(jnp.int32, (tm, TOP_K), 1)
        dest_ref[...] = jnp.where(col == 0, d0, d1).astype(jnp.int32)
        carry[...] += tile_counts


def _dispatch_plan(eidx, n_blk):
    t = eidx.shape[0]
    tm = 512
    dest, counts = pl.pallas_call(
        functools.partial(_plan_kernel, tm=tm),
        out_shape=(jax.ShapeDtypeStruct((t, TOP_K), jnp.int32), jax.ShapeDtypeStruct((1, LANES), jnp.int32)),
        grid=(2, t // tm),
        in_specs=[pl.BlockSpec((tm, TOP_K), lambda p, i: (i, 0))],
        out_specs=(pl.BlockSpec((tm, TOP_K), lambda p, i: (i * p, 0)), pl.BlockSpec((1, LANES), lambda p, i: (0, 0))),
        scratch_shapes=[pltpu.VMEM((1, LANES), F32), pltpu.VMEM((1, LANES), F32)],
        compiler_params=_params(32, ("arbitrary", "arbitrary")),
        name="dispatch_plan",
    )(eidx)
    counts = counts[0, :N_EXPERTS]
    pcounts = (counts + MOE_BLOCK - 1) // MOE_BLOCK * MOE_BLOCK
    pend = jnp.cumsum(pcounts)
    blk_start = jnp.arange(n_blk, dtype=jnp.int32) * MOE_BLOCK
    blk_e = jnp.sum((pend[None, :] <= blk_start[:, None]).astype(jnp.int32), axis=1)
    blk_e = jnp.minimum(blk_e, N_EXPERTS - 1).astype(jnp.int32)
    n_used = (pend[-1] // MOE_BLOCK).astype(jnp.int32).reshape(1)
    return blk_e, n_used, dest.reshape(-1), counts


def _rope_tables():
    pos = jnp.arange(SEQ, dtype=F32)
    inv_freq = ROPE_THETA ** (-jnp.arange(0, HEAD_DIM, 2, dtype=F32) / HEAD_DIM)
    ang = pos[:, None] * inv_freq[None, :]
    cos, sin = jnp.cos(ang), jnp.sin(ang)
    return jnp.concatenate([cos, cos], axis=1), jnp.concatenate([-sin, sin], axis=1)


def kernel(x, c, w_ada, b_ada, norm_mix, norm_ffn, w_in, w_alpha2, b_alpha2, gla_gain, w_out_a, w_out_b, w_out,
           w_router, b_router, w_gate_e, w_up_e, w_down_e, final_norm):
    b, s, d = x.shape
    depth = w_ada.shape[0]
    t = b * s
    n_blk = (t * TOP_K) // MOE_BLOCK + N_EXPERTS
    cosf, sinf = _rope_tables()
    ada = _ada(c, w_ada, b_ada).reshape(depth, b, 6, d)
    w_router_t = jnp.pad(w_router.T, ((0, LANES - N_EXPERTS), (0, 0))).astype(BF16)
    b_router_c = jnp.pad(b_router, (0, LANES - N_EXPERTS)).reshape(LANES, 1)
    final_gain = final_norm.reshape(1, d)
    w_in_t = jnp.swapaxes(w_in, 1, 2)
    w_alow_all = jnp.pad(w_in_t[:, COL_ALOW:COL_GATES, :], ((0, 0), (0, LANES - GLA_LOWRANK), (0, 0))).astype(BF16)
    w_oa, w_ob, w_o = w_out_a.astype(BF16), w_out_b.astype(BF16), w_out.astype(BF16)
    n_exp, _, d_ff = w_gate_e.shape[1:]
    wg_rows = w_gate_e.reshape(depth, n_exp * d, d_ff)
    wu_rows = w_up_e.reshape(depth, n_exp * d, d_ff)
    wd_rows = w_down_e.reshape(depth, n_exp * d_ff, d)
    x2 = x.reshape(t, d)
    h, a_low = _norm_mod(x2, ada[0], norm_mix[0].reshape(1, d), w_alow_all[0])
    for l in range(depth):
        w_alpha2p = jnp.pad(w_alpha2[l], ((0, LANES - GLA_LOWRANK), (0, 0))).astype(BF16)
        proj, w_g, w_u = _in_proj(h, w_in_t, l, 0, COL_ALOW, "in_proj", sides=(wg_rows, wu_rows))
        gates, w_d = _in_proj(h, w_in_t, l, COL_GATES, 2 * D_MODEL, "in_proj_gates", sides=(wd_rows,))
        w_g, w_u, w_d = (w_g.reshape(n_exp, d, d_ff), w_u.reshape(n_exp, d, d_ff), w_d.reshape(n_exp, d_ff, d))
        o_list, l_list = [], []
        for g in range(len(DILATED_GROUPS)):
            o_g, l_g = _attention(proj, cosf, sinf, g)
            o_list.append(o_g)
            l_list.append(l_g)
        y_b = _gla(proj, a_low, w_alpha2p, b_alpha2[l].reshape(1, GLA_KEY), gla_gain[l].reshape(1, GLA_VAL))
        x2, h2, eidx, gw = _post_mixer(
            x2, gates, l, o_list, l_list, y_b, ada[l], w_oa, w_ob, w_o,
            norm_ffn[l].reshape(1, d), w_router_t, b_router_c)
        blk_e, n_used, dest, counts = _dispatch_plan(eidx, n_blk)
        xbuf = _dispatch(dest, counts, h2, n_blk * MOE_BLOCK)
        ybuf = _moe_experts(blk_e, n_used, xbuf, w_g, w_u, w_d)
        if l == depth - 1:
            (x2,) = _combine(dest, ybuf, x2, gw, ada[l], final_gain, ada[l], w_alow_all[l], final=True)
        else:
            x2, h, a_low = _combine(dest, ybuf, x2, gw, ada[l], norm_mix[l + 1].reshape(1, d), ada[l + 1],
                                    w_alow_all[l + 1], final=False)
    return x2.reshape(b, s, d)
```

```python
import functools

import jax
import jax.numpy as jnp
from jax import lax
from jax.experimental import pallas as pl
from jax.experimental.pallas import tpu as pltpu

F32 = jnp.float32
BF16 = jnp.bfloat16

D_MODEL = 2048
SEQ = 2048
HEAD_DIM = 128
A_HEADS_PER_GROUP = 4
DILATED_GROUPS = ((128, 1), (512, 4), (2048, 16))
A_WIDTH = 12 * HEAD_DIM
A_OUT = A_HEADS_PER_GROUP * HEAD_DIM
ATT_BLOCK = 128
ROPE_THETA = 10000.0
GLA_HEADS = 4
GLA_DK = 128
GLA_DV = 256
GLA_KEY = GLA_HEADS * GLA_DK
GLA_VAL = GLA_HEADS * GLA_DV
GLA_LOWRANK = 16
GLA_TAU = 16.0
GLA_CHUNK = 64
N_EXPERTS = 16
N_GROUPS = 4
EXPERTS_PER_GROUP = N_EXPERTS // N_GROUPS
TOP_K = 2
D_FF_EXPERT = 1024
NORM_EPS = 1e-6

LANES = 128
MOE_BLOCK = 256
PROJ_TILE = 512

COL_QA = 0
COL_KA = A_WIDTH
COL_VA = 2 * A_WIDTH
COL_QB = 3 * A_WIDTH
COL_KB = COL_QB + GLA_KEY
COL_VB = COL_KB + GLA_KEY
COL_RB = COL_VB + GLA_VAL
COL_ALOW = COL_RB + GLA_VAL
COL_GATES = COL_ALOW + GLA_LOWRANK


def _params(vmem_mib, sem=None):
    kw = dict(vmem_limit_bytes=int(vmem_mib) << 20)
    if sem is not None:
        kw["dimension_semantics"] = sem
    return pltpu.CompilerParams(**kw)


def _sigmoid(x):
    return 0.5 * jnp.tanh(0.5 * x) + 0.5


def _ada_kernel(c_ref, w_ref, b_ref, o_ref):
    @pl.when(pl.program_id(1) == 0)
    def _():
        o_ref[0] = jnp.broadcast_to(b_ref[0], o_ref.shape[1:])

    c = c_ref[...]
    ca = (c * _sigmoid(c)).astype(BF16)
    o_ref[0] += jnp.dot(ca, w_ref[0].astype(BF16), preferred_element_type=F32)


def _ada(c, w_ada, b_ada):
    depth, d, n = w_ada.shape
    b = c.shape[0]
    tk = 256
    return pl.pallas_call(
        _ada_kernel,
        out_shape=jax.ShapeDtypeStruct((depth, b, n), F32),
        grid=(depth, d // tk),
        in_specs=[
            pl.BlockSpec((b, tk), lambda l, k: (0, k)),
            pl.BlockSpec((1, tk, n), lambda l, k: (l, k, 0)),
            pl.BlockSpec((1, 1, n), lambda l, k: (l, 0, 0)),
        ],
        out_specs=pl.BlockSpec((1, b, n), lambda l, k: (l, 0, 0)),
        compiler_params=_params(40, ("arbitrary", "arbitrary")),
        name="ada",
    )(c, w_ada, b_ada.reshape(depth, 1, n))


def _norm_mod_kernel(x_ref, ada_ref, gain_ref, wal_ref, h_ref, al_ref):
    x = x_ref[...]
    y = x * lax.rsqrt(jnp.mean(x * x, axis=-1, keepdims=True) + NORM_EPS) * gain_ref[...]
    h = (y * (1.0 + ada_ref[0, 1:2, :]) + ada_ref[0, 0:1, :]).astype(BF16)
    h_ref[...] = h
    al_ref[...] = lax.dot_general(h, wal_ref[...], (((1,), (1,)), ((), ())), preferred_element_type=F32)


def _norm_mod(x2, ada_l, gain, w_alow):
    t, d = x2.shape
    tm = 512
    per_b = SEQ // tm
    return pl.pallas_call(
        _norm_mod_kernel,
        out_shape=(jax.ShapeDtypeStruct((t, d), BF16), jax.ShapeDtypeStruct((t, LANES), F32)),
        grid=(t // tm,),
        in_specs=[
            pl.BlockSpec((tm, d), lambda i: (i, 0)),
            pl.BlockSpec((1, 6, d), lambda i: (i // per_b, 0, 0)),
            pl.BlockSpec((1, d), lambda i: (0, 0)),
            pl.BlockSpec((LANES, d), lambda i: (0, 0)),
        ],
        out_specs=(pl.BlockSpec((tm, d), lambda i: (i, 0)), pl.BlockSpec((tm, LANES), lambda i: (i, 0))),
        compiler_params=_params(40),
        name="norm_mod",
    )(x2, ada_l, gain, w_alow)


def _matmul_nt_kernel(a_ref, b_ref, *rest, n_side):
    side_in, o_ref, side_out = rest[:n_side], rest[n_side], rest[n_side + 1:]
    o_ref[...] = lax.dot_general(a_ref[...], b_ref[...].astype(BF16), (((1,), (1,)), ((), ())),
                                 preferred_element_type=F32).astype(o_ref.dtype)
    for s_ref, so_ref in zip(side_in, side_out):
        so_ref[...] = s_ref[...].astype(so_ref.dtype)


def _in_proj(h, w_t, layer, col0, n_cols, name, sides=()):
    t, d = h.shape
    tm, tn = 2048, PROJ_TILE
    sub = 8
    assert col0 % sub == 0 and tn % sub == 0
    w_spec = pl.BlockSpec((pl.Squeezed(), pl.Element(tn), pl.Element(d)),
                          lambda i, j: (layer, (col0 // sub + j * (tn // sub)) * sub, 0))
    n_j = n_cols // tn
    n_steps = (t // tm) * n_j
    bf16_rows = 16
    in_specs = [pl.BlockSpec((tm, d), lambda i, j: (i, 0)), w_spec]
    out_shape = [jax.ShapeDtypeStruct((t, n_cols), BF16)]
    out_specs = [pl.BlockSpec((tm, tn), lambda i, j: (i, j))]
    for s in sides:
        _, rows, cols = s.shape
        rb = -(-pl.cdiv(rows, n_steps) // bf16_rows) * bf16_rows
        last = pl.cdiv(rows, rb) - 1
        slab = lambda i, j, last=last: jnp.minimum(i * n_j + j, last)
        in_specs.append(pl.BlockSpec((pl.Squeezed(), rb, cols), lambda i, j, slab=slab: (layer, slab(i, j), 0)))
        out_shape.append(jax.ShapeDtypeStruct((rows, cols), BF16))
        out_specs.append(pl.BlockSpec((rb, cols), lambda i, j, slab=slab: (slab(i, j), 0)))
    return pl.pallas_call(
        functools.partial(_matmul_nt_kernel, n_side=len(sides)),
        out_shape=tuple(out_shape),
        grid=(t // tm, n_j),
        in_specs=in_specs,
        out_specs=tuple(out_specs),
        compiler_params=_params(56, ("arbitrary", "arbitrary")),
        name=name,
    )(h, w_t, *sides)


def _attn_kernel(q_ref, k_ref, v_ref, cos_ref, sin_ref, o_ref, lse_ref, q_s, k_s, v_s, o_s, l_s, *, dil, nb, wd):
    scale = HEAD_DIM ** -0.5
    blk = ATT_BLOCK
    nh = A_HEADS_PER_GROUP
    cosf = cos_ref[...]
    sinf = sin_ref[...]
    for h in range(nh):
        hc = slice(h * HEAD_DIM, (h + 1) * HEAD_DIM)
        tq = q_ref[:, hc].astype(F32)
        q_s[h] = (tq * cosf + pltpu.roll(tq, HEAD_DIM // 2, 1) * sinf) * scale
        tk = k_ref[:, hc].astype(F32)
        k_s[h] = tk * cosf + pltpu.roll(tk, HEAD_DIM // 2, 1) * sinf
        v_s[h] = v_ref[:, hc].astype(F32)

    def rows(start, size):
        return pl.ds(start, size) if dil == 1 else pl.ds(start, size, stride=dil)

    def head_mask(nk, offset):
        i = lax.broadcasted_iota(jnp.int32, (nh * blk, nk), 0) % blk
        j = lax.broadcasted_iota(jnp.int32, (nh * blk, nk), 1)
        dist = offset + i - j
        return (dist >= 0) & (dist <= wd)

    mask_first = head_mask(blk, 0)
    mask_later = head_mask(2 * blk, blk)
    lane = lax.broadcasted_iota(jnp.int32, (blk, LANES), 1)
    ones = jnp.ones((2 * blk, HEAD_DIM), BF16)

    def attend(qs, ks, nk, mask):
        s = jnp.concatenate(
            [lax.dot_general(q_s[h, rows(qs, blk), :].astype(BF16), k_s[h, rows(ks, nk), :].astype(BF16),
                             (((1,), (1,)), ((), ())), preferred_element_type=F32) for h in range(nh)], axis=0)
        s = jnp.where(mask, s, -jnp.inf)
        m = jnp.max(s, axis=-1, keepdims=True)
        p = jnp.exp(s - m).astype(BF16)
        lse_tile = jnp.zeros((blk, LANES), F32)
        for h in range(nh):
            v_aug = jnp.concatenate([v_s[h, rows(ks, nk), :].astype(BF16), ones[:nk]], axis=1)
            ov = jnp.dot(p[h * blk:(h + 1) * blk], v_aug, preferred_element_type=F32)
            den = ov[:, HEAD_DIM:]
            o_s[h, rows(qs, blk), :] = ov[:, :HEAD_DIM] / den
            lse_tile = jnp.where(lane == h, m[h * blk:(h + 1) * blk] + jnp.log(den), lse_tile)
        l_s[rows(qs, blk), :] = lse_tile

    def per_phase(ph, carry):
        attend(ph, ph, blk, mask_first)

        def per_blk(n, c):
            qs = ph + dil * blk * n
            ks = ph + dil * blk * (n - 1)
            if dil == 1:
                qs = pl.multiple_of(qs, blk)
                ks = pl.multiple_of(ks, blk)
            attend(qs, ks, 2 * blk, mask_later)
            return c

        if nb > 1:
            lax.fori_loop(1, nb, per_blk, 0, unroll=min(nb - 1, 5))
        return carry

    lax.fori_loop(0, dil, per_phase, 0, unroll=max(1, min(dil, 8 // nb)))
    for h in range(nh):
        o_ref[:, h * HEAD_DIM:(h + 1) * HEAD_DIM] = o_s[h].astype(o_ref.dtype)
    lse_ref[...] = l_s[...]


def _attention(proj, cosf, sinf, group):
    window, dil = DILATED_GROUPS[group]
    t = proj.shape[0]
    nb = (SEQ // dil) // ATT_BLOCK
    col = lambda base: base // A_OUT + group
    kern = functools.partial(_attn_kernel, dil=dil, nb=nb, wd=window // dil)
    return pl.pallas_call(
        kern,
        out_shape=(jax.ShapeDtypeStruct((t, A_OUT), BF16), jax.ShapeDtypeStruct((t, LANES), F32)),
        grid=(t // SEQ,),
        in_specs=[
            pl.BlockSpec((SEQ, A_OUT), lambda b: (b, col(COL_QA))),
            pl.BlockSpec((SEQ, A_OUT), lambda b: (b, col(COL_KA))),
            pl.BlockSpec((SEQ, A_OUT), lambda b: (b, col(COL_VA))),
            pl.BlockSpec((SEQ, HEAD_DIM), lambda b: (0, 0)),
            pl.BlockSpec((SEQ, HEAD_DIM), lambda b: (0, 0)),
        ],
        out_specs=(pl.BlockSpec((SEQ, A_OUT), lambda b: (b, 0)), pl.BlockSpec((SEQ, LANES), lambda b: (b, 0))),
        scratch_shapes=[pltpu.VMEM((A_HEADS_PER_GROUP, SEQ, HEAD_DIM), F32) for _ in range(4)]
        + [pltpu.VMEM((SEQ, LANES), F32)],
        compiler_params=_params(48),
        name=f"dilated_attn_g{group}",
    )(proj, proj, proj, cosf, sinf)


def _gla_kernel(q_ref, k_ref, v0_ref, v1_ref, r0_ref, r1_ref, al_ref, wa_ref, ba_ref, gain_ref, y_ref, st_ref, *, rows):
    @pl.when(pl.program_id(1) == 0)
    def _():
        st_ref[...] = jnp.zeros_like(st_ref)

    ck = GLA_CHUNK
    nseq = q_ref.shape[0]
    half = GLA_HEADS // 2
    ri = lax.broadcasted_iota(jnp.int32, (rows, rows), 0)
    ci = lax.broadcasted_iota(jnp.int32, (rows, rows), 1)
    same = (ri // ck) == (ci // ck)
    tri = (same & (ri >= ci)).astype(BF16)
    tot = same.astype(BF16)

    def prepare(sq):
        z = jnp.dot(al_ref[sq].astype(BF16), wa_ref[...], preferred_element_type=F32) + ba_ref[...]
        log_a = (jnp.minimum(z, 0.0) - jnp.log(1.0 + jnp.exp(-jnp.abs(z)))) * (1.0 / GLA_TAU)
        la_hi = log_a.astype(BF16)
        la_lo = (log_a - la_hi.astype(F32)).astype(BF16)
        b = jnp.dot(tri, la_hi, preferred_element_type=F32) + jnp.dot(tri, la_lo, preferred_element_type=F32)
        b_end = jnp.dot(tot, la_hi, preferred_element_type=F32) + jnp.dot(tot, la_lo, preferred_element_type=F32)
        q = q_ref[sq].astype(F32) * (GLA_DK ** -0.5)
        k = k_ref[sq].astype(F32)
        return ((q * jnp.exp(b)).astype(BF16), (k * jnp.exp(-b)).astype(BF16),
                (k * jnp.exp(b_end - b)).astype(BF16), jnp.exp(b_end))

    prepared = [prepare(sq) for sq in range(nseq)]
    intra_mask = same & (ri >= ci)
    nt = (((1,), (1,)), ((), ()))
    v_refs = (v0_ref, v1_ref)
    r_refs = (r0_ref, r1_ref)
    for sq in range(nseq):
        q_dec, k_inv, k_end, decay = prepared[sq]
        for h in range(GLA_HEADS):
            hk = slice(h * GLA_DK, (h + 1) * GLA_DK)
            hv = slice(h * GLA_DV, (h + 1) * GLA_DV)
            hb = slice((h % half) * GLA_DV, (h % half + 1) * GLA_DV)
            v = v_refs[h // half][sq, :, hb]
            attn = lax.dot_general(q_dec[:, hk], k_inv[:, hk], nt, preferred_element_type=F32)
            attn = jnp.where(intra_mask, attn, 0.0).astype(BF16)
            o = jnp.dot(attn, v, preferred_element_type=F32)
            st = st_ref[sq * GLA_HEADS + h]
            inter = []
            for c in range(rows // ck):
                rs = slice(c * ck, (c + 1) * ck)
                inter.append(lax.dot_general(q_dec[rs, hk], st.astype(BF16), nt, preferred_element_type=F32))
                v_t = v[rs].astype(F32).T.astype(BF16)
                st = st * decay[c * ck:c * ck + 1, hk] + jnp.dot(v_t, k_end[rs, hk], preferred_element_type=F32)
            st_ref[sq * GLA_HEADS + h] = st
            o = o + jnp.concatenate(inter, axis=0)
            o = o * lax.rsqrt(jnp.mean(o * o, axis=-1, keepdims=True) + NORM_EPS) * gain_ref[:, hv]
            r = r_refs[h // half][sq, :, hb].astype(F32)
            y_ref[sq, :, hv] = (o * (r * _sigmoid(r))).astype(y_ref.dtype)


def _gla(proj, a_low, w_alpha2p, b_alpha2, gla_gain):
    t, width = proj.shape
    nb = t // SEQ
    rows = 256
    nseq = 2 if nb % 2 == 0 else 1
    kern = functools.partial(_gla_kernel, rows=rows)
    col = lambda base, j=0: (lambda b, s: (b, s, base // GLA_KEY + j))
    blk = lambda w: (nseq, rows, w)
    proj3 = proj.reshape(nb, SEQ, width)
    y = pl.pallas_call(
        kern,
        out_shape=jax.ShapeDtypeStruct((nb, SEQ, GLA_VAL), BF16),
        grid=(nb // nseq, SEQ // rows),
        in_specs=[
            pl.BlockSpec(blk(GLA_KEY), col(COL_QB)),
            pl.BlockSpec(blk(GLA_KEY), col(COL_KB)),
            pl.BlockSpec(blk(GLA_KEY), col(COL_VB, 0)),
            pl.BlockSpec(blk(GLA_KEY), col(COL_VB, 1)),
            pl.BlockSpec(blk(GLA_KEY), col(COL_RB, 0)),
            pl.BlockSpec(blk(GLA_KEY), col(COL_RB, 1)),
            pl.BlockSpec(blk(LANES), lambda b, s: (b, s, 0)),
            pl.BlockSpec((LANES, GLA_KEY), lambda b, s: (0, 0)),
            pl.BlockSpec((1, GLA_KEY), lambda b, s: (0, 0)),
            pl.BlockSpec((1, GLA_VAL), lambda b, s: (0, 0)),
        ],
        out_specs=pl.BlockSpec(blk(GLA_VAL), lambda b, s: (b, s, 0)),
        scratch_shapes=[pltpu.VMEM((nseq * GLA_HEADS, GLA_DV, GLA_DK), F32)],
        compiler_params=_params(32),
        name="gla",
    )(proj3, proj3, proj3, proj3, proj3, proj3, a_low.reshape(nb, SEQ, LANES), w_alpha2p, b_alpha2, gla_gain)
    return y.reshape(t, GLA_VAL)


def _post_mixer_kernel(x_ref, ga_ref, gb_ref, o0_ref, o1_ref, o2_ref, l0_ref, l1_ref, l2_ref, yb_ref, ada_ref,
                       woa_ref, wob_ref, wout_ref, g2_ref, wr_ref, br_ref,
                       xo_ref, h2_ref, e_ref, w_ref):
    ls = [l0_ref[...], l1_ref[...], l2_ref[...]]
    os_ = [o0_ref, o1_ref, o2_ref]
    mx = jnp.maximum(jnp.maximum(ls[0], ls[1]), ls[2])
    es = [jnp.exp(l - mx) for l in ls]
    zs = es[0] + es[1] + es[2]
    wt = [e / zs for e in es]
    parts = []
    for h in range(A_HEADS_PER_GROUP):
        hc = slice(h * HEAD_DIM, (h + 1) * HEAD_DIM)
        acc = wt[0][:, h:h + 1] * os_[0][:, hc].astype(F32)
        acc = acc + wt[1][:, h:h + 1] * os_[1][:, hc].astype(F32)
        acc = acc + wt[2][:, h:h + 1] * os_[2][:, hc].astype(F32)
        parts.append(acc)
    ya = jnp.concatenate(parts, axis=1).astype(BF16)
    ta = jnp.dot(ya, woa_ref[0], preferred_element_type=F32)
    tb = jnp.dot(yb_ref[...], wob_ref[0], preferred_element_type=F32)
    merged = _sigmoid(ga_ref[...].astype(F32)) * ta + _sigmoid(gb_ref[...].astype(F32)) * tb
    mix = jnp.dot(merged.astype(BF16), wout_ref[0], preferred_element_type=F32)
    xn = x_ref[...] + ada_ref[0, 2:3, :] * mix
    xo_ref[...] = xn
    y = xn * lax.rsqrt(jnp.mean(xn * xn, axis=-1, keepdims=True) + NORM_EPS) * g2_ref[...]
    h2 = y * (1.0 + ada_ref[0, 4:5, :]) + ada_ref[0, 3:4, :]
    h2_ref[...] = h2

    logits = lax.dot_general(wr_ref[...], h2.astype(BF16), (((1,), (1,)), ((), ())), preferred_element_type=F32)
    tm = logits.shape[1]
    aff_all = _sigmoid(logits)
    sel_all = aff_all + br_ref[...]
    sel = [sel_all[e:e + 1, :] for e in range(N_EXPERTS)]
    aff = [aff_all[e:e + 1, :] for e in range(N_EXPERTS)]
    neg = -jnp.inf

    def top2_sum(v0, v1, v2, v3):
        hi01, lo01 = jnp.maximum(v0, v1), jnp.minimum(v0, v1)
        hi23, lo23 = jnp.maximum(v2, v3), jnp.minimum(v2, v3)
        return jnp.maximum(hi01, hi23) + jnp.maximum(jnp.minimum(hi01, hi23), jnp.maximum(lo01, lo23))

    epg = EXPERTS_PER_GROUP
    scores = [top2_sum(*sel[g * epg:(g + 1) * epg]) for g in range(N_GROUPS)]
    best = jnp.zeros_like(scores[0])
    best_s = scores[0]
    for g in range(1, N_GROUPS):
        better = scores[g] > best_s
        best = jnp.where(better, float(g), best)
        best_s = jnp.where(better, scores[g], best_s)
    vals, affs = [], []
    for j in range(epg):
        vj, aj = sel[j], aff[j]
        for g in range(1, N_GROUPS):
            in_g = best == float(g)
            vj = jnp.where(in_g, sel[g * epg + j], vj)
            aj = jnp.where(in_g, aff[g * epg + j], aj)
        vals.append(vj)
        affs.append(aj)

    def first_argmax(vs):
        idx, m, a = jnp.zeros_like(vs[0]), vs[0], affs[0]
        for j in range(1, epg):
            gt = vs[j] > m
            idx = jnp.where(gt, float(j), idx)
            m = jnp.where(gt, vs[j], m)
            a = jnp.where(gt, affs[j], a)
        return idx, a

    i1, a1 = first_argmax(vals)
    i2, a2 = first_argmax([jnp.where(i1 == float(j), neg, vals[j]) for j in range(epg)])
    tot = a1 + a2
    srow = lax.broadcasted_iota(jnp.int32, (8, tm), 0)
    packed = jnp.where(srow == 0, best * epg + i1,
                       jnp.where(srow == 1, best * epg + i2,
                                 jnp.where(srow == 2, a1 / tot, jnp.where(srow == 3, a2 / tot, 0.0))))
    packed_t = jnp.concatenate([packed, jnp.zeros((LANES - 8, tm), F32)], axis=0).T
    e_ref[...] = packed_t[:, 0:TOP_K].astype(jnp.int32)
    w_ref[...] = packed_t[:, TOP_K:2 * TOP_K]


def _post_mixer(x2, gates, layer, o_list, l_list, y_b, ada_l, woa, wob, wout, gain2, w_router_t, b_router_c):
    t, d = x2.shape
    tm = 256
    per_b = SEQ // tm
    const = lambda shape: pl.BlockSpec(shape, lambda i: (0,) * len(shape), pipeline_mode=pl.Buffered(1))
    wblk = lambda k: pl.BlockSpec((1, k, d), lambda i: (layer, 0, 0), pipeline_mode=pl.Buffered(1))
    rowblk = lambda w: pl.BlockSpec((tm, w), lambda i: (i, 0))
    return pl.pallas_call(
        _post_mixer_kernel,
        out_shape=(
            jax.ShapeDtypeStruct((t, d), F32),
            jax.ShapeDtypeStruct((t, d), F32),
            jax.ShapeDtypeStruct((t, TOP_K), jnp.int32),
            jax.ShapeDtypeStruct((t, TOP_K), F32),
        ),
        grid=(t // tm,),
        in_specs=[
            rowblk(d),
            pl.BlockSpec((tm, d), lambda i: (i, 0)),
            pl.BlockSpec((tm, d), lambda i: (i, 1)),
            rowblk(A_OUT), rowblk(A_OUT), rowblk(A_OUT),
            rowblk(LANES), rowblk(LANES), rowblk(LANES),
            rowblk(GLA_VAL),
            pl.BlockSpec((1, 6, d), lambda i: (i // per_b, 0, 0)),
            wblk(A_OUT), wblk(GLA_VAL), wblk(d),
            const((1, d)), const((LANES, d)), const((LANES, 1)),
        ],
        out_specs=(rowblk(d), rowblk(d), rowblk(TOP_K), rowblk(TOP_K)),
        compiler_params=_params(56),
        name="post_mixer",
    )(x2, gates, gates, *o_list, *l_list, y_b, ada_l, woa, wob, wout, gain2, w_router_t, b_router_c)


def _dispatch_kernel(dest_ref, cnt_ref, h_ref, x_hbm, sem, *, tm):
    i = pl.program_id(0)
    n_rows = x_hbm.shape[0]

    def copy(r, row):
        return pltpu.make_async_copy(h_ref.at[pl.ds(r, 1), :], x_hbm.at[pl.ds(row, 1), :], sem)

    def rows(start):
        def body(r, carry):
            for k in range(TOP_K):
                cp = copy(r, dest_ref[(i * tm + r) * TOP_K + k])
                if start:
                    cp.start(priority=k % 2)
                else:
                    cp.wait()
            return carry
        if start:
            for r in range(tm):
                body(r, 0)
        else:
            lax.fori_loop(0, tm, body, 0, unroll=4)

    def fill(lo, hi, start):
        def body(r, carry):
            cp = copy(0, r)
            if start:
                cp.start()
            else:
                cp.wait()
            return carry
        lax.fori_loop(lo, hi, body, 0)

    def pads(start):
        def per_expert(e, base):
            cnt = cnt_ref[e]
            padded = (cnt + (MOE_BLOCK - 1)) // MOE_BLOCK * MOE_BLOCK
            fill(base + cnt, base + padded, start)
            return base + padded
        end = lax.fori_loop(0, N_EXPERTS, per_expert, 0)
        fill(end, n_rows, start)

    rows(True)

    @pl.when(i == 0)
    def _():
        pads(True)
        pads(False)

    rows(False)


def _dispatch(dest, counts, h2, n_rows):
    t, d = h2.shape
    tm = 256
    grid_spec = pltpu.PrefetchScalarGridSpec(
        num_scalar_prefetch=2,
        grid=(t // tm,),
        in_specs=[pl.BlockSpec((tm, d), lambda i, dst, cnt: (i, 0))],
        out_specs=pl.BlockSpec(memory_space=pl.ANY),
        scratch_shapes=[pltpu.SemaphoreType.DMA(())],
    )
    return pl.pallas_call(
        functools.partial(_dispatch_kernel, tm=tm),
        out_shape=jax.ShapeDtypeStruct((n_rows, d), h2.dtype),
        grid_spec=grid_spec,
        compiler_params=_params(16, ("arbitrary",)),
        name="moe_dispatch",
    )(dest, counts, h2)


def _moe_kernel(be_ref, nu_ref, x_ref, wg_ref, wu_ref, wd_ref, y_ref):
    i = pl.program_id(0)
    n_used = nu_ref[0]

    @pl.when(i < n_used)
    def _():
        xb = x_ref[...].astype(BF16)
        g = jnp.dot(xb, wg_ref[0], preferred_element_type=F32)
        u = jnp.dot(xb, wu_ref[0], preferred_element_type=F32)
        a = (g * _sigmoid(g) * u).astype(BF16)
        y_ref[...] = jnp.dot(a, wd_ref[0], preferred_element_type=F32)

    @pl.when(i >= n_used)
    def _():
        y_ref[...] = jnp.zeros_like(y_ref)


def _moe_experts(blk_e, n_used, xbuf, wg, wu, wd):
    n_rows, d = xbuf.shape
    n_blk = blk_e.shape[0]
    f = wg.shape[2]
    grid_spec = pltpu.PrefetchScalarGridSpec(
        num_scalar_prefetch=2,
        grid=(n_blk,),
        in_specs=[
            pl.BlockSpec((MOE_BLOCK, d), lambda i, be, nu: (jnp.minimum(i, nu[0] - 1), 0)),
            pl.BlockSpec((1, d, f), lambda i, be, nu: (be[i], 0, 0)),
            pl.BlockSpec((1, d, f), lambda i, be, nu: (be[i], 0, 0)),
            pl.BlockSpec((1, f, d), lambda i, be, nu: (be[i], 0, 0)),
        ],
        out_specs=pl.BlockSpec((MOE_BLOCK, d), lambda i, be, nu: (i, 0)),
    )
    return pl.pallas_call(
        _moe_kernel,
        out_shape=jax.ShapeDtypeStruct((n_rows, d), F32),
        grid_spec=grid_spec,
        compiler_params=_params(48, ("arbitrary",)),
        name="moe_experts",
    )(blk_e, n_used, xbuf, wg, wu, wd)


def _pair_copy(src_hbm, row, dst, slot, k, r, sem):
    return pltpu.make_async_copy(src_hbm.at[pl.ds(row, 1), :], dst.at[slot, k, pl.ds(r, 1), :], sem.at[slot])


def _combine_kernel(pos_ref, y_hbm, x_ref, w_ref, ada_ref, gain_ref, adan_ref, wal_ref, *rest, tm, final):
    if final:
        o_ref, yg, sem = rest
    else:
        o_ref, h_ref, al_ref, yg, sem = rest
    i = pl.program_id(0)
    n = pl.num_programs(0)
    slot = i % 2

    def gather(tile, s, start, static_rows=False):
        def body(r, c):
            for k in range(TOP_K):
                cp = _pair_copy(y_hbm, pos_ref[(tile * tm + r) * TOP_K + k], yg, s, k, r, sem)
                if start:
                    cp.start(priority=k % 2)
                else:
                    cp.wait()
            return c
        if static_rows:
            for r in range(tm):
                body(r, 0)
        else:
            lax.fori_loop(0, tm, body, 0, unroll=8)

    @pl.when(i == 0)
    def _():
        gather(0, 0, True)

    @pl.when(i + 1 < n)
    def _():
        gather(i + 1, 1 - slot, True, static_rows=True)

    gather(i, slot, False)
    w = w_ref[...]
    y = w[:, 0:1] * yg[slot, 0] + w[:, 1:2] * yg[slot, 1]
    out = x_ref[...] + ada_ref[0, 5:6, :] * y
    normed = out * lax.rsqrt(jnp.mean(out * out, axis=-1, keepdims=True) + NORM_EPS) * gain_ref[...]
    if final:
        o_ref[...] = normed
    else:
        o_ref[...] = out
        h = (normed * (1.0 + adan_ref[0, 1:2, :]) + adan_ref[0, 0:1, :]).astype(BF16)
        h_ref[...] = h
        al_ref[...] = lax.dot_general(h, wal_ref[...], (((1,), (1,)), ((), ())), preferred_element_type=F32)


def _combine(pos, ybuf, x2, gw, ada_l, gain, ada_next, w_alow_next, final):
    t, d = x2.shape
    tm = 256
    per_b = SEQ // tm
    kern = functools.partial(_combine_kernel, tm=tm, final=final)
    rowblk = lambda w: pl.BlockSpec((tm, w), lambda i, p: (i, 0))
    out_shape = [jax.ShapeDtypeStruct((t, d), F32)]
    out_specs = [rowblk(d)]
    if not final:
        out_shape += [jax.ShapeDtypeStruct((t, d), BF16), jax.ShapeDtypeStruct((t, LANES), F32)]
        out_specs += [rowblk(d), rowblk(LANES)]
    grid_spec = pltpu.PrefetchScalarGridSpec(
        num_scalar_prefetch=1,
        grid=(t // tm,),
        in_specs=[
            pl.BlockSpec(memory_space=pl.ANY),
            rowblk(d),
            rowblk(TOP_K),
            pl.BlockSpec((1, 6, d), lambda i, p: (i // per_b, 0, 0)),
            pl.BlockSpec((1, d), lambda i, p: (0, 0)),
            pl.BlockSpec((1, 6, d), lambda i, p: (i // per_b, 0, 0)),
            pl.BlockSpec((LANES, d), lambda i, p: (0, 0)),
        ],
        out_specs=tuple(out_specs),
        scratch_shapes=[pltpu.VMEM((2, TOP_K, tm, d), F32), pltpu.SemaphoreType.DMA((2,))],
    )
    return pl.pallas_call(
        kern,
        out_shape=tuple(out_shape),
        grid_spec=grid_spec,
        compiler_params=_params(48, ("arbitrary",)),
        name="moe_combine_final" if final else "moe_combine",
    )(pos, ybuf, x2, gw, ada_l, gain, ada_next, w_alow_next)


def _plan_kernel(e_ref, dest_ref, cnt_ref, carry, pstart_s, *, tm):
    ph = pl.program_id(0)
    i = pl.program_id(1)
    lane = lax.broadcasted_iota(jnp.int32, (tm, LANES), 1)
    e = e_ref[...]
    oh0 = e[:, 0:1] == lane
    oh1 = e[:, 1:2] == lane
    oh = oh0.astype(F32) + oh1.astype(F32)
    tile_counts = jnp.sum(oh, axis=0, keepdims=True)

    @pl.when((ph == 0) & (i == 0))
    def _():
        carry[...] = jnp.zeros_like(carry)

    @pl.when(ph == 0)
    def _():
        carry[...] += tile_counts

    @pl.when((ph == 1) & (i == 0))
    def _():
        counts = carry[...]
        cnt_ref[...] = counts.astype(jnp.int32)
        nblk = jnp.floor((counts + (MOE_BLOCK - 1)) * (1.0 / MOE_BLOCK))
        r = lax.broadcasted_iota(jnp.int32, (LANES, LANES), 0)
        c = lax.broadcasted_iota(jnp.int32, (LANES, LANES), 1)
        upper = (r < c).astype(BF16)
        nb8 = jnp.broadcast_to(nblk, (8, LANES)).astype(BF16)
        pstart_s[...] = jnp.dot(nb8, upper, preferred_element_type=F32)[0:1, :] * float(MOE_BLOCK)
        carry[...] = jnp.zeros_like(carry)

    @pl.when(ph == 1)
    def _():
        r = lax.broadcasted_iota(jnp.int32, (tm, tm), 0)
        c = lax.broadcasted_iota(jnp.int32, (tm, tm), 1)
        lower = (r > c).astype(BF16)
        before = jnp.dot(lower, oh.astype(BF16), preferred_element_type=F32) + carry[...]
        base = before + pstart_s[...]
        d0 = jnp.sum(jnp.where(oh0, base, 0.0), axis=1, keepdims=True)
        d1 = jnp.sum(jnp.where(oh1, base, 0.0), axis=1, keepdims=True)
        col = lax.broadcasted_iota(jnp.int32, (tm, TOP_K), 1)
        dest_ref[...] = jnp.where(col == 0, d0, d1).astype(jnp.int32)
        carry[...] += tile_counts


def _dispatch_plan(eidx, n_blk):
    t = eidx.shape[0]
    tm = 512
    dest, counts = pl.pallas_call(
        functools.partial(_plan_kernel, tm=tm),
        out_shape=(jax.ShapeDtypeStruct((t, TOP_K), jnp.int32), jax.ShapeDtypeStruct((1, LANES), jnp.int32)),
        grid=(2, t // tm),
        in_specs=[pl.BlockSpec((tm, TOP_K), lambda p, i: (i, 0))],
        out_specs=(pl.BlockSpec((tm, TOP_K), lambda p, i: (i * p, 0)), pl.BlockSpec((1, LANES), lambda p, i: (0, 0))),
        scratch_shapes=[pltpu.VMEM((1, LANES), F32), pltpu.VMEM((1, LANES), F32)],
        compiler_params=_params(32, ("arbitrary", "arbitrary")),
        name="dispatch_plan",
    )(eidx)
    counts = counts[0, :N_EXPERTS]
    pcounts = (counts + MOE_BLOCK - 1) // MOE_BLOCK * MOE_BLOCK
    pend = jnp.cumsum(pcounts)
    blk_start = jnp.arange(n_blk, dtype=jnp.int32) * MOE_BLOCK
    blk_e = jnp.sum((pend[None, :] <= blk_start[:, None]).astype(jnp.int32), axis=1)
    blk_e = jnp.minimum(blk_e, N_EXPERTS - 1).astype(jnp.int32)
    n_used = (pend[-1] // MOE_BLOCK).astype(jnp.int32).reshape(1)
    return blk_e, n_used, dest.reshape(-1), counts


def _rope_tables():
    pos = jnp.arange(SEQ, dtype=F32)
    inv_freq = ROPE_THETA ** (-jnp.arange(0, HEAD_DIM, 2, dtype=F32) / HEAD_DIM)
    ang = pos[:, None] * inv_freq[None, :]
    cos, sin = jnp.cos(ang), jnp.sin(ang)
    return jnp.concatenate([cos, cos], axis=1), jnp.concatenate([-sin, sin], axis=1)


def kernel(x, c, w_ada, b_ada, norm_mix, norm_ffn, w_in, w_alpha2, b_alpha2, gla_gain, w_out_a, w_out_b, w_out,
           w_router, b_router, w_gate_e, w_up_e, w_down_e, final_norm):
    b, s, d = x.shape
    depth = w_ada.shape[0]
    t = b * s
    n_blk = (t * TOP_K) // MOE_BLOCK + N_EXPERTS
    cosf, sinf = _rope_tables()
    ada = _ada(c, w_ada, b_ada).reshape(depth, b, 6, d)
    w_router_t = jnp.pad(w_router.T, ((0, LANES - N_EXPERTS), (0, 0))).astype(BF16)
    b_router_c = jnp.pad(b_router, (0, LANES - N_EXPERTS)).reshape(LANES, 1)
    final_gain = final_norm.reshape(1, d)
    w_in_t = jnp.swapaxes(w_in, 1, 2)
    w_alow_all = jnp.pad(w_in_t[:, COL_ALOW:COL_GATES, :], ((0, 0), (0, LANES - GLA_LOWRANK), (0, 0))).astype(BF16)
    w_oa, w_ob, w_o = w_out_a.astype(BF16), w_out_b.astype(BF16), w_out.astype(BF16)
    n_exp, _, d_ff = w_gate_e.shape[1:]
    wg_rows = w_gate_e.reshape(depth, n_exp * d, d_ff)
    wu_rows = w_up_e.reshape(depth, n_exp * d, d_ff)
    wd_rows = w_down_e.reshape(depth, n_exp * d_ff, d)
    x2 = x.reshape(t, d)
    h, a_low = _norm_mod(x2, ada[0], norm_mix[0].reshape(1, d), w_alow_all[0])
    for l in range(depth):
        w_alpha2p = jnp.pad(w_alpha2[l], ((0, LANES - GLA_LOWRANK), (0, 0))).astype(BF16)
        proj, w_g, w_u = _in_proj(h, w_in_t, l, 0, COL_ALOW, "in_proj", sides=(wg_rows, wu_rows))
        gates, w_d = _in_proj(h, w_in_t, l, COL_GATES, 2 * D_MODEL, "in_proj_gates", sides=(wd_rows,))
        w_g, w_u, w_d = (w_g.reshape(n_exp, d, d_ff), w_u.reshape(n_exp, d, d_ff), w_d.reshape(n_exp, d_ff, d))
        o_list, l_list = [], []
        for g in range(len(DILATED_GROUPS)):
            o_g, l_g = _attention(proj, cosf, sinf, g)
            o_list.append(o_g)
            l_list.append(l_g)
        y_b = _gla(proj, a_low, w_alpha2p, b_alpha2[l].reshape(1, GLA_KEY), gla_gain[l].reshape(1, GLA_VAL))
        x2, h2, eidx, gw = _post_mixer(
            x2, gates, l, o_list, l_list, y_b, ada[l], w_oa, w_ob, w_o,
            norm_ffn[l].reshape(1, d), w_router_t, b_router_c)
        blk_e, n_used, dest, counts = _dispatch_plan(eidx, n_blk)
        xbuf = _dispatch(dest, counts, h2, n_blk * MOE_BLOCK)
        ybuf = _moe_experts(blk_e, n_used, xbuf, w_g, w_u, w_d)
        if l == depth - 1:
            (x2,) = _combine(dest, ybuf, x2, gw, ada[l], final_gain, ada[l], w_alow_all[l], final=True)
        else:
            x2, h, a_low = _combine(dest, ybuf, x2, gw, ada[l], norm_mix[l + 1].reshape(1, d), ada[l + 1],
                                    w_alow_all[l + 1], final=False)
    return x2.reshape(b, s, d)
```

```python
import functools

import jax
import jax.numpy as jnp
from jax import lax
from jax.experimental import pallas as pl
from jax.experimental.pallas import tpu as pltpu

F32 = jnp.float32
BF16 = jnp.bfloat16

D_MODEL = 2048
SEQ = 2048
HEAD_DIM = 128
A_HEADS_PER_GROUP = 4
DILATED_GROUPS = ((128, 1), (512, 4), (2048, 16))
A_WIDTH = 12 * HEAD_DIM
A_OUT = A_HEADS_PER_GROUP * HEAD_DIM
ATT_BLOCK = 128
ROPE_THETA = 10000.0
GLA_HEADS = 4
GLA_DK = 128
GLA_DV = 256
GLA_KEY = GLA_HEADS * GLA_DK
GLA_VAL = GLA_HEADS * GLA_DV
GLA_LOWRANK = 16
GLA_TAU = 16.0
GLA_CHUNK = 64
N_EXPERTS = 16
N_GROUPS = 4
EXPERTS_PER_GROUP = N_EXPERTS // N_GROUPS
TOP_K = 2
D_FF_EXPERT = 1024
NORM_EPS = 1e-6

LANES = 128
MOE_BLOCK = 256
PROJ_TILE = 512

COL_QA = 0
COL_KA = A_WIDTH
COL_VA = 2 * A_WIDTH
COL_QB = 3 * A_WIDTH
COL_KB = COL_QB + GLA_KEY
COL_VB = COL_KB + GLA_KEY
COL_RB = COL_VB + GLA_VAL
COL_ALOW = COL_RB + GLA_VAL
COL_GATES = COL_ALOW + GLA_LOWRANK


def _params(vmem_mib, sem=None):
    kw = dict(vmem_limit_bytes=int(vmem_mib) << 20)
    if sem is not None:
        kw["dimension_semantics"] = sem
    return pltpu.CompilerParams(**kw)


def _sigmoid(x):
    return 0.5 * jnp.tanh(0.5 * x) + 0.5


def _ada_kernel(c_ref, w_ref, b_ref, o_ref):
    @pl.when(pl.program_id(1) == 0)
    def _():
        o_ref[0] = jnp.broadcast_to(b_ref[0], o_ref.shape[1:])

    c = c_ref[...]
    ca = (c * _sigmoid(c)).astype(BF16)
    o_ref[0] += jnp.dot(ca, w_ref[0].astype(BF16), preferred_element_type=F32)


def _ada(c, w_ada, b_ada):
    depth, d, n = w_ada.shape
    b = c.shape[0]
    tk = 256
    return pl.pallas_call(
        _ada_kernel,
        out_shape=jax.ShapeDtypeStruct((depth, b, n), F32),
        grid=(depth, d // tk),
        in_specs=[
            pl.BlockSpec((b, tk), lambda l, k: (0, k)),
            pl.BlockSpec((1, tk, n), lambda l, k: (l, k, 0)),
            pl.BlockSpec((1, 1, n), lambda l, k: (l, 0, 0)),
        ],
        out_specs=pl.BlockSpec((1, b, n), lambda l, k: (l, 0, 0)),
        compiler_params=_params(40, ("arbitrary", "arbitrary")),
        name="ada",
    )(c, w_ada, b_ada.reshape(depth, 1, n))


def _norm_mod_kernel(x_ref, ada_ref, gain_ref, wal_ref, h_ref, al_ref):
    x = x_ref[...]
    y = x * lax.rsqrt(jnp.mean(x * x, axis=-1, keepdims=True) + NORM_EPS) * gain_ref[...]
    h = (y * (1.0 + ada_ref[0, 1:2, :]) + ada_ref[0, 0:1, :]).astype(BF16)
    h_ref[...] = h
    al_ref[...] = lax.dot_general(h, wal_ref[...], (((1,), (1,)), ((), ())), preferred_element_type=F32)


def _norm_mod(x2, ada_l, gain, w_alow):
    t, d = x2.shape
    tm = 512
    per_b = SEQ // tm
    return pl.pallas_call(
        _norm_mod_kernel,
        out_shape=(jax.ShapeDtypeStruct((t, d), BF16), jax.ShapeDtypeStruct((t, LANES), F32)),
        grid=(t // tm,),
        in_specs=[
            pl.BlockSpec((tm, d), lambda i: (i, 0)),
            pl.BlockSpec((1, 6, d), lambda i: (i // per_b, 0, 0)),
            pl.BlockSpec((1, d), lambda i: (0, 0)),
            pl.BlockSpec((LANES, d), lambda i: (0, 0)),
        ],
        out_specs=(pl.BlockSpec((tm, d), lambda i: (i, 0)), pl.BlockSpec((tm, LANES), lambda i: (i, 0))),
        compiler_params=_params(40),
        name="norm_mod",
    )(x2, ada_l, gain, w_alow)


def _matmul_nt_kernel(a_ref, b_ref, *rest, n_side):
    side_in, o_ref, side_out = rest[:n_side], rest[n_side], rest[n_side + 1:]
    o_ref[...] = lax.dot_general(a_ref[...], b_ref[...].astype(BF16), (((1,), (1,)), ((), ())),
                                 preferred_element_type=F32).astype(o_ref.dtype)
    for s_ref, so_ref in zip(side_in, side_out):
        so_ref[...] = s_ref[...].astype(so_ref.dtype)


def _in_proj(h, w_t, layer, col0, n_cols, name, sides=()):
    t, d = h.shape
    tm, tn = 2048, PROJ_TILE
    sub = 8
    assert col0 % sub == 0 and tn % sub == 0
    w_spec = pl.BlockSpec((pl.Squeezed(), pl.Element(tn), pl.Element(d)),
                          lambda i, j: (layer, (col0 // sub + j * (tn // sub)) * sub, 0))
    n_j = n_cols // tn
    n_steps = (t // tm) * n_j
    bf16_rows = 16
    in_specs = [pl.BlockSpec((tm, d), lambda i, j: (i, 0)), w_spec]
    out_shape = [jax.ShapeDtypeStruct((t, n_cols), BF16)]
    out_specs = [pl.BlockSpec((tm, tn), lambda i, j: (i, j))]
    for s in sides:
        _, rows, cols = s.shape
        rb = -(-pl.cdiv(rows, n_steps) // bf16_rows) * bf16_rows
        last = pl.cdiv(rows, rb) - 1
        slab = lambda i, j, last=last: jnp.minimum(i * n_j + j, last)
        in_specs.append(pl.BlockSpec((pl.Squeezed(), rb, cols), lambda i, j, slab=slab: (layer, slab(i, j), 0)))
        out_shape.append(jax.ShapeDtypeStruct((rows, cols), BF16))
        out_specs.append(pl.BlockSpec((rb, cols), lambda i, j, slab=slab: (slab(i, j), 0)))
    return pl.pallas_call(
        functools.partial(_matmul_nt_kernel, n_side=len(sides)),
        out_shape=tuple(out_shape),
        grid=(t // tm, n_j),
        in_specs=in_specs,
        out_specs=tuple(out_specs),
        compiler_params=_params(56, ("arbitrary", "arbitrary")),
        name=name,
    )(h, w_t, *sides)


def _attn_kernel(q_ref, k_ref, v_ref, cos_ref, sin_ref, o_ref, lse_ref, q_s, k_s, v_s, o_s, l_s, *, dil, nb, wd):
    scale = HEAD_DIM ** -0.5
    blk = ATT_BLOCK
    nh = A_HEADS_PER_GROUP
    cosf = cos_ref[...]
    sinf = sin_ref[...]
    for h in range(nh):
        hc = slice(h * HEAD_DIM, (h + 1) * HEAD_DIM)
        tq = q_ref[:, hc].astype(F32)
        q_s[h] = (tq * cosf + pltpu.roll(tq, HEAD_DIM // 2, 1) * sinf) * scale
        tk = k_ref[:, hc].astype(F32)
        k_s[h] = tk * cosf + pltpu.roll(tk, HEAD_DIM // 2, 1) * sinf
        v_s[h] = v_ref[:, hc].astype(F32)

    def rows(start, size):
        return pl.ds(start, size) if dil == 1 else pl.ds(start, size, stride=dil)

    def head_mask(nk, offset):
        i = lax.broadcasted_iota(jnp.int32, (nh * blk, nk), 0) % blk
        j = lax.broadcasted_iota(jnp.int32, (nh * blk, nk), 1)
        dist = offset + i - j
        return (dist >= 0) & (dist <= wd)

    mask_first = head_mask(blk, 0)
    mask_later = head_mask(2 * blk, blk)
    lane = lax.broadcasted_iota(jnp.int32, (blk, LANES), 1)
    ones = jnp.ones((2 * blk, HEAD_DIM), BF16)

    def attend(qs, ks, nk, mask):
        s = jnp.concatenate(
            [lax.dot_general(q_s[h, rows(qs, blk), :].astype(BF16), k_s[h, rows(ks, nk), :].astype(BF16),
                             (((1,), (1,)), ((), ())), preferred_element_type=F32) for h in range(nh)], axis=0)
        s = jnp.where(mask, s, -jnp.inf)
        m = jnp.max(s, axis=-1, keepdims=True)
        p = jnp.exp(s - m).astype(BF16)
        lse_tile = jnp.zeros((blk, LANES), F32)
        for h in range(nh):
            v_aug = jnp.concatenate([v_s[h, rows(ks, nk), :].astype(BF16), ones[:nk]], axis=1)
            ov = jnp.dot(p[h * blk:(h + 1) * blk], v_aug, preferred_element_type=F32)
            den = ov[:, HEAD_DIM:]
            o_s[h, rows(qs, blk), :] = ov[:, :HEAD_DIM] / den
            lse_tile = jnp.where(lane == h, m[h * blk:(h + 1) * blk] + jnp.log(den), lse_tile)
        l_s[rows(qs, blk), :] = lse_tile

    def per_phase(ph, carry):
        attend(ph, ph, blk, mask_first)

        def per_blk(n, c):
            qs = ph + dil * blk * n
            ks = ph + dil * blk * (n - 1)
            if dil == 1:
                qs = pl.multiple_of(qs, blk)
                ks = pl.multiple_of(ks, blk)
            attend(qs, ks, 2 * blk, mask_later)
            return c

        if nb > 1:
            lax.fori_loop(1, nb, per_blk, 0, unroll=min(nb - 1, 5))
        return carry

    lax.fori_loop(0, dil, per_phase, 0, unroll=max(1, min(dil, 8 // nb)))
    for h in range(nh):
        o_ref[:, h * HEAD_DIM:(h + 1) * HEAD_DIM] = o_s[h].astype(o_ref.dtype)
    lse_ref[...] = l_s[...]


def _attention(proj, cosf, sinf, group):
    window, dil = DILATED_GROUPS[group]
    t = proj.shape[0]
    nb = (SEQ // dil) // ATT_BLOCK
    col = lambda base: base // A_OUT + group
    kern = functools.partial(_attn_kernel, dil=dil, nb=nb, wd=window // dil)
    return pl.pallas_call(
        kern,
        out_shape=(jax.ShapeDtypeStruct((t, A_OUT), BF16), jax.ShapeDtypeStruct((t, LANES), F32)),
        grid=(t // SEQ,),
        in_specs=[
            pl.BlockSpec((SEQ, A_OUT), lambda b: (b, col(COL_QA))),
            pl.BlockSpec((SEQ, A_OUT), lambda b: (b, col(COL_KA))),
            pl.BlockSpec((SEQ, A_OUT), lambda b: (b, col(COL_VA))),
            pl.BlockSpec((SEQ, HEAD_DIM), lambda b: (0, 0)),
            pl.BlockSpec((SEQ, HEAD_DIM), lambda b: (0, 0)),
        ],
        out_specs=(pl.BlockSpec((SEQ, A_OUT), lambda b: (b, 0)), pl.BlockSpec((SEQ, LANES), lambda b: (b, 0))),
        scratch_shapes=[pltpu.VMEM((A_HEADS_PER_GROUP, SEQ, HEAD_DIM), F32) for _ in range(4)]
        + [pltpu.VMEM((SEQ, LANES), F32)],
        compiler_params=_params(48),
        name=f"dilated_attn_g{group}",
    )(proj, proj, proj, cosf, sinf)


def _gla_kernel(q_ref, k_ref, v0_ref, v1_ref, r0_ref, r1_ref, al_ref, wa_ref, ba_ref, gain_ref, y_ref, st_ref, *, rows):
    @pl.when(pl.program_id(1) == 0)
    def _():
        st_ref[...] = jnp.zeros_like(st_ref)

    ck = GLA_CHUNK
    nseq = q_ref.shape[0]
    half = GLA_HEADS // 2
    ri = lax.broadcasted_iota(jnp.int32, (rows, rows), 0)
    ci = lax.broadcasted_iota(jnp.int32, (rows, rows), 1)
    same = (ri // ck) == (ci // ck)
    tri = (same & (ri >= ci)).astype(BF16)
    tot = same.astype(BF16)

    def prepare(sq):
        z = jnp.dot(al_ref[sq].astype(BF16), wa_ref[...], preferred_element_type=F32) + ba_ref[...]
        log_a = (jnp.minimum(z, 0.0) - jnp.log(1.0 + jnp.exp(-jnp.abs(z)))) * (1.0 / GLA_TAU)
        la_hi = log_a.astype(BF16)
        la_lo = (log_a - la_hi.astype(F32)).astype(BF16)
        b = jnp.dot(tri, la_hi, preferred_element_type=F32) + jnp.dot(tri, la_lo, preferred_element_type=F32)
        b_end = jnp.dot(tot, la_hi, preferred_element_type=F32) + jnp.dot(tot, la_lo, preferred_element_type=F32)
        q = q_ref[sq].astype(F32) * (GLA_DK ** -0.5)
        k = k_ref[sq].astype(F32)
        return ((q * jnp.exp(b)).astype(BF16), (k * jnp.exp(-b)).astype(BF16),
                (k * jnp.exp(b_end - b)).astype(BF16), jnp.exp(b_end))

    prepared = [prepare(sq) for sq in range(nseq)]
    intra_mask = same & (ri >= ci)
    nt = (((1,), (1,)), ((), ()))
    v_refs = (v0_ref, v1_ref)
    r_refs = (r0_ref, r1_ref)
    for sq in range(nseq):
        q_dec, k_inv, k_end, decay = prepared[sq]
        for h in range(GLA_HEADS):
            hk = slice(h * GLA_DK, (h + 1) * GLA_DK)
            hv = slice(h * GLA_DV, (h + 1) * GLA_DV)
            hb = slice((h % half) * GLA_DV, (h % half + 1) * GLA_DV)
            v = v_refs[h // half][sq, :, hb]
            attn = lax.dot_general(q_dec[:, hk], k_inv[:, hk], nt, preferred_element_type=F32)
            attn = jnp.where(intra_mask, attn, 0.0).astype(BF16)
            o = jnp.dot(attn, v, preferred_element_type=F32)
            st = st_ref[sq * GLA_HEADS + h]
            inter = []
            for c in range(rows // ck):
                rs = slice(c * ck, (c + 1) * ck)
                inter.append(lax.dot_general(q_dec[rs, hk], st.astype(BF16), nt, preferred_element_type=F32))
                v_t = v[rs].astype(F32).T.astype(BF16)
                st = st * decay[c * ck:c * ck + 1, hk] + jnp.dot(v_t, k_end[rs, hk], preferred_element_type=F32)
            st_ref[sq * GLA_HEADS + h] = st
            o = o + jnp.concatenate(inter, axis=0)
            o = o * lax.rsqrt(jnp.mean(o * o, axis=-1, keepdims=True) + NORM_EPS) * gain_ref[:, hv]
            r = r_refs[h // half][sq, :, hb].astype(F32)
            y_ref[sq, :, hv] = (o * (r * _sigmoid(r))).astype(y_ref.dtype)


def _gla(proj, a_low, w_alpha2p, b_alpha2, gla_gain):
    t, width = proj.shape
    nb = t // SEQ
    rows = 256
    nseq = 2 if nb % 2 == 0 else 1
    kern = functools.partial(_gla_kernel, rows=rows)
    col = lambda base, j=0: (lambda b, s: (b, s, base // GLA_KEY + j))
    blk = lambda w: (nseq, rows, w)
    proj3 = proj.reshape(nb, SEQ, width)
    y = pl.pallas_call(
        kern,
        out_shape=jax.ShapeDtypeStruct((nb, SEQ, GLA_VAL), BF16),
        grid=(nb // nseq, SEQ // rows),
        in_specs=[
            pl.BlockSpec(blk(GLA_KEY), col(COL_QB)),
            pl.BlockSpec(blk(GLA_KEY), col(COL_KB)),
            pl.BlockSpec(blk(GLA_KEY), col(COL_VB, 0)),
            pl.BlockSpec(blk(GLA_KEY), col(COL_VB, 1)),
            pl.BlockSpec(blk(GLA_KEY), col(COL_RB, 0)),
            pl.BlockSpec(blk(GLA_KEY), col(COL_RB, 1)),
            pl.BlockSpec(blk(LANES), lambda b, s: (b, s, 0)),
            pl.BlockSpec((LANES, GLA_KEY), lambda b, s: (0, 0)),
            pl.BlockSpec((1, GLA_KEY), lambda b, s: (0, 0)),
            pl.BlockSpec((1, GLA_VAL), lambda b, s: (0, 0)),
        ],
        out_specs=pl.BlockSpec(blk(GLA_VAL), lambda b, s: (b, s, 0)),
        scratch_shapes=[pltpu.VMEM((nseq * GLA_HEADS, GLA_DV, GLA_DK), F32)],
        compiler_params=_params(32),
        name="gla",
    )(proj3, proj3, proj3, proj3, proj3, proj3, a_low.reshape(nb, SEQ, LANES), w_alpha2p, b_alpha2, gla_gain)
    return y.reshape(t, GLA_VAL)


def _post_mixer_kernel(x_ref, ga_ref, gb_ref, o0_ref, o1_ref, o2_ref, l0_ref, l1_ref, l2_ref, yb_ref, ada_ref,
                       woa_ref, wob_ref, wout_ref, g2_ref, wr_ref, br_ref,
                       xo_ref, h2_ref, e_ref, w_ref, merged_s):
    @pl.when(pl.program_id(0) == 0)
    def _():
        merged_s[...] = jnp.zeros_like(merged_s)

    def stage_a():
        ls = [l0_ref[...], l1_ref[...], l2_ref[...]]
        os_ = [o0_ref, o1_ref, o2_ref]
        mx = jnp.maximum(jnp.maximum(ls[0], ls[1]), ls[2])
        es = [jnp.exp(l - mx) for l in ls]
        zs = es[0] + es[1] + es[2]
        wt = [e / zs for e in es]
        parts = []
        for h in range(A_HEADS_PER_GROUP):
            hc = slice(h * HEAD_DIM, (h + 1) * HEAD_DIM)
            acc = wt[0][:, h:h + 1] * os_[0][:, hc].astype(F32)
            acc = acc + wt[1][:, h:h + 1] * os_[1][:, hc].astype(F32)
            acc = acc + wt[2][:, h:h + 1] * os_[2][:, hc].astype(F32)
            parts.append(acc)
        ya = jnp.concatenate(parts, axis=1).astype(BF16)
        ta = jnp.dot(ya, woa_ref[0], preferred_element_type=F32)
        tb = jnp.dot(yb_ref[...], wob_ref[0], preferred_element_type=F32)
        merged = _sigmoid(ga_ref[...].astype(F32)) * ta + _sigmoid(gb_ref[...].astype(F32)) * tb
        merged_s[...] = merged.astype(BF16)

    mix = jnp.dot(merged_s[...], wout_ref[0], preferred_element_type=F32)
    xn = x_ref[...] + ada_ref[0, 2:3, :] * mix
    xo_ref[...] = xn
    y = xn * lax.rsqrt(jnp.mean(xn * xn, axis=-1, keepdims=True) + NORM_EPS) * g2_ref[...]
    h2 = y * (1.0 + ada_ref[0, 4:5, :]) + ada_ref[0, 3:4, :]
    h2_ref[...] = h2

    logits = lax.dot_general(wr_ref[...], h2.astype(BF16), (((1,), (1,)), ((), ())), preferred_element_type=F32)
    tm = logits.shape[1]
    aff_all = _sigmoid(logits)
    sel_all = aff_all + br_ref[...]
    sel = [sel_all[e:e + 1, :] for e in range(N_EXPERTS)]
    aff = [aff_all[e:e + 1, :] for e in range(N_EXPERTS)]
    neg = -jnp.inf

    def top2_sum(v0, v1, v2, v3):
        hi01, lo01 = jnp.maximum(v0, v1), jnp.minimum(v0, v1)
        hi23, lo23 = jnp.maximum(v2, v3), jnp.minimum(v2, v3)
        return jnp.maximum(hi01, hi23) + jnp.maximum(jnp.minimum(hi01, hi23), jnp.maximum(lo01, lo23))

    epg = EXPERTS_PER_GROUP
    scores = [top2_sum(*sel[g * epg:(g + 1) * epg]) for g in range(N_GROUPS)]
    best = jnp.zeros_like(scores[0])
    best_s = scores[0]
    for g in range(1, N_GROUPS):
        better = scores[g] > best_s
        best = jnp.where(better, float(g), best)
        best_s = jnp.where(better, scores[g], best_s)
    vals, affs = [], []
    for j in range(epg):
        vj, aj = sel[j], aff[j]
        for g in range(1, N_GROUPS):
            in_g = best == float(g)
            vj = jnp.where(in_g, sel[g * epg + j], vj)
            aj = jnp.where(in_g, aff[g * epg + j], aj)
        vals.append(vj)
        affs.append(aj)

    def first_argmax(vs):
        idx, m, a = jnp.zeros_like(vs[0]), vs[0], affs[0]
        for j in range(1, epg):
            gt = vs[j] > m
            idx = jnp.where(gt, float(j), idx)
            m = jnp.where(gt, vs[j], m)
            a = jnp.where(gt, affs[j], a)
        return idx, a

    i1, a1 = first_argmax(vals)
    i2, a2 = first_argmax([jnp.where(i1 == float(j), neg, vals[j]) for j in range(epg)])
    tot = a1 + a2
    srow = lax.broadcasted_iota(jnp.int32, (8, tm), 0)
    packed = jnp.where(srow == 0, best * epg + i1,
                       jnp.where(srow == 1, best * epg + i2,
                                 jnp.where(srow == 2, a1 / tot, jnp.where(srow == 3, a2 / tot, 0.0))))
    packed_t = jnp.concatenate([packed, jnp.zeros((LANES - 8, tm), F32)], axis=0).T
    e_ref[...] = packed_t[:, 0:TOP_K].astype(jnp.int32)
    w_ref[...] = packed_t[:, TOP_K:2 * TOP_K]
    stage_a()


def _post_mixer(x2, gates, layer, o_list, l_list, y_b, ada_l, woa, wob, wout, gain2, w_router_t, b_router_c):
    t, d = x2.shape
    tm = 256
    per_b = SEQ // tm
    const = lambda shape: pl.BlockSpec(shape, lambda i: (0,) * len(shape), pipeline_mode=pl.Buffered(1))
    wblk = lambda k: pl.BlockSpec((1, k, d), lambda i: (layer, 0, 0), pipeline_mode=pl.Buffered(1))
    n = t // tm
    cur = lambda i: jnp.minimum(i, n - 1)
    prev = lambda i: jnp.maximum(i - 1, 0)
    a_blk = lambda w, col=0: pl.BlockSpec((tm, w), lambda i: (cur(i), col))
    b_blk = lambda w: pl.BlockSpec((tm, w), lambda i: (prev(i), 0))
    return pl.pallas_call(
        _post_mixer_kernel,
        out_shape=(
            jax.ShapeDtypeStruct((t, d), F32),
            jax.ShapeDtypeStruct((t, d), F32),
            jax.ShapeDtypeStruct((t, TOP_K), jnp.int32),
            jax.ShapeDtypeStruct((t, TOP_K), F32),
        ),
        grid=(n + 1,),
        in_specs=[
            b_blk(d),
            a_blk(d, 0),
            a_blk(d, 1),
            a_blk(A_OUT), a_blk(A_OUT), a_blk(A_OUT),
            a_blk(LANES), a_blk(LANES), a_blk(LANES),
            a_blk(GLA_VAL),
            pl.BlockSpec((1, 6, d), lambda i: (prev(i) // per_b, 0, 0)),
            wblk(A_OUT), wblk(GLA_VAL), wblk(d),
            const((1, d)), const((LANES, d)), const((LANES, 1)),
        ],
        out_specs=(b_blk(d), b_blk(d), b_blk(TOP_K), b_blk(TOP_K)),
        scratch_shapes=[pltpu.VMEM((tm, d), BF16)],
        compiler_params=_params(56, ("arbitrary",)),
        name="post_mixer",
    )(x2, gates, gates, *o_list, *l_list, y_b, ada_l, woa, wob, wout, gain2, w_router_t, b_router_c)


def _dispatch_kernel(dest_ref, cnt_ref, h_ref, x_hbm, sem, *, tm):
    i = pl.program_id(0)
    n_rows = x_hbm.shape[0]

    def copy(r, row):
        return pltpu.make_async_copy(h_ref.at[pl.ds(r, 1), :], x_hbm.at[pl.ds(row, 1), :], sem)

    def rows(start):
        def body(r, carry):
            for k in range(TOP_K):
                cp = copy(r, dest_ref[(i * tm + r) * TOP_K + k])
                if start:
                    cp.start(priority=k % 2)
                else:
                    cp.wait()
            return carry
        if start:
            for r in range(tm):
                body(r, 0)
        else:
            lax.fori_loop(0, tm, body, 0, unroll=4)

    def fill(lo, hi, start):
        def body(r, carry):
            cp = copy(0, r)
            if start:
                cp.start()
            else:
                cp.wait()
            return carry
        lax.fori_loop(lo, hi, body, 0)

    def pads(start):
        def per_expert(e, base):
            cnt = cnt_ref[e]
            padded = (cnt + (MOE_BLOCK - 1)) // MOE_BLOCK * MOE_BLOCK
            fill(base + cnt, base + padded, start)
            return base + padded
        end = lax.fori_loop(0, N_EXPERTS, per_expert, 0)
        fill(end, n_rows, start)

    rows(True)

    @pl.when(i == 0)
    def _():
        pads(True)
        pads(False)

    rows(False)


def _dispatch(dest, counts, h2, n_rows):
    t, d = h2.shape
    tm = 256
    grid_spec = pltpu.PrefetchScalarGridSpec(
        num_scalar_prefetch=2,
        grid=(t // tm,),
        in_specs=[pl.BlockSpec((tm, d), lambda i, dst, cnt: (i, 0))],
        out_specs=pl.BlockSpec(memory_space=pl.ANY),
        scratch_shapes=[pltpu.SemaphoreType.DMA(())],
    )
    return pl.pallas_call(
        functools.partial(_dispatch_kernel, tm=tm),
        out_shape=jax.ShapeDtypeStruct((n_rows, d), h2.dtype),
        grid_spec=grid_spec,
        compiler_params=_params(16, ("arbitrary",)),
        name="moe_dispatch",
    )(dest, counts, h2)


def _moe_kernel(be_ref, nu_ref, x_ref, wg_ref, wu_ref, wd_ref, y_ref):
    i = pl.program_id(0)
    n_used = nu_ref[0]

    @pl.when(i < n_used)
    def _():
        xb = x_ref[...].astype(BF16)
        g = jnp.dot(xb, wg_ref[0], preferred_element_type=F32)
        u = jnp.dot(xb, wu_ref[0], preferred_element_type=F32)
        a = (g * _sigmoid(g) * u).astype(BF16)
        y_ref[...] = jnp.dot(a, wd_ref[0], preferred_element_type=F32)

    @pl.when(i >= n_used)
    def _():
        y_ref[...] = jnp.zeros_like(y_ref)


def _moe_experts(blk_e, n_used, xbuf, wg, wu, wd):
    n_rows, d = xbuf.shape
    n_blk = blk_e.shape[0]
    f = wg.shape[2]
    grid_spec = pltpu.PrefetchScalarGridSpec(
        num_scalar_prefetch=2,
        grid=(n_blk,),
        in_specs=[
            pl.BlockSpec((MOE_BLOCK, d), lambda i, be, nu: (jnp.minimum(i, nu[0] - 1), 0)),
            pl.BlockSpec((1, d, f), lambda i, be, nu: (be[i], 0, 0)),
            pl.BlockSpec((1, d, f), lambda i, be, nu: (be[i], 0, 0)),
            pl.BlockSpec((1, f, d), lambda i, be, nu: (be[i], 0, 0)),
        ],
        out_specs=pl.BlockSpec((MOE_BLOCK, d), lambda i, be, nu: (i, 0)),
    )
    return pl.pallas_call(
        _moe_kernel,
        out_shape=jax.ShapeDtypeStruct((n_rows, d), F32),
        grid_spec=grid_spec,
        compiler_params=_params(48, ("arbitrary",)),
        name="moe_experts",
    )(blk_e, n_used, xbuf, wg, wu, wd)


def _pair_copy(src_hbm, row, dst, slot, k, r, sem):
    return pltpu.make_async_copy(src_hbm.at[pl.ds(row, 1), :], dst.at[slot, k, pl.ds(r, 1), :], sem.at[slot])


def _combine_kernel(pos_ref, y_hbm, x_ref, w_ref, ada_ref, gain_ref, adan_ref, wal_ref, *rest, tm, final):
    if final:
        o_ref, yg, sem = rest
    else:
        o_ref, h_ref, al_ref, yg, sem = rest
    i = pl.program_id(0)
    n = pl.num_programs(0)
    slot = i % 2

    def gather(tile, s, start, static_rows=False):
        def body(r, c):
            for k in range(TOP_K):
                cp = _pair_copy(y_hbm, pos_ref[(tile * tm + r) * TOP_K + k], yg, s, k, r, sem)
                if start:
                    cp.start(priority=k % 2)
                else:
                    cp.wait()
            return c
        if static_rows:
            for r in range(tm):
                body(r, 0)
        else:
            lax.fori_loop(0, tm, body, 0, unroll=8)

    @pl.when(i == 0)
    def _():
        gather(0, 0, True)

    @pl.when(i + 1 < n)
    def _():
        gather(i + 1, 1 - slot, True, static_rows=True)

    gather(i, slot, False)
    w = w_ref[...]
    y = w[:, 0:1] * yg[slot, 0] + w[:, 1:2] * yg[slot, 1]
    out = x_ref[...] + ada_ref[0, 5:6, :] * y
    normed = out * lax.rsqrt(jnp.mean(out * out, axis=-1, keepdims=True) + NORM_EPS) * gain_ref[...]
    if final:
        o_ref[...] = normed
    else:
        o_ref[...] = out
        h = (normed * (1.0 + adan_ref[0, 1:2, :]) + adan_ref[0, 0:1, :]).astype(BF16)
        h_ref[...] = h
        al_ref[...] = lax.dot_general(h, wal_ref[...], (((1,), (1,)), ((), ())), preferred_element_type=F32)


def _combine(pos, ybuf, x2, gw, ada_l, gain, ada_next, w_alow_next, final):
    t, d = x2.shape
    tm = 256
    per_b = SEQ // tm
    kern = functools.partial(_combine_kernel, tm=tm, final=final)
    rowblk = lambda w: pl.BlockSpec((tm, w), lambda i, p: (i, 0))
    out_shape = [jax.ShapeDtypeStruct((t, d), F32)]
    out_specs = [rowblk(d)]
    if not final:
        out_shape += [jax.ShapeDtypeStruct((t, d), BF16), jax.ShapeDtypeStruct((t, LANES), F32)]
        out_specs += [rowblk(d), rowblk(LANES)]
    grid_spec = pltpu.PrefetchScalarGridSpec(
        num_scalar_prefetch=1,
        grid=(t // tm,),
        in_specs=[
            pl.BlockSpec(memory_space=pl.ANY),
            rowblk(d),
            rowblk(TOP_K),
            pl.BlockSpec((1, 6, d), lambda i, p: (i // per_b, 0, 0)),
            pl.BlockSpec((1, d), lambda i, p: (0, 0)),
            pl.BlockSpec((1, 6, d), lambda i, p: (i // per_b, 0, 0)),
            pl.BlockSpec((LANES, d), lambda i, p: (0, 0)),
        ],
        out_specs=tuple(out_specs),
        scratch_shapes=[pltpu.VMEM((2, TOP_K, tm, d), F32), pltpu.SemaphoreType.DMA((2,))],
    )
    return pl.pallas_call(
        kern,
        out_shape=tuple(out_shape),
        grid_spec=grid_spec,
        compiler_params=_params(48, ("arbitrary",)),
        name="moe_combine_final" if final else "moe_combine",
    )(pos, ybuf, x2, gw, ada_l, gain, ada_next, w_alow_next)


def _plan_kernel(e_ref, dest_ref, cnt_ref, carry, pstart_s, *, tm):
    ph = pl.program_id(0)
    i = pl.program_id(1)
    lane = lax.broadcasted_iota(jnp.int32, (tm, LANES), 1)
    e = e_ref[...]
    oh0 = e[:, 0:1] == lane
    oh1 = e[:, 1:2] == lane
    oh = oh0.astype(F32) + oh1.astype(F32)
    tile_counts = jnp.sum(oh, axis=0, keepdims=True)

    @pl.when((ph == 0) & (i == 0))
    def _():
        carry[...] = jnp.zeros_like(carry)

    @pl.when(ph == 0)
    def _():
        carry[...] += tile_counts

    @pl.when((ph == 1) & (i == 0))
    def _():
        counts = carry[...]
        cnt_ref[...] = counts.astype(jnp.int32)
        nblk = jnp.floor((counts + (MOE_BLOCK - 1)) * (1.0 / MOE_BLOCK))
        r = lax.broadcasted_iota(jnp.int32, (LANES, LANES), 0)
        c = lax.broadcasted_iota(jnp.int32, (LANES, LANES), 1)
        upper = (r < c).astype(BF16)
        nb8 = jnp.broadcast_to(nblk, (8, LANES)).astype(BF16)
        pstart_s[...] = jnp.dot(nb8, upper, preferred_element_type=F32)[0:1, :] * float(MOE_BLOCK)
        carry[...] = jnp.zeros_like(carry)

    @pl.when(ph == 1)
    def _():
        r = lax.broadcasted_iota(jnp.int32, (tm, tm), 0)
        c = lax.broadcasted_iota(jnp.int32, (tm, tm), 1)
        lower = (r > c).astype(BF16)
        before = jnp.dot(lower, oh.astype(BF16), preferred_element_type=F32) + carry[...]
        base = before + pstart_s[...]
        d0 = jnp.sum(jnp.where(oh0, base, 0.0), axis=1, keepdims=True)
        d1 = jnp.sum(jnp.where(oh1, base, 0.0), axis=1, keepdims=True)
        col = lax.broadcasted_iota(jnp.int32, (tm, TOP_K), 1)
        dest_ref[...] = jnp.where(col == 0, d0, d1).astype(jnp.int32)
        carry[...] += tile_counts


def _dispatch_plan(eidx, n_blk):
    t = eidx.shape[0]
    tm = 512
    dest, counts = pl.pallas_call(
        functools.partial(_plan_kernel, tm=tm),
        out_shape=(jax.ShapeDtypeStruct((t, TOP_K), jnp.int32), jax.ShapeDtypeStruct((1, LANES), jnp.int32)),
        grid=(2, t // tm),
        in_specs=[pl.BlockSpec((tm, TOP_K), lambda p, i: (i, 0))],
        out_specs=(pl.BlockSpec((tm, TOP_K), lambda p, i: (i * p, 0)), pl.BlockSpec((1, LANES), lambda p, i: (0, 0))),
        scratch_shapes=[pltpu.VMEM((1, LANES), F32), pltpu.VMEM((1, LANES), F32)],
        compiler_params=_params(32, ("arbitrary", "arbitrary")),
        name="dispatch_plan",
    )(eidx)
    counts = counts[0, :N_EXPERTS]
    pcounts = (counts + MOE_BLOCK - 1) // MOE_BLOCK * MOE_BLOCK
    pend = jnp.cumsum(pcounts)
    blk_start = jnp.arange(n_blk, dtype=jnp.int32) * MOE_BLOCK
    blk_e = jnp.sum((pend[None, :] <= blk_start[:, None]).astype(jnp.int32), axis=1)
    blk_e = jnp.minimum(blk_e, N_EXPERTS - 1).astype(jnp.int32)
    n_used = (pend[-1] // MOE_BLOCK).astype(jnp.int32).reshape(1)
    return blk_e, n_used, dest.reshape(-1), counts


def _rope_tables():
    pos = jnp.arange(SEQ, dtype=F32)
    inv_freq = ROPE_THETA ** (-jnp.arange(0, HEAD_DIM, 2, dtype=F32) / HEAD_DIM)
    ang = pos[:, None] * inv_freq[None, :]
    cos, sin = jnp.cos(ang), jnp.sin(ang)
    return jnp.concatenate([cos, cos], axis=1), jnp.concatenate([-sin, sin], axis=1)


def kernel(x, c, w_ada, b_ada, norm_mix, norm_ffn, w_in, w_alpha2, b_alpha2, gla_gain, w_out_a, w_out_b, w_out,
           w_router, b_router, w_gate_e, w_up_e, w_down_e, final_norm):
    b, s, d = x.shape
    depth = w_ada.shape[0]
    t = b * s
    n_blk = (t * TOP_K) // MOE_BLOCK + N_EXPERTS
    cosf, sinf = _rope_tables()
    ada = _ada(c, w_ada, b_ada).reshape(depth, b, 6, d)
    w_router_t = jnp.pad(w_router.T, ((0, LANES - N_EXPERTS), (0, 0))).astype(BF16)
    b_router_c = jnp.pad(b_router, (0, LANES - N_EXPERTS)).reshape(LANES, 1)
    final_gain = final_norm.reshape(1, d)
    w_in_t = jnp.swapaxes(w_in, 1, 2)
    w_alow_all = jnp.pad(w_in_t[:, COL_ALOW:COL_GATES, :], ((0, 0), (0, LANES - GLA_LOWRANK), (0, 0))).astype(BF16)
    w_oa, w_ob, w_o = w_out_a.astype(BF16), w_out_b.astype(BF16), w_out.astype(BF16)
    n_exp, _, d_ff = w_gate_e.shape[1:]
    wg_rows = w_gate_e.reshape(depth, n_exp * d, d_ff)
    wu_rows = w_up_e.reshape(depth, n_exp * d, d_ff)
    wd_rows = w_down_e.reshape(depth, n_exp * d_ff, d)
    x2 = x.reshape(t, d)
    h, a_low = _norm_mod(x2, ada[0], norm_mix[0].reshape(1, d), w_alow_all[0])
    for l in range(depth):
        w_alpha2p = jnp.pad(w_alpha2[l], ((0, LANES - GLA_LOWRANK), (0, 0))).astype(BF16)
        proj, w_g, w_u = _in_proj(h, w_in_t, l, 0, COL_ALOW, "in_proj", sides=(wg_rows, wu_rows))
        gates, w_d = _in_proj(h, w_in_t, l, COL_GATES, 2 * D_MODEL, "in_proj_gates", sides=(wd_rows,))
        w_g, w_u, w_d = (w_g.reshape(n_exp, d, d_ff), w_u.reshape(n_exp, d, d_ff), w_d.reshape(n_exp, d_ff, d))
        o_list, l_list = [], []
        for g in range(len(DILATED_GROUPS)):
            o_g, l_g = _attention(proj, cosf, sinf, g)
            o_list.append(o_g)
            l_list.append(l_g)
        y_b = _gla(proj, a_low, w_alpha2p, b_alpha2[l].reshape(1, GLA_KEY), gla_gain[l].reshape(1, GLA_VAL))
        x2, h2, eidx, gw = _post_mixer(
            x2, gates, l, o_list, l_list, y_b, ada[l], w_oa, w_ob, w_o,
            norm_ffn[l].reshape(1, d), w_router_t, b_router_c)
        blk_e, n_used, dest, counts = _dispatch_plan(eidx, n_blk)
        xbuf = _dispatch(dest, counts, h2, n_blk * MOE_BLOCK)
        ybuf = _moe_experts(blk_e, n_used, xbuf, w_g, w_u, w_d)
        if l == depth - 1:
            (x2,) = _combine(dest, ybuf, x2, gw, ada[l], final_gain, ada[l], w_alow_all[l], final=True)
        else:
            x2, h, a_low = _combine(dest, ybuf, x2, gw, ada[l], norm_mix[l + 1].reshape(1, d), ada[l + 1],
                                    w_alow_all[l + 1], final=False)
    return x2.reshape(b, s, d)
```

```python
import functools

import jax
import jax.numpy as jnp
from jax import lax
from jax.experimental import pallas as pl
from jax.experimental.pallas import tpu as pltpu

F32 = jnp.float32
BF16 = jnp.bfloat16

D_MODEL = 2048
SEQ = 2048
HEAD_DIM = 128
A_HEADS_PER_GROUP = 4
DILATED_GROUPS = ((128, 1), (512, 4), (2048, 16))
A_WIDTH = 12 * HEAD_DIM
A_OUT = A_HEADS_PER_GROUP * HEAD_DIM
ATT_BLOCK = 128
ROPE_THETA = 10000.0
GLA_HEADS = 4
GLA_DK = 128
GLA_DV = 256
GLA_KEY = GLA_HEADS * GLA_DK
GLA_VAL = GLA_HEADS * GLA_DV
GLA_LOWRANK = 16
GLA_TAU = 16.0
GLA_CHUNK = 64
N_EXPERTS = 16
N_GROUPS = 4
EXPERTS_PER_GROUP = N_EXPERTS // N_GROUPS
TOP_K = 2
D_FF_EXPERT = 1024
NORM_EPS = 1e-6

LANES = 128
MOE_BLOCK = 256
PROJ_TILE = 512

COL_QA = 0
COL_KA = A_WIDTH
COL_VA = 2 * A_WIDTH
COL_QB = 3 * A_WIDTH
COL_KB = COL_QB + GLA_KEY
COL_VB = COL_KB + GLA_KEY
COL_RB = COL_VB + GLA_VAL
COL_ALOW = COL_RB + GLA_VAL
COL_GATES = COL_ALOW + GLA_LOWRANK


def _params(vmem_mib, sem=None):
    kw = dict(vmem_limit_bytes=int(vmem_mib) << 20)
    if sem is not None:
        kw["dimension_semantics"] = sem
    return pltpu.CompilerParams(**kw)


def _sigmoid(x):
    return 0.5 * jnp.tanh(0.5 * x) + 0.5


def _ada_kernel(c_ref, w_ref, b_ref, o_ref):
    @pl.when(pl.program_id(1) == 0)
    def _():
        o_ref[0] = jnp.broadcast_to(b_ref[0], o_ref.shape[1:])

    c = c_ref[...]
    ca = (c * _sigmoid(c)).astype(BF16)
    o_ref[0] += jnp.dot(ca, w_ref[0].astype(BF16), preferred_element_type=F32)


def _ada(c, w_ada, b_ada):
    depth, d, n = w_ada.shape
    b = c.shape[0]
    tk = 256
    return pl.pallas_call(
        _ada_kernel,
        out_shape=jax.ShapeDtypeStruct((depth, b, n), F32),
        grid=(depth, d // tk),
        in_specs=[
            pl.BlockSpec((b, tk), lambda l, k: (0, k)),
            pl.BlockSpec((1, tk, n), lambda l, k: (l, k, 0)),
            pl.BlockSpec((1, 1, n), lambda l, k: (l, 0, 0)),
        ],
        out_specs=pl.BlockSpec((1, b, n), lambda l, k: (l, 0, 0)),
        compiler_params=_params(40, ("arbitrary", "arbitrary")),
        name="ada",
    )(c, w_ada, b_ada.reshape(depth, 1, n))


def _norm_mod_kernel(x_ref, ada_ref, gain_ref, wal_ref, h_ref, al_ref):
    x = x_ref[...]
    y = x * lax.rsqrt(jnp.mean(x * x, axis=-1, keepdims=True) + NORM_EPS) * gain_ref[...]
    h = (y * (1.0 + ada_ref[0, 1:2, :]) + ada_ref[0, 0:1, :]).astype(BF16)
    h_ref[...] = h
    al_ref[...] = lax.dot_general(h, wal_ref[...], (((1,), (1,)), ((), ())), preferred_element_type=F32)


def _norm_mod(x2, ada_l, gain, w_alow):
    t, d = x2.shape
    tm = 512
    per_b = SEQ // tm
    return pl.pallas_call(
        _norm_mod_kernel,
        out_shape=(jax.ShapeDtypeStruct((t, d), BF16), jax.ShapeDtypeStruct((t, LANES), F32)),
        grid=(t // tm,),
        in_specs=[
            pl.BlockSpec((tm, d), lambda i: (i, 0)),
            pl.BlockSpec((1, 6, d), lambda i: (i // per_b, 0, 0)),
            pl.BlockSpec((1, d), lambda i: (0, 0)),
            pl.BlockSpec((LANES, d), lambda i: (0, 0)),
        ],
        out_specs=(pl.BlockSpec((tm, d), lambda i: (i, 0)), pl.BlockSpec((tm, LANES), lambda i: (i, 0))),
        compiler_params=_params(40),
        name="norm_mod",
    )(x2, ada_l, gain, w_alow)


def _matmul_nt_kernel(a_ref, b_ref, *rest, n_side):
    side_in, o_ref, side_out = rest[:n_side], rest[n_side], rest[n_side + 1:]
    o_ref[...] = lax.dot_general(a_ref[...], b_ref[...].astype(BF16), (((1,), (1,)), ((), ())),
                                 preferred_element_type=F32).astype(o_ref.dtype)
    for s_ref, so_ref in zip(side_in, side_out):
        so_ref[...] = s_ref[...].astype(so_ref.dtype)


def _in_proj(h, w_t, layer, col0, n_cols, name, sides=()):
    t, d = h.shape
    tm, tn = 2048, PROJ_TILE
    sub = 8
    assert col0 % sub == 0 and tn % sub == 0
    w_spec = pl.BlockSpec((pl.Squeezed(), pl.Element(tn), pl.Element(d)),
                          lambda i, j: (layer, (col0 // sub + j * (tn // sub)) * sub, 0))
    n_j = n_cols // tn
    n_steps = (t // tm) * n_j
    bf16_rows = 16
    in_specs = [pl.BlockSpec((tm, d), lambda i, j: (i, 0)), w_spec]
    out_shape = [jax.ShapeDtypeStruct((t, n_cols), BF16)]
    out_specs = [pl.BlockSpec((tm, tn), lambda i, j: (i, j))]
    for s in sides:
        _, rows, cols = s.shape
        rb = -(-pl.cdiv(rows, n_steps) // bf16_rows) * bf16_rows
        last = pl.cdiv(rows, rb) - 1
        slab = lambda i, j, last=last: jnp.minimum(i * n_j + j, last)
        in_specs.append(pl.BlockSpec((pl.Squeezed(), rb, cols), lambda i, j, slab=slab: (layer, slab(i, j), 0)))
        out_shape.append(jax.ShapeDtypeStruct((rows, cols), BF16))
        out_specs.append(pl.BlockSpec((rb, cols), lambda i, j, slab=slab: (slab(i, j), 0)))
    return pl.pallas_call(
        functools.partial(_matmul_nt_kernel, n_side=len(sides)),
        out_shape=tuple(out_shape),
        grid=(t // tm, n_j),
        in_specs=in_specs,
        out_specs=tuple(out_specs),
        compiler_params=_params(56, ("arbitrary", "arbitrary")),
        name=name,
    )(h, w_t, *sides)


def _attn_kernel(q_ref, k_ref, v_ref, cos_ref, sin_ref, o_ref, lse_ref, q_s, k_s, v_s, o_s, l_s, *, dil, nb, wd):
    scale = HEAD_DIM ** -0.5
    blk = ATT_BLOCK
    nh = A_HEADS_PER_GROUP
    cosf = cos_ref[...]
    sinf = sin_ref[...]
    for h in range(nh):
        hc = slice(h * HEAD_DIM, (h + 1) * HEAD_DIM)
        tq = q_ref[:, hc].astype(F32)
        q_s[h] = (tq * cosf + pltpu.roll(tq, HEAD_DIM // 2, 1) * sinf) * scale
        tk = k_ref[:, hc].astype(F32)
        k_s[h] = tk * cosf + pltpu.roll(tk, HEAD_DIM // 2, 1) * sinf
        v_s[h] = v_ref[:, hc].astype(F32)

    def rows(start, size):
        return pl.ds(start, size) if dil == 1 else pl.ds(start, size, stride=dil)

    def head_mask(nk, offset):
        i = lax.broadcasted_iota(jnp.int32, (nh * blk, nk), 0) % blk
        j = lax.broadcasted_iota(jnp.int32, (nh * blk, nk), 1)
        dist = offset + i - j
        return (dist >= 0) & (dist <= wd)

    mask_first = head_mask(blk, 0)
    mask_later = head_mask(2 * blk, blk)
    lane = lax.broadcasted_iota(jnp.int32, (blk, LANES), 1)
    ones = jnp.ones((2 * blk, HEAD_DIM), BF16)

    def attend(qs, ks, nk, mask):
        s = jnp.concatenate(
            [lax.dot_general(q_s[h, rows(qs, blk), :].astype(BF16), k_s[h, rows(ks, nk), :].astype(BF16),
                             (((1,), (1,)), ((), ())), preferred_element_type=F32) for h in range(nh)], axis=0)
        s = jnp.where(mask, s, -jnp.inf)
        m = jnp.max(s, axis=-1, keepdims=True)
        p = jnp.exp(s - m).astype(BF16)
        lse_tile = jnp.zeros((blk, LANES), F32)
        for h in range(nh):
            v_aug = jnp.concatenate([v_s[h, rows(ks, nk), :].astype(BF16), ones[:nk]], axis=1)
            ov = jnp.dot(p[h * blk:(h + 1) * blk], v_aug, preferred_element_type=F32)
            den = ov[:, HEAD_DIM:]
            o_s[h, rows(qs, blk), :] = ov[:, :HEAD_DIM] / den
            lse_tile = jnp.where(lane == h, m[h * blk:(h + 1) * blk] + jnp.log(den), lse_tile)
        l_s[rows(qs, blk), :] = lse_tile

    def per_phase(ph, carry):
        attend(ph, ph, blk, mask_first)

        def per_blk(n, c):
            qs = ph + dil * blk * n
            ks = ph + dil * blk * (n - 1)
            if dil == 1:
                qs = pl.multiple_of(qs, blk)
                ks = pl.multiple_of(ks, blk)
            attend(qs, ks, 2 * blk, mask_later)
            return c

        if nb > 1:
            lax.fori_loop(1, nb, per_blk, 0, unroll=min(nb - 1, 5))
        return carry

    lax.fori_loop(0, dil, per_phase, 0, unroll=max(1, min(dil, 8 // nb)))
    for h in range(nh):
        o_ref[:, h * HEAD_DIM:(h + 1) * HEAD_DIM] = o_s[h].astype(o_ref.dtype)
    lse_ref[...] = l_s[...]


def _attention(proj, cosf, sinf, group):
    window, dil = DILATED_GROUPS[group]
    t = proj.shape[0]
    nb = (SEQ // dil) // ATT_BLOCK
    col = lambda base: base // A_OUT + group
    kern = functools.partial(_attn_kernel, dil=dil, nb=nb, wd=window // dil)
    return pl.pallas_call(
        kern,
        out_shape=(jax.ShapeDtypeStruct((t, A_OUT), BF16), jax.ShapeDtypeStruct((t, LANES), F32)),
        grid=(t // SEQ,),
        in_specs=[
            pl.BlockSpec((SEQ, A_OUT), lambda b: (b, col(COL_QA))),
            pl.BlockSpec((SEQ, A_OUT), lambda b: (b, col(COL_KA))),
            pl.BlockSpec((SEQ, A_OUT), lambda b: (b, col(COL_VA))),
            pl.BlockSpec((SEQ, HEAD_DIM), lambda b: (0, 0)),
            pl.BlockSpec((SEQ, HEAD_DIM), lambda b: (0, 0)),
        ],
        out_specs=(pl.BlockSpec((SEQ, A_OUT), lambda b: (b, 0)), pl.BlockSpec((SEQ, LANES), lambda b: (b, 0))),
        scratch_shapes=[pltpu.VMEM((A_HEADS_PER_GROUP, SEQ, HEAD_DIM), F32) for _ in range(4)]
        + [pltpu.VMEM((SEQ, LANES), F32)],
        compiler_params=_params(48),
        name=f"dilated_attn_g{group}",
    )(proj, proj, proj, cosf, sinf)


def _gla_kernel(q_ref, k_ref, v0_ref, v1_ref, r0_ref, r1_ref, al_ref, wa_ref, ba_ref, gain_ref, y_ref, st_ref, *, rows):
    @pl.when(pl.program_id(1) == 0)
    def _():
        st_ref[...] = jnp.zeros_like(st_ref)

    ck = GLA_CHUNK
    nseq = q_ref.shape[0]
    half = GLA_HEADS // 2
    ri = lax.broadcasted_iota(jnp.int32, (rows, rows), 0)
    ci = lax.broadcasted_iota(jnp.int32, (rows, rows), 1)
    same = (ri // ck) == (ci // ck)
    tri = (same & (ri >= ci)).astype(BF16)
    tot = same.astype(BF16)

    def prepare(sq):
        z = jnp.dot(al_ref[sq].astype(BF16), wa_ref[...], preferred_element_type=F32) + ba_ref[...]
        log_a = (jnp.minimum(z, 0.0) - jnp.log(1.0 + jnp.exp(-jnp.abs(z)))) * (1.0 / GLA_TAU)
        la_hi = log_a.astype(BF16)
        la_lo = (log_a - la_hi.astype(F32)).astype(BF16)
        b = jnp.dot(tri, la_hi, preferred_element_type=F32) + jnp.dot(tri, la_lo, preferred_element_type=F32)
        b_end = jnp.dot(tot, la_hi, preferred_element_type=F32) + jnp.dot(tot, la_lo, preferred_element_type=F32)
        q = q_ref[sq].astype(F32) * (GLA_DK ** -0.5)
        k = k_ref[sq].astype(F32)
        return ((q * jnp.exp(b)).astype(BF16), (k * jnp.exp(-b)).astype(BF16),
                (k * jnp.exp(b_end - b)).astype(BF16), jnp.exp(b_end))

    prepared = [prepare(sq) for sq in range(nseq)]
    intra_mask = same & (ri >= ci)
    nt = (((1,), (1,)), ((), ()))
    v_refs = (v0_ref, v1_ref)
    r_refs = (r0_ref, r1_ref)
    for sq in range(nseq):
        q_dec, k_inv, k_end, decay = prepared[sq]
        for h in range(GLA_HEADS):
            hk = slice(h * GLA_DK, (h + 1) * GLA_DK)
            hv = slice(h * GLA_DV, (h + 1) * GLA_DV)
            hb = slice((h % half) * GLA_DV, (h % half + 1) * GLA_DV)
            v = v_refs[h // half][sq, :, hb]
            attn = lax.dot_general(q_dec[:, hk], k_inv[:, hk], nt, preferred_element_type=F32)
            attn = jnp.where(intra_mask, attn, 0.0).astype(BF16)
            o = jnp.dot(attn, v, preferred_element_type=F32)
            st = st_ref[sq * GLA_HEADS + h]
            inter = []
            for c in range(rows // ck):
                rs = slice(c * ck, (c + 1) * ck)
                inter.append(lax.dot_general(q_dec[rs, hk], st.astype(BF16), nt, preferred_element_type=F32))
                v_t = v[rs].astype(F32).T.astype(BF16)
                st = st * decay[c * ck:c * ck + 1, hk] + jnp.dot(v_t, k_end[rs, hk], preferred_element_type=F32)
            st_ref[sq * GLA_HEADS + h] = st
            o = o + jnp.concatenate(inter, axis=0)
            o = o * lax.rsqrt(jnp.mean(o * o, axis=-1, keepdims=True) + NORM_EPS) * gain_ref[:, hv]
            r = r_refs[h // half][sq, :, hb].astype(F32)
            y_ref[sq, :, hv] = (o * (r * _sigmoid(r))).astype(y_ref.dtype)


def _gla(proj, a_low, w_alpha2p, b_alpha2, gla_gain):
    t, width = proj.shape
    nb = t // SEQ
    rows = 256
    nseq = 2 if nb % 2 == 0 else 1
    kern = functools.partial(_gla_kernel, rows=rows)
    col = lambda base, j=0: (lambda b, s: (b, s, base // GLA_KEY + j))
    blk = lambda w: (nseq, rows, w)
    proj3 = proj.reshape(nb, SEQ, width)
    y = pl.pallas_call(
        kern,
        out_shape=jax.ShapeDtypeStruct((nb, SEQ, GLA_VAL), BF16),
        grid=(nb // nseq, SEQ // rows),
        in_specs=[
            pl.BlockSpec(blk(GLA_KEY), col(COL_QB)),
            pl.BlockSpec(blk(GLA_KEY), col(COL_KB)),
            pl.BlockSpec(blk(GLA_KEY), col(COL_VB, 0)),
            pl.BlockSpec(blk(GLA_KEY), col(COL_VB, 1)),
            pl.BlockSpec(blk(GLA_KEY), col(COL_RB, 0)),
            pl.BlockSpec(blk(GLA_KEY), col(COL_RB, 1)),
            pl.BlockSpec(blk(LANES), lambda b, s: (b, s, 0)),
            pl.BlockSpec((LANES, GLA_KEY), lambda b, s: (0, 0)),
            pl.BlockSpec((1, GLA_KEY), lambda b, s: (0, 0)),
            pl.BlockSpec((1, GLA_VAL), lambda b, s: (0, 0)),
        ],
        out_specs=pl.BlockSpec(blk(GLA_VAL), lambda b, s: (b, s, 0)),
        scratch_shapes=[pltpu.VMEM((nseq * GLA_HEADS, GLA_DV, GLA_DK), F32)],
        compiler_params=_params(32),
        name="gla",
    )(proj3, proj3, proj3, proj3, proj3, proj3, a_low.reshape(nb, SEQ, LANES), w_alpha2p, b_alpha2, gla_gain)
    return y.reshape(t, GLA_VAL)


def _post_mixer_kernel(x_ref, ga_ref, gb_ref, o0_ref, o1_ref, o2_ref, l0_ref, l1_ref, l2_ref, yb_ref, ada_ref,
                       woa_ref, wob_ref, wout_ref, g2_ref, wr_ref, br_ref,
                       xo_ref, h2_ref, e_ref, w_ref, merged_s):
    @pl.when(pl.program_id(0) == 0)
    def _():
        merged_s[...] = jnp.zeros_like(merged_s)

    def stage_a():
        ls = [l0_ref[...], l1_ref[...], l2_ref[...]]
        os_ = [o0_ref, o1_ref, o2_ref]
        mx = jnp.maximum(jnp.maximum(ls[0], ls[1]), ls[2])
        es = [jnp.exp(l - mx) for l in ls]
        zs = es[0] + es[1] + es[2]
        wt = [e / zs for e in es]
        parts = []
        for h in range(A_HEADS_PER_GROUP):
            hc = slice(h * HEAD_DIM, (h + 1) * HEAD_DIM)
            acc = wt[0][:, h:h + 1] * os_[0][:, hc].astype(F32)
            acc = acc + wt[1][:, h:h + 1] * os_[1][:, hc].astype(F32)
            acc = acc + wt[2][:, h:h + 1] * os_[2][:, hc].astype(F32)
            parts.append(acc)
        ya = jnp.concatenate(parts, axis=1).astype(BF16)
        ta = jnp.dot(ya, woa_ref[0], preferred_element_type=F32)
        tb = jnp.dot(yb_ref[...], wob_ref[0], preferred_element_type=F32)
        merged = _sigmoid(ga_ref[...].astype(F32)) * ta + _sigmoid(gb_ref[...].astype(F32)) * tb
        merged_s[...] = merged.astype(BF16)

    mix = jnp.dot(merged_s[...], wout_ref[0], preferred_element_type=F32)
    xn = x_ref[...] + ada_ref[0, 2:3, :] * mix
    xo_ref[...] = xn
    y = xn * lax.rsqrt(jnp.mean(xn * xn, axis=-1, keepdims=True) + NORM_EPS) * g2_ref[...]
    h2 = y * (1.0 + ada_ref[0, 4:5, :]) + ada_ref[0, 3:4, :]
    h2_ref[...] = h2

    logits = lax.dot_general(wr_ref[...], h2.astype(BF16), (((1,), (1,)), ((), ())), preferred_element_type=F32)
    tm = logits.shape[1]
    aff_all = _sigmoid(logits)
    sel_all = aff_all + br_ref[...]
    sel = [sel_all[e:e + 1, :] for e in range(N_EXPERTS)]
    aff = [aff_all[e:e + 1, :] for e in range(N_EXPERTS)]
    neg = -jnp.inf

    def top2_sum(v0, v1, v2, v3):
        hi01, lo01 = jnp.maximum(v0, v1), jnp.minimum(v0, v1)
        hi23, lo23 = jnp.maximum(v2, v3), jnp.minimum(v2, v3)
        return jnp.maximum(hi01, hi23) + jnp.maximum(jnp.minimum(hi01, hi23), jnp.maximum(lo01, lo23))

    epg = EXPERTS_PER_GROUP
    scores = [top2_sum(*sel[g * epg:(g + 1) * epg]) for g in range(N_GROUPS)]
    best = jnp.zeros_like(scores[0])
    best_s = scores[0]
    for g in range(1, N_GROUPS):
        better = scores[g] > best_s
        best = jnp.where(better, float(g), best)
        best_s = jnp.where(better, scores[g], best_s)
    vals, affs = [], []
    for j in range(epg):
        vj, aj = sel[j], aff[j]
        for g in range(1, N_GROUPS):
            in_g = best == float(g)
            vj = jnp.where(in_g, sel[g * epg + j], vj)
            aj = jnp.where(in_g, aff[g * epg + j], aj)
        vals.append(vj)
        affs.append(aj)

    def first_argmax(vs):
        idx, m, a = jnp.zeros_like(vs[0]), vs[0], affs[0]
        for j in range(1, epg):
            gt = vs[j] > m
            idx = jnp.where(gt, float(j), idx)
            m = jnp.where(gt, vs[j], m)
            a = jnp.where(gt, affs[j], a)
        return idx, a

    i1, a1 = first_argmax(vals)
    i2, a2 = first_argmax([jnp.where(i1 == float(j), neg, vals[j]) for j in range(epg)])
    tot = a1 + a2
    srow = lax.broadcasted_iota(jnp.int32, (8, tm), 0)
    packed = jnp.where(srow == 0, best * epg + i1,
                       jnp.where(srow == 1, best * epg + i2,
                                 jnp.where(srow == 2, a1 / tot, jnp.where(srow == 3, a2 / tot, 0.0))))
    packed_t = jnp.concatenate([packed, jnp.zeros((LANES - 8, tm), F32)], axis=0).T
    e_ref[...] = packed_t[:, 0:TOP_K].astype(jnp.int32)
    w_ref[...] = packed_t[:, TOP_K:2 * TOP_K]
    stage_a()


def _post_mixer(x2, gates, layer, o_list, l_list, y_b, ada_l, woa, wob, wout, gain2, w_router_t, b_router_c):
    t, d = x2.shape
    tm = 256
    per_b = SEQ // tm
    const = lambda shape: pl.BlockSpec(shape, lambda i: (0,) * len(shape), pipeline_mode=pl.Buffered(1))
    wblk = lambda k: pl.BlockSpec((1, k, d), lambda i: (layer, 0, 0), pipeline_mode=pl.Buffered(1))
    n = t // tm
    cur = lambda i: jnp.minimum(i, n - 1)
    prev = lambda i: jnp.maximum(i - 1, 0)
    a_blk = lambda w, col=0: pl.BlockSpec((tm, w), lambda i: (cur(i), col))
    b_blk = lambda w: pl.BlockSpec((tm, w), lambda i: (prev(i), 0))
    return pl.pallas_call(
        _post_mixer_kernel,
        out_shape=(
            jax.ShapeDtypeStruct((t, d), F32),
            jax.ShapeDtypeStruct((t, d), F32),
            jax.ShapeDtypeStruct((t, TOP_K), jnp.int32),
            jax.ShapeDtypeStruct((t, TOP_K), F32),
        ),
        grid=(n + 1,),
        in_specs=[
            b_blk(d),
            a_blk(d, 0),
            a_blk(d, 1),
            a_blk(A_OUT), a_blk(A_OUT), a_blk(A_OUT),
            a_blk(LANES), a_blk(LANES), a_blk(LANES),
            a_blk(GLA_VAL),
            pl.BlockSpec((1, 6, d), lambda i: (prev(i) // per_b, 0, 0)),
            wblk(A_OUT), wblk(GLA_VAL), wblk(d),
            const((1, d)), const((LANES, d)), const((LANES, 1)),
        ],
        out_specs=(b_blk(d), b_blk(d), b_blk(TOP_K), b_blk(TOP_K)),
        scratch_shapes=[pltpu.VMEM((tm, d), BF16)],
        compiler_params=_params(56, ("arbitrary",)),
        name="post_mixer",
    )(x2, gates, gates, *o_list, *l_list, y_b, ada_l, woa, wob, wout, gain2, w_router_t, b_router_c)


def _dispatch_kernel(dest_ref, cnt_ref, h_ref, x_hbm, sem, *, tm):
    i = pl.program_id(0)
    n_rows = x_hbm.shape[0]

    def copy(r, row):
        return pltpu.make_async_copy(h_ref.at[pl.ds(r, 1), :], x_hbm.at[pl.ds(row, 1), :], sem)

    def rows(start):
        def body(r, carry):
            for k in range(TOP_K):
                cp = copy(r, dest_ref[(i * tm + r) * TOP_K + k])
                if start:
                    cp.start(priority=k % 2)
                else:
                    cp.wait()
            return carry
        if start:
            for r in range(tm):
                body(r, 0)
        else:
            lax.fori_loop(0, tm, body, 0, unroll=4)

    def fill(lo, hi, start):
        def body(r, carry):
            cp = copy(0, r)
            if start:
                cp.start()
            else:
                cp.wait()
            return carry
        lax.fori_loop(lo, hi, body, 0)

    def pads(start):
        def per_expert(e, base):
            cnt = cnt_ref[e]
            padded = (cnt + (MOE_BLOCK - 1)) // MOE_BLOCK * MOE_BLOCK
            fill(base + cnt, base + padded, start)
            return base + padded
        end = lax.fori_loop(0, N_EXPERTS, per_expert, 0)
        fill(end, n_rows, start)

    rows(True)

    @pl.when(i == 0)
    def _():
        pads(True)
        pads(False)

    rows(False)


def _dispatch(dest, counts, h2, n_rows):
    t, d = h2.shape
    tm = 256
    grid_spec = pltpu.PrefetchScalarGridSpec(
        num_scalar_prefetch=2,
        grid=(t // tm,),
        in_specs=[pl.BlockSpec((tm, d), lambda i, dst, cnt: (i, 0))],
        out_specs=pl.BlockSpec(memory_space=pl.ANY),
        scratch_shapes=[pltpu.SemaphoreType.DMA(())],
    )
    return pl.pallas_call(
        functools.partial(_dispatch_kernel, tm=tm),
        out_shape=jax.ShapeDtypeStruct((n_rows, d), h2.dtype),
        grid_spec=grid_spec,
        compiler_params=_params(16, ("arbitrary",)),
        name="moe_dispatch",
    )(dest, counts, h2)


def _moe_kernel(be_ref, nu_ref, ring_ref, x_ref, wg_hbm, wu_hbm, wd_hbm, y_ref, wg_s, wu_s, wd_s, sem):
    i = pl.program_id(0)
    n_used = nu_ref[0]
    n_blk = pl.num_programs(0)
    is_first, slot = ring_ref[i], ring_ref[n_blk + i]
    has_next, nxt_e = ring_ref[2 * n_blk + i], ring_ref[3 * n_blk + i]

    def weight_copies(e, s):
        return (pltpu.make_async_copy(wg_hbm.at[e], wg_s.at[s], sem.at[s, 0]),
                pltpu.make_async_copy(wu_hbm.at[e], wu_s.at[s], sem.at[s, 1]),
                pltpu.make_async_copy(wd_hbm.at[e], wd_s.at[s], sem.at[s, 2]))

    @pl.when(i == 0)
    def _():
        for cp in weight_copies(be_ref[0], 0):
            cp.start()

    @pl.when(is_first == 1)
    def _():
        for cp in weight_copies(be_ref[i], slot):
            cp.wait()

        @pl.when(has_next == 1)
        def _():
            for cp in weight_copies(nxt_e, 1 - slot):
                cp.start()

    @pl.when(i < n_used)
    def _():
        xb = x_ref[...].astype(BF16)
        g = jnp.dot(xb, wg_s[slot], preferred_element_type=F32)
        u = jnp.dot(xb, wu_s[slot], preferred_element_type=F32)
        a = (g * _sigmoid(g) * u).astype(BF16)
        y_ref[...] = jnp.dot(a, wd_s[slot], preferred_element_type=F32)

    @pl.when(i >= n_used)
    def _():
        y_ref[...] = jnp.zeros_like(y_ref)


def _moe_experts(blk_e, n_used, ring, xbuf, wg, wu, wd):
    n_rows, d = xbuf.shape
    n_blk = blk_e.shape[0]
    f = wg.shape[2]
    grid_spec = pltpu.PrefetchScalarGridSpec(
        num_scalar_prefetch=3,
        grid=(n_blk,),
        in_specs=[
            pl.BlockSpec((MOE_BLOCK, d), lambda i, be, nu, rg: (jnp.minimum(i, nu[0] - 1), 0)),
            pl.BlockSpec(memory_space=pl.ANY),
            pl.BlockSpec(memory_space=pl.ANY),
            pl.BlockSpec(memory_space=pl.ANY),
        ],
        out_specs=pl.BlockSpec((MOE_BLOCK, d), lambda i, be, nu, rg: (i, 0)),
        scratch_shapes=[pltpu.VMEM((2, d, f), BF16), pltpu.VMEM((2, d, f), BF16), pltpu.VMEM((2, f, d), BF16),
                        pltpu.SemaphoreType.DMA((2, 3))],
    )
    return pl.pallas_call(
        _moe_kernel,
        out_shape=jax.ShapeDtypeStruct((n_rows, d), F32),
        grid_spec=grid_spec,
        compiler_params=_params(48, ("arbitrary",)),
        name="moe_experts",
    )(blk_e, n_used, ring, xbuf, wg, wu, wd)


def _pair_copy(src_hbm, row, dst, slot, k, r, sem):
    return pltpu.make_async_copy(src_hbm.at[pl.ds(row, 1), :], dst.at[slot, k, pl.ds(r, 1), :], sem.at[slot])


def _combine_kernel(pos_ref, y_hbm, x_ref, w_ref, ada_ref, gain_ref, adan_ref, wal_ref, *rest, tm, final):
    if final:
        o_ref, yg, sem = rest
    else:
        o_ref, h_ref, al_ref, yg, sem = rest
    i = pl.program_id(0)
    n = pl.num_programs(0)
    slot = i % 2

    def gather(tile, s, start, static_rows=False):
        def body(r, c):
            for k in range(TOP_K):
                cp = _pair_copy(y_hbm, pos_ref[(tile * tm + r) * TOP_K + k], yg, s, k, r, sem)
                if start:
                    cp.start(priority=k % 2)
                else:
                    cp.wait()
            return c
        if static_rows:
            for r in range(tm):
                body(r, 0)
        else:
            lax.fori_loop(0, tm, body, 0, unroll=8)

    @pl.when(i == 0)
    def _():
        gather(0, 0, True)

    @pl.when(i + 1 < n)
    def _():
        gather(i + 1, 1 - slot, True, static_rows=True)

    gather(i, slot, False)
    w = w_ref[...]
    y = w[:, 0:1] * yg[slot, 0] + w[:, 1:2] * yg[slot, 1]
    out = x_ref[...] + ada_ref[0, 5:6, :] * y
    normed = out * lax.rsqrt(jnp.mean(out * out, axis=-1, keepdims=True) + NORM_EPS) * gain_ref[...]
    if final:
        o_ref[...] = normed
    else:
        o_ref[...] = out
        h = (normed * (1.0 + adan_ref[0, 1:2, :]) + adan_ref[0, 0:1, :]).astype(BF16)
        h_ref[...] = h
        al_ref[...] = lax.dot_general(h, wal_ref[...], (((1,), (1,)), ((), ())), preferred_element_type=F32)


def _combine(pos, ybuf, x2, gw, ada_l, gain, ada_next, w_alow_next, final):
    t, d = x2.shape
    tm = 256
    per_b = SEQ // tm
    kern = functools.partial(_combine_kernel, tm=tm, final=final)
    rowblk = lambda w: pl.BlockSpec((tm, w), lambda i, p: (i, 0))
    out_shape = [jax.ShapeDtypeStruct((t, d), F32)]
    out_specs = [rowblk(d)]
    if not final:
        out_shape += [jax.ShapeDtypeStruct((t, d), BF16), jax.ShapeDtypeStruct((t, LANES), F32)]
        out_specs += [rowblk(d), rowblk(LANES)]
    grid_spec = pltpu.PrefetchScalarGridSpec(
        num_scalar_prefetch=1,
        grid=(t // tm,),
        in_specs=[
            pl.BlockSpec(memory_space=pl.ANY),
            rowblk(d),
            rowblk(TOP_K),
            pl.BlockSpec((1, 6, d), lambda i, p: (i // per_b, 0, 0)),
            pl.BlockSpec((1, d), lambda i, p: (0, 0)),
            pl.BlockSpec((1, 6, d), lambda i, p: (i // per_b, 0, 0)),
            pl.BlockSpec((LANES, d), lambda i, p: (0, 0)),
        ],
        out_specs=tuple(out_specs),
        scratch_shapes=[pltpu.VMEM((2, TOP_K, tm, d), F32), pltpu.SemaphoreType.DMA((2,))],
    )
    return pl.pallas_call(
        kern,
        out_shape=tuple(out_shape),
        grid_spec=grid_spec,
        compiler_params=_params(48, ("arbitrary",)),
        name="moe_combine_final" if final else "moe_combine",
    )(pos, ybuf, x2, gw, ada_l, gain, ada_next, w_alow_next)


def _plan_kernel(e_ref, dest_ref, cnt_ref, carry, pstart_s, *, tm):
    ph = pl.program_id(0)
    i = pl.program_id(1)
    lane = lax.broadcasted_iota(jnp.int32, (tm, LANES), 1)
    e = e_ref[...]
    oh0 = e[:, 0:1] == lane
    oh1 = e[:, 1:2] == lane
    oh = oh0.astype(F32) + oh1.astype(F32)
    tile_counts = jnp.sum(oh, axis=0, keepdims=True)

    @pl.when((ph == 0) & (i == 0))
    def _():
        carry[...] = jnp.zeros_like(carry)

    @pl.when(ph == 0)
    def _():
        carry[...] += tile_counts

    @pl.when((ph == 1) & (i == 0))
    def _():
        counts = carry[...]
        cnt_ref[...] = counts.astype(jnp.int32)
        nblk = jnp.floor((counts + (MOE_BLOCK - 1)) * (1.0 / MOE_BLOCK))
        r = lax.broadcasted_iota(jnp.int32, (LANES, LANES), 0)
        c = lax.broadcasted_iota(jnp.int32, (LANES, LANES), 1)
        upper = (r < c).astype(BF16)
        nb8 = jnp.broadcast_to(nblk, (8, LANES)).astype(BF16)
        pstart_s[...] = jnp.dot(nb8, upper, preferred_element_type=F32)[0:1, :] * float(MOE_BLOCK)
        carry[...] = jnp.zeros_like(carry)

    @pl.when(ph == 1)
    def _():
        r = lax.broadcasted_iota(jnp.int32, (tm, tm), 0)
        c = lax.broadcasted_iota(jnp.int32, (tm, tm), 1)
        lower = (r > c).astype(BF16)
        before = jnp.dot(lower, oh.astype(BF16), preferred_element_type=F32) + carry[...]
        base = before + pstart_s[...]
        d0 = jnp.sum(jnp.where(oh0, base, 0.0), axis=1, keepdims=True)
        d1 = jnp.sum(jnp.where(oh1, base, 0.0), axis=1, keepdims=True)
        col = lax.broadcasted_iota(jnp.int32, (tm, TOP_K), 1)
        dest_ref[...] = jnp.where(col == 0, d0, d1).astype(jnp.int32)
        carry[...] += tile_counts


def _dispatch_plan(eidx, n_blk):
    t = eidx.shape[0]
    tm = 512
    dest, counts = pl.pallas_call(
        functools.partial(_plan_kernel, tm=tm),
        out_shape=(jax.ShapeDtypeStruct((t, TOP_K), jnp.int32), jax.ShapeDtypeStruct((1, LANES), jnp.int32)),
        grid=(2, t // tm),
        in_specs=[pl.BlockSpec((tm, TOP_K), lambda p, i: (i, 0))],
        out_specs=(pl.BlockSpec((tm, TOP_K), lambda p, i: (i * p, 0)), pl.BlockSpec((1, LANES), lambda p, i: (0, 0))),
        scratch_shapes=[pltpu.VMEM((1, LANES), F32), pltpu.VMEM((1, LANES), F32)],
        compiler_params=_params(32, ("arbitrary", "arbitrary")),
        name="dispatch_plan",
    )(eidx)
    counts = counts[0, :N_EXPERTS]
    pcounts = (counts + MOE_BLOCK - 1) // MOE_BLOCK * MOE_BLOCK
    pend = jnp.cumsum(pcounts)
    blk_start = jnp.arange(n_blk, dtype=jnp.int32) * MOE_BLOCK
    blk_e = jnp.sum((pend[None, :] <= blk_start[:, None]).astype(jnp.int32), axis=1)
    blk_e = jnp.minimum(blk_e, N_EXPERTS - 1).astype(jnp.int32)
    n_used = (pend[-1] // MOE_BLOCK).astype(jnp.int32).reshape(1)
    idx = jnp.arange(n_blk, dtype=jnp.int32)
    first = (idx < n_used[0]) & ((idx == 0) | (blk_e != jnp.roll(blk_e, 1)))
    seg = jnp.cumsum(first.astype(jnp.int32)) - 1
    has_next = first & (seg < jnp.sum(first.astype(jnp.int32)) - 1)
    nxt_e = blk_e[jnp.minimum(pend[blk_e] // MOE_BLOCK, n_blk - 1)]
    ring = jnp.stack([first.astype(jnp.int32), seg % 2, has_next.astype(jnp.int32), nxt_e]).astype(jnp.int32)
    return blk_e, n_used, dest.reshape(-1), counts, ring.reshape(-1)


def _rope_tables():
    pos = jnp.arange(SEQ, dtype=F32)
    inv_freq = ROPE_THETA ** (-jnp.arange(0, HEAD_DIM, 2, dtype=F32) / HEAD_DIM)
    ang = pos[:, None] * inv_freq[None, :]
    cos, sin = jnp.cos(ang), jnp.sin(ang)
    return jnp.concatenate([cos, cos], axis=1), jnp.concatenate([-sin, sin], axis=1)


def kernel(x, c, w_ada, b_ada, norm_mix, norm_ffn, w_in, w_alpha2, b_alpha2, gla_gain, w_out_a, w_out_b, w_out,
           w_router, b_router, w_gate_e, w_up_e, w_down_e, final_norm):
    b, s, d = x.shape
    depth = w_ada.shape[0]
    t = b * s
    n_blk = (t * TOP_K) // MOE_BLOCK + N_EXPERTS
    cosf, sinf = _rope_tables()
    ada = _ada(c, w_ada, b_ada).reshape(depth, b, 6, d)
    w_router_t = jnp.pad(w_router.T, ((0, LANES - N_EXPERTS), (0, 0))).astype(BF16)
    b_router_c = jnp.pad(b_router, (0, LANES - N_EXPERTS)).reshape(LANES, 1)
    final_gain = final_norm.reshape(1, d)
    w_in_t = jnp.swapaxes(w_in, 1, 2)
    w_alow_all = jnp.pad(w_in_t[:, COL_ALOW:COL_GATES, :], ((0, 0), (0, LANES - GLA_LOWRANK), (0, 0))).astype(BF16)
    w_oa, w_ob, w_o = w_out_a.astype(BF16), w_out_b.astype(BF16), w_out.astype(BF16)
    n_exp, _, d_ff = w_gate_e.shape[1:]
    wg_rows = w_gate_e.reshape(depth, n_exp * d, d_ff)
    wu_rows = w_up_e.reshape(depth, n_exp * d, d_ff)
    wd_rows = w_down_e.reshape(depth, n_exp * d_ff, d)
    x2 = x.reshape(t, d)
    h, a_low = _norm_mod(x2, ada[0], norm_mix[0].reshape(1, d), w_alow_all[0])
    for l in range(depth):
        w_alpha2p = jnp.pad(w_alpha2[l], ((0, LANES - GLA_LOWRANK), (0, 0))).astype(BF16)
        proj, w_g, w_u = _in_proj(h, w_in_t, l, 0, COL_ALOW, "in_proj", sides=(wg_rows, wu_rows))
        gates, w_d = _in_proj(h, w_in_t, l, COL_GATES, 2 * D_MODEL, "in_proj_gates", sides=(wd_rows,))
        w_g, w_u, w_d = (w_g.reshape(n_exp, d, d_ff), w_u.reshape(n_exp, d, d_ff), w_d.reshape(n_exp, d_ff, d))
        o_list, l_list = [], []
        for g in range(len(DILATED_GROUPS)):
            o_g, l_g = _attention(proj, cosf, sinf, g)
            o_list.append(o_g)
            l_list.append(l_g)
        y_b = _gla(proj, a_low, w_alpha2p, b_alpha2[l].reshape(1, GLA_KEY), gla_gain[l].reshape(1, GLA_VAL))
        x2, h2, eidx, gw = _post_mixer(
            x2, gates, l, o_list, l_list, y_b, ada[l], w_oa, w_ob, w_o,
            norm_ffn[l].reshape(1, d), w_router_t, b_router_c)
        blk_e, n_used, dest, counts, ring = _dispatch_plan(eidx, n_blk)
        xbuf = _dispatch(dest, counts, h2, n_blk * MOE_BLOCK)
        ybuf = _moe_experts(blk_e, n_used, ring, xbuf, w_g, w_u, w_d)
        if l == depth - 1:
            (x2,) = _combine(dest, ybuf, x2, gw, ada[l], final_gain, ada[l], w_alow_all[l], final=True)
        else:
            x2, h, a_low = _combine(dest, ybuf, x2, gw, ada[l], norm_mix[l + 1].reshape(1, d), ada[l + 1],
                                    w_alow_all[l + 1], final=False)
    return x2.reshape(b, s, d)
```
